```python
import jax, jax.numpy as jnp
from jax import lax
import numpy as np

D_MODEL = 1024
BATCH = 8
SEQ = 4096
DEPTH = 1
DEC_BATCH = 4
DEC_SEQ = 4096
PAST_LEN = 128

HEAD_DIM = 64
GRID_W = 64
NA_HEADS = 8
NA_KH = 8
NA_KW = 16
NA_WIDTH = NA_HEADS * HEAD_DIM
SW_HEADS = 8
SW_KV_HEADS = 2
SW_GROUP = SW_HEADS // SW_KV_HEADS
SW_WIDTH = SW_HEADS * HEAD_DIM
SW_KV_WIDTH = SW_KV_HEADS * HEAD_DIM
WINDOW = 128
WIN_BLOCK = 128
ROPE_THETA = 10000.0
NORM_EPS = 1e-6
SPLIT_SIZES = (NA_WIDTH, NA_WIDTH, NA_WIDTH, NA_WIDTH,
               SW_WIDTH, SW_KV_WIDTH, SW_KV_WIDTH, SW_WIDTH,
               D_MODEL, D_MODEL)
IN_WIDTH = sum(SPLIT_SIZES)

kernel_name = "hybrid_na_swa_gated_encoder"


def rms_norm(x, g):
    xf = x.astype(jnp.float32)
    y = xf * lax.rsqrt(jnp.mean(xf * xf, axis=-1, keepdims=True) + NORM_EPS)
    return (y * g.astype(jnp.float32)).astype(x.dtype)


def rotary(x):
    S, d = x.shape[1], x.shape[-1]
    half = d // 2
    inv = ROPE_THETA ** (-jnp.arange(half, dtype=jnp.float32) / half)
    ang = jnp.arange(S, dtype=jnp.float32)[:, None] * inv[None, :]
    cos = jnp.cos(ang)[None, :, None, :]
    sin = jnp.sin(ang)[None, :, None, :]
    xf = x.astype(jnp.float32)
    x1, x2 = xf[..., :half], xf[..., half:]
    out = jnp.concatenate([x1 * cos - x2 * sin, x2 * cos + x1 * sin], axis=-1)
    return out.astype(x.dtype)


def neighbourhood_attention(q, k, v, rpb):
    B, S, H, d = q.shape
    rows = S // GRID_W
    kh = min(NA_KH, rows)
    r = jnp.arange(rows)
    krow = jnp.clip(r - kh // 2, 0, rows - kh)[:, None] + jnp.arange(kh)[None, :]
    c = jnp.arange(GRID_W)
    cs = jnp.clip(c - NA_KW // 2, 0, GRID_W - NA_KW)
    col_in = (c[None, :] >= cs[:, None]) & (c[None, :] < cs[:, None] + NA_KW)
    dr = krow - r[:, None] + (NA_KH - 1)
    dc = jnp.clip(c[None, :] - c[:, None], -(NA_KW - 1), NA_KW - 1) + (NA_KW - 1)
    bias = rpb[:, dr[:, None, :, None], dc[None, :, None, :]].astype(jnp.float32)
    q5 = q.reshape(B, rows, GRID_W, H, d)
    kg = k.reshape(B, rows, GRID_W, H, d)[:, krow]
    vg = v.reshape(B, rows, GRID_W, H, d)[:, krow]
    s = jnp.einsum('brqhd,brkwhd->bhrqkw', q5, kg).astype(jnp.float32) * (d ** -0.5)
    s = s + bias[None]
    s = jnp.where(col_in[:, None, :], s, -jnp.inf)
    p = jax.nn.softmax(s, axis=(-2, -1))
    o = jnp.einsum('bhrqkw,brkwhd->brqhd', p.astype(v.dtype), vg)
    return o.reshape(B, S, H * d)


def window_attention(q, k, v, sink):
    B, S, H, d = q.shape
    kvh = k.shape[2]
    g = H // kvh
    nb = S // WIN_BLOCK
    pad = ((0, 0), (WIN_BLOCK, WIN_BLOCK), (0, 0), (0, 0))
    kp = jnp.pad(k, pad).reshape(B, nb + 2, WIN_BLOCK, kvh, d)
    vp = jnp.pad(v, pad).reshape(B, nb + 2, WIN_BLOCK, kvh, d)
    kb = jnp.concatenate([kp[:, :-2], kp[:, 1:-1], kp[:, 2:]], axis=2)
    vb = jnp.concatenate([vp[:, :-2], vp[:, 1:-1], vp[:, 2:]], axis=2)
    qb = q.reshape(B, nb, WIN_BLOCK, kvh, g, d)
    s = jnp.einsum('bnqkgd,bnckd->bnkgqc', qb, kb).astype(jnp.float32) * (d ** -0.5)
    qi = jnp.arange(WIN_BLOCK)
    kj = jnp.arange(3 * WIN_BLOCK)
    rel = kj[None, :] - WIN_BLOCK - qi[:, None]
    kpos = jnp.arange(nb)[:, None] * WIN_BLOCK + kj[None, :] - WIN_BLOCK
    mask = (jnp.abs(rel) <= WINDOW)[None] & ((kpos >= 0) & (kpos < S))[:, None, :]
    s = jnp.where(mask[None, :, None, None], s, -jnp.inf)
    sk = sink.reshape(kvh, g).astype(jnp.float32)[None, None, :, :, None, None]
    m = jnp.maximum(jnp.max(s, axis=-1, keepdims=True), sk)
    p = jnp.exp(s - m)
    l = jnp.sum(p, axis=-1, keepdims=True) + jnp.exp(sk - m)
    o = jnp.einsum('bnkgqc,bnckd->bnqkgd', (p / l).astype(v.dtype), vb)
    return o.reshape(B, S, H * d)


def layer(x, norm_g, w_in, qn_a, kn_a, rpb_a, qn_b, kn_b, sink_b, w_out_a, w_out_b, w_o):
    B, S, _ = x.shape
    h = rms_norm(x, norm_g)
    proj = h @ w_in
    offsets = []
    acc = 0
    for n in SPLIT_SIZES[:-1]:
        acc += n
        offsets.append(acc)
    q_a, k_a, v_a, z_a, q_b, k_b, v_b, z_b, g_a, g_b = jnp.split(proj, offsets, axis=-1)
    q_a = rms_norm(q_a.reshape(B, S, NA_HEADS, HEAD_DIM), qn_a)
    k_a = rms_norm(k_a.reshape(B, S, NA_HEADS, HEAD_DIM), kn_a)
    v_a = v_a.reshape(B, S, NA_HEADS, HEAD_DIM)
    y_a = neighbourhood_attention(q_a, k_a, v_a, rpb_a) * jax.nn.silu(z_a)
    q_b = rotary(rms_norm(q_b.reshape(B, S, SW_HEADS, HEAD_DIM), qn_b))
    k_b = rotary(rms_norm(k_b.reshape(B, S, SW_KV_HEADS, HEAD_DIM), kn_b))
    v_b = v_b.reshape(B, S, SW_KV_HEADS, HEAD_DIM)
    y_b = window_attention(q_b, k_b, v_b, sink_b) * jax.nn.silu(z_b)
    merged = jax.nn.sigmoid(g_a) * (y_a @ w_out_a) + jax.nn.sigmoid(g_b) * (y_b @ w_out_b)
    return x + merged @ w_o


def setup_inputs(seed: int = 0) -> dict:
    key = jax.random.key(seed)
    ks = jax.random.split(key, 14)
    f = jnp.float32
    return {
        "x_prompt": jax.random.normal(ks[0], (BATCH, SEQ, D_MODEL), f),
        "x_sample": jax.random.normal(ks[1], (DEC_BATCH, DEC_SEQ, D_MODEL), f),
        "norm_g": 1.0 + 0.05 * jax.random.normal(ks[2], (DEPTH, D_MODEL), f),
        "w_in": jax.random.normal(ks[3], (DEPTH, D_MODEL, IN_WIDTH), f) * D_MODEL ** -0.5,
        "qn_a": 1.0 + 0.05 * jax.random.normal(ks[4], (DEPTH, HEAD_DIM), f),
        "kn_a": 1.0 + 0.05 * jax.random.normal(ks[5], (DEPTH, HEAD_DIM), f),
        "rpb_a": 0.1 * jax.random.normal(ks[6], (DEPTH, NA_HEADS, 2 * NA_KH - 1, 2 * NA_KW - 1), f),
        "qn_b": 1.0 + 0.05 * jax.random.normal(ks[7], (DEPTH, HEAD_DIM), f),
        "kn_b": 1.0 + 0.05 * jax.random.normal(ks[8], (DEPTH, HEAD_DIM), f),
        "sink_b": 0.5 * jax.random.normal(ks[9], (DEPTH, SW_HEADS), f),
        "w_out_a": jax.random.normal(ks[10], (DEPTH, NA_WIDTH, D_MODEL), f) * NA_WIDTH ** -0.5,
        "w_out_b": jax.random.normal(ks[11], (DEPTH, SW_WIDTH, D_MODEL), f) * SW_WIDTH ** -0.5,
        "w_o": jax.random.normal(ks[12], (DEPTH, D_MODEL, D_MODEL), f) * D_MODEL ** -0.5,
    }


def reference(x_prompt, x_sample, norm_g, w_in, qn_a, kn_a, rpb_a, qn_b, kn_b, sink_b, w_out_a, w_out_b, w_o):
    y_prompt = x_prompt
    y_sample = x_sample
    for l in range(DEPTH):
        y_prompt = layer(y_prompt, norm_g[l], w_in[l], qn_a[l], kn_a[l], rpb_a[l], qn_b[l], kn_b[l],
                         sink_b[l], w_out_a[l], w_out_b[l], w_o[l])
        y_sample = layer(y_sample, norm_g[l], w_in[l], qn_a[l], kn_a[l], rpb_a[l], qn_b[l], kn_b[l],
                         sink_b[l], w_out_a[l], w_out_b[l], w_o[l])
    return (y_prompt, y_sample)
```

```python
import functools

import jax
import jax.numpy as jnp
from jax import lax
from jax.experimental import pallas as pl
from jax.experimental.pallas import tpu as pltpu

D_MODEL = 1024
HEAD_DIM = 64
GRID_W = 64
NA_HEADS = 8
NA_KH = 8
NA_KW = 16
NA_WIDTH = NA_HEADS * HEAD_DIM
SW_HEADS = 8
SW_KV_HEADS = 2
SW_GROUP = SW_HEADS // SW_KV_HEADS
SW_WIDTH = SW_HEADS * HEAD_DIM
SW_KV_WIDTH = SW_KV_HEADS * HEAD_DIM
WINDOW = 128
ROPE_THETA = 10000.0
NORM_EPS = 1e-6

C_QA = 0
C_KA = C_QA + NA_WIDTH
C_VA = C_KA + NA_WIDTH
C_ZA = C_VA + NA_WIDTH
C_QB = C_ZA + NA_WIDTH
C_KB = C_QB + SW_WIDTH
C_VB = C_KB + SW_KV_WIDTH
C_ZB = C_VB + SW_KV_WIDTH
C_GA = C_ZB + SW_WIDTH
C_GB = C_GA + D_MODEL
IN_WIDTH = C_GB + D_MODEL

LANES = 128
MXU_DIM = 256
HEADS_PER_MXU = MXU_DIM // HEAD_DIM
NEG_BIG = -1e30
VMEM_LIMIT = 56 * 1024 * 1024

PROJ_TILE = 512
ATT_TILE = 512
ATT_ROWS = ATT_TILE // GRID_W
NA_HALO = 256
SW_BLOCK = 128
SW_HALO = WINDOW


def _segment_mean_sq(t, bd):
    w = t.shape[1]
    sq = (t * t).astype(jnp.bfloat16)
    outs = []
    for c in range(0, w, MXU_DIM):
        cw = min(MXU_DIM, w - c)
        outs.append(jnp.dot(sq[:, c:c + cw], bd[:cw, :cw], preferred_element_type=jnp.float32))
    return outs[0] if len(outs) == 1 else jnp.concatenate(outs, axis=1)


def _swap_halves(t):
    w = t.shape[1]
    lane = lax.broadcasted_iota(jnp.int32, (t.shape[0], LANES), 1)
    low = (lane % HEAD_DIM) < (HEAD_DIM // 2)
    outs = []
    for c in range(0, w, LANES):
        blk = t[:, c:c + LANES]
        fwd = pltpu.roll(blk, HEAD_DIM // 2, 1)
        bwd = pltpu.roll(blk, LANES - HEAD_DIM // 2, 1)
        outs.append(jnp.where(low, bwd, fwd))
    return outs[0] if len(outs) == 1 else jnp.concatenate(outs, axis=1)


def _tile_lanes(t, reps):
    return t if reps == 1 else jnp.concatenate([t] * reps, axis=1)


def _proj_kernel(x_ref, g_ref, w_ref, bd_ref, qna_ref, kna_ref, qnb_ref, knb_ref, cos_ref, sin_ref,
                 qa_o, ka_o, va_o, sza_o, qb_o, kb_o, vb_o, szb_o, sga_o, sgb_o):
    f32, bf16 = jnp.float32, jnp.bfloat16
    x = x_ref[0]
    ms = jnp.mean(x * x, axis=-1, keepdims=True)
    h = (x * lax.rsqrt(ms + NORM_EPS) * g_ref[...]).astype(bf16)
    bd = bd_ref[...]
    scale = HEAD_DIM ** -0.5

    def proj(c0, c1):
        return jnp.dot(h, w_ref[:, c0:c1], preferred_element_type=f32)

    def head_norm(t, gain):
        return t * lax.rsqrt(_segment_mean_sq(t, bd) + NORM_EPS) * gain

    def rotary(t):
        reps = t.shape[1] // LANES
        cos = _tile_lanes(cos_ref[...], reps)
        sin = _tile_lanes(sin_ref[...], reps)
        return t * cos + _swap_halves(t) * sin

    qa_o[0] = (head_norm(proj(C_QA, C_KA), qna_ref[...]) * scale).astype(bf16)
    ka_o[0] = head_norm(proj(C_KA, C_VA), kna_ref[...]).astype(bf16)
    va_o[0] = proj(C_VA, C_ZA).astype(bf16)
    sza_o[0] = jax.nn.silu(proj(C_ZA, C_QB)).astype(bf16)
    qb_o[0] = (rotary(head_norm(proj(C_QB, C_KB), qnb_ref[...])) * scale).astype(bf16)

    kvb = proj(C_KB, C_ZB)
    kb = rotary(head_norm(kvb[:, :SW_KV_WIDTH], knb_ref[...]))
    vb = kvb[:, SW_KV_WIDTH:]
    lane = lax.broadcasted_iota(jnp.int32, kb.shape, 1)
    first = lane < HEAD_DIM

    def replicate(t):
        r = pltpu.roll(t, HEAD_DIM, 1)
        h0 = jnp.where(first, t, r)
        h1 = jnp.where(first, r, t)
        return jnp.concatenate([h0, h0, h1, h1], axis=1)

    kb_o[0] = replicate(kb).astype(bf16)
    vb_o[0] = replicate(vb).astype(bf16)
    szb_o[0] = jax.nn.silu(proj(C_ZB, C_GA)).astype(bf16)
    sga_o[0] = jax.nn.sigmoid(proj(C_GA, C_GB)).astype(bf16)
    sgb_o[0] = jax.nn.sigmoid(proj(C_GB, IN_WIDTH)).astype(bf16)


def _const_spec(shape):
    nd = len(shape)
    return pl.BlockSpec(shape, lambda b, i: (0,) * nd)


def _projection(x, g, w_bf, bd, qna, kna, qnb, knb, cos, sin):
    B, S, _ = x.shape
    tm = PROJ_TILE
    grid = (B, S // tm)
    tok = lambda w: pl.BlockSpec((1, tm, w), lambda b, i: (b, i, 0))
    out_widths = [NA_WIDTH, NA_WIDTH, NA_WIDTH, NA_WIDTH, SW_WIDTH, SW_WIDTH, SW_WIDTH, SW_WIDTH,
                  D_MODEL, D_MODEL]
    return pl.pallas_call(
        _proj_kernel,
        grid=grid,
        in_specs=[
            tok(D_MODEL),
            _const_spec((1, D_MODEL)),
            _const_spec((D_MODEL, IN_WIDTH)),
            _const_spec((MXU_DIM, MXU_DIM)),
            _const_spec((1, NA_WIDTH)),
            _const_spec((1, NA_WIDTH)),
            _const_spec((1, SW_WIDTH)),
            _const_spec((1, SW_KV_WIDTH)),
            pl.BlockSpec((tm, LANES), lambda b, i: (i, 0)),
            pl.BlockSpec((tm, LANES), lambda b, i: (i, 0)),
        ],
        out_specs=[tok(w) for w in out_widths],
        out_shape=[jax.ShapeDtypeStruct((B, S, w), jnp.bfloat16) for w in out_widths],
        compiler_params=pltpu.CompilerParams(
            dimension_semantics=("arbitrary", "arbitrary"), vmem_limit_bytes=VMEM_LIMIT),
        name="proj",
    )(x, g, w_bf, bd, qna, kna, qnb, knb, cos, sin)


def _stack_heads(q, hm_ref):
    rows = q.shape[0]
    return jnp.concatenate([q * hm_ref[hh, :rows, :] for hh in range(HEADS_PER_MXU)], axis=0)


def _take_diag(r, inv_l, rows):
    lane = lax.broadcasted_iota(jnp.int32, (rows, MXU_DIM), 1) // HEAD_DIM
    rn = r * inv_l
    out = rn[(HEADS_PER_MXU - 1) * rows:]
    for hh in range(HEADS_PER_MXU - 2, -1, -1):
        out = jnp.where(lane == hh, rn[hh * rows:(hh + 1) * rows], out)
    return out


def _attn_kernel(qa_ref, sza_ref, qb_ref, szb_ref, sga_ref, sgb_ref, x_ref,
                 kap_ref, kac_ref, kan_ref, vap_ref, vac_ref, van_ref,
                 kbp_ref, kbc_ref, kbn_ref, vbp_ref, vbc_ref, vbn_ref,
                 pb_ref, swm_ref, sink_ref, hm_ref, woa_ref, wob_ref, wo_ref,
                 y_ref, kaext, vaext, kbext, vbext, oa_sc, ob_sc, *, rows_total, blocks_total):
    f32, bf16 = jnp.float32, jnp.bfloat16
    i = pl.program_id(1)
    nt = (((1,), (1,)), ((), ()))

    kaext[0:NA_HALO] = kap_ref[0]
    kaext[NA_HALO:NA_HALO + ATT_TILE] = kac_ref[0]
    kaext[NA_HALO + ATT_TILE:] = kan_ref[0]
    vaext[0:NA_HALO] = vap_ref[0]
    vaext[NA_HALO:NA_HALO + ATT_TILE] = vac_ref[0]
    vaext[NA_HALO + ATT_TILE:] = van_ref[0]
    kbext[0:SW_HALO] = kbp_ref[0]
    kbext[SW_HALO:SW_HALO + ATT_TILE] = kbc_ref[0]
    kbext[SW_HALO + ATT_TILE:] = kbn_ref[0]
    vbext[0:SW_HALO] = vbp_ref[0]
    vbext[SW_HALO:SW_HALO + ATT_TILE] = vbc_ref[0]
    vbext[SW_HALO + ATT_TILE:] = vbn_ref[0]

    kh = min(NA_KH, rows_total)
    ext_row0 = i * ATT_ROWS - NA_HALO // GRID_W
    win = kh * GRID_W

    def na_row(rl, carry):
        r = i * ATT_ROWS + rl
        krow0 = jnp.clip(r - kh // 2, 0, rows_total - kh)
        off = pl.multiple_of((krow0 - ext_row0) * GRID_W, GRID_W)
        d0 = krow0 - r + (NA_KH - 1)
        q0 = pl.multiple_of(rl * GRID_W, GRID_W)
        for g in range(NA_HEADS // HEADS_PER_MXU):
            cols = slice(g * MXU_DIM, (g + 1) * MXU_DIM)
            qs = _stack_heads(qa_ref[0, pl.ds(q0, GRID_W), cols], hm_ref)
            s = lax.dot_general(qs, kaext[pl.ds(off, win), cols], nt,
                                preferred_element_type=f32)
            sb = [s[:, m * LANES:(m + 1) * LANES] + pb_ref[d0 + 2 * m, g]
                  for m in range(win // LANES)]
            mx = functools.reduce(jnp.maximum, sb)
            mx = jnp.max(mx, axis=-1, keepdims=True)
            p = [jnp.exp(t - mx) for t in sb]
            l = jnp.sum(functools.reduce(jnp.add, p), axis=-1, keepdims=True)
            pc = jnp.concatenate(p, axis=1).astype(bf16)
            rr = jnp.dot(pc, vaext[pl.ds(off, win), cols], preferred_element_type=f32)
            oa_sc[pl.ds(q0, GRID_W), cols] = _take_diag(rr, 1.0 / l, GRID_W)
        return carry

    lax.fori_loop(0, ATT_ROWS, na_row, 0)

    kwin = SW_BLOCK + 2 * SW_HALO
    for nl in range(ATT_TILE // SW_BLOCK):
        gblk = i * (ATT_TILE // SW_BLOCK) + nl
        var = jnp.where(gblk == 0, 0, jnp.where(gblk == blocks_total - 1, 2, 1))
        mask = swm_ref[var]
        mask = jnp.concatenate([mask] * SW_GROUP, axis=0)
        for j in range(SW_KV_HEADS):
            cols = slice(j * MXU_DIM, (j + 1) * MXU_DIM)
            qs = _stack_heads(qb_ref[0, nl * SW_BLOCK:(nl + 1) * SW_BLOCK, cols], hm_ref)
            s = lax.dot_general(qs, kbext[nl * SW_BLOCK:nl * SW_BLOCK + kwin, cols], nt,
                                preferred_element_type=f32) + mask
            sk = sink_ref[j]
            mx = jnp.maximum(jnp.max(s, axis=-1, keepdims=True), sk)
            p = jnp.exp(s - mx)
            l = jnp.sum(p, axis=-1, keepdims=True) + jnp.exp(sk - mx)
            rr = jnp.dot(p.astype(bf16), vbext[nl * SW_BLOCK:nl * SW_BLOCK + kwin, cols],
                         preferred_element_type=f32)
            ob_sc[nl * SW_BLOCK:(nl + 1) * SW_BLOCK, cols] = _take_diag(rr, 1.0 / l, SW_BLOCK)

    ya = (oa_sc[...] * sza_ref[0].astype(f32)).astype(bf16)
    yb = (ob_sc[...] * szb_ref[0].astype(f32)).astype(bf16)
    pa = jnp.dot(ya, woa_ref[...], preferred_element_type=f32)
    pbm = jnp.dot(yb, wob_ref[...], preferred_element_type=f32)
    merged = (sga_ref[0].astype(f32) * pa + sgb_ref[0].astype(f32) * pbm).astype(bf16)
    y_ref[0] = x_ref[0] + jnp.dot(merged, wo_ref[...], preferred_element_type=f32)


def _attention(x, qa, ka, va, sza, qb, kb, vb, szb, sga, sgb, pb, swm, sink_col, hm, woa, wob, wo):
    B, S, _ = x.shape
    tq = ATT_TILE
    grid = (B, S // tq)
    n_na = S // NA_HALO
    n_sw = S // SW_HALO
    na_per = tq // NA_HALO
    sw_per = tq // SW_HALO

    tok = lambda w: pl.BlockSpec((1, tq, w), lambda b, i: (b, i, 0))

    def halo(width, blk, per, nblk):
        prev = pl.BlockSpec((1, blk, width), lambda b, i: (b, jnp.maximum(i * per - 1, 0), 0))
        nxt = pl.BlockSpec((1, blk, width), lambda b, i: (b, jnp.minimum((i + 1) * per, nblk - 1), 0))
        return prev, tok(width), nxt

    ka_specs = halo(NA_WIDTH, NA_HALO, na_per, n_na)
    kb_specs = halo(SW_WIDTH, SW_HALO, sw_per, n_sw)

    kern = functools.partial(_attn_kernel, rows_total=S // GRID_W, blocks_total=S // SW_BLOCK)
    return pl.pallas_call(
        kern,
        grid=grid,
        in_specs=[
            tok(NA_WIDTH), tok(NA_WIDTH), tok(SW_WIDTH), tok(SW_WIDTH), tok(D_MODEL), tok(D_MODEL),
            tok(D_MODEL),
            *ka_specs, *ka_specs, *kb_specs, *kb_specs,
            _const_spec(pb.shape), _const_spec(swm.shape), _const_spec(sink_col.shape),
            _const_spec(hm.shape),
            _const_spec(woa.shape), _const_spec(wob.shape), _const_spec(wo.shape),
        ],
        out_specs=tok(D_MODEL),
        out_shape=jax.ShapeDtypeStruct((B, S, D_MODEL), jnp.float32),
        scratch_shapes=[
            pltpu.VMEM((tq + 2 * NA_HALO, NA_WIDTH), jnp.bfloat16),
            pltpu.VMEM((tq + 2 * NA_HALO, NA_WIDTH), jnp.bfloat16),
            pltpu.VMEM((tq + 2 * SW_HALO, SW_WIDTH), jnp.bfloat16),
            pltpu.VMEM((tq + 2 * SW_HALO, SW_WIDTH), jnp.bfloat16),
            pltpu.VMEM((tq, NA_WIDTH), jnp.float32),
            pltpu.VMEM((tq, SW_WIDTH), jnp.float32),
        ],
        compiler_params=pltpu.CompilerParams(
            dimension_semantics=("arbitrary", "arbitrary"), vmem_limit_bytes=VMEM_LIMIT),
        name="attn",
    )(qa, sza, qb, szb, sga, sgb, x, ka, ka, ka, va, va, va, kb, kb, kb, vb, vb, vb,
      pb, swm, sink_col, hm, woa, wob, wo)


def _na_bias_table(rpb):
    c = jnp.arange(GRID_W)
    cs = jnp.clip(c - NA_KW // 2, 0, GRID_W - NA_KW)
    col_in = (c[None, :] >= cs[:, None]) & (c[None, :] < cs[:, None] + NA_KW)
    dc = jnp.clip(c[None, :] - c[:, None], -(NA_KW - 1), NA_KW - 1) + (NA_KW - 1)
    b = rpb[:, :, dc].astype(jnp.float32)
    b = jnp.where(col_in[None, None], b, NEG_BIG)
    nd = 2 * NA_KH - 2
    pair = jnp.concatenate([b[:, :nd], b[:, 1:nd + 1]], axis=-1)
    pair = pair.reshape(NA_HEADS // HEADS_PER_MXU, HEADS_PER_MXU, nd, GRID_W, LANES)
    return pair.transpose(2, 0, 1, 3, 4).reshape(nd, NA_HEADS // HEADS_PER_MXU,
                                                 HEADS_PER_MXU * GRID_W, LANES)


def _sw_mask_table():
    qi = jnp.arange(SW_BLOCK)[:, None]
    kj = jnp.arange(SW_BLOCK + 2 * SW_HALO)[None, :]
    band = jnp.abs(kj - SW_HALO - qi) <= WINDOW
    first = band & (kj >= SW_HALO)
    last = band & (kj < SW_HALO + SW_BLOCK)
    m = jnp.stack([first, band, last])
    return jnp.where(m, 0.0, NEG_BIG).astype(jnp.float32)


def _layer(x, p):
    S = x.shape[1]
    outs = _projection(x, p["g"], p["w_in"], p["bd"], p["qna"], p["kna"], p["qnb"], p["knb"],
                       p["cos"][:S], p["sin"][:S])
    qa, ka, va, sza, qb, kb, vb, szb, sga, sgb = outs
    return _attention(x, qa, ka, va, sza, qb, kb, vb, szb, sga, sgb, p["pb"], p["swm"], p["sink"],
                      p["hm"], p["woa"], p["wob"], p["wo"])


def _prepare(S, norm_g, w_in, qn_a, kn_a, rpb_a, qn_b, kn_b, sink_b, w_out_a, w_out_b, w_o):
    f32, bf16 = jnp.float32, jnp.bfloat16
    seg = jnp.arange(MXU_DIM) // HEAD_DIM
    bd = jnp.where(seg[:, None] == seg[None, :], 1.0 / HEAD_DIM, 0.0).astype(bf16)
    half = HEAD_DIM // 2
    inv = ROPE_THETA ** (-jnp.arange(half, dtype=f32) / half)
    ang = jnp.arange(S, dtype=f32)[:, None] * inv[None, :]
    cos, sin = jnp.cos(ang), jnp.sin(ang)
    cos = jnp.tile(jnp.concatenate([cos, cos], axis=1), (1, LANES // HEAD_DIM))
    sin = jnp.tile(jnp.concatenate([-sin, sin], axis=1), (1, LANES // HEAD_DIM))
    lane_head = jnp.arange(MXU_DIM) // HEAD_DIM
    hm = (lane_head[None, None, :] == jnp.arange(HEADS_PER_MXU)[:, None, None])
    hm = jnp.broadcast_to(hm, (HEADS_PER_MXU, SW_BLOCK, MXU_DIM)).astype(bf16)
    sink_col = jnp.repeat(sink_b.astype(f32).reshape(SW_KV_HEADS, SW_GROUP), SW_BLOCK, axis=1)
    return {
        "g": norm_g.astype(f32).reshape(1, D_MODEL),
        "w_in": w_in.astype(bf16),
        "bd": bd,
        "qna": jnp.tile(qn_a.astype(f32), NA_HEADS).reshape(1, NA_WIDTH),
        "kna": jnp.tile(kn_a.astype(f32), NA_HEADS).reshape(1, NA_WIDTH),
        "qnb": jnp.tile(qn_b.astype(f32), SW_HEADS).reshape(1, SW_WIDTH),
        "knb": jnp.tile(kn_b.astype(f32), SW_KV_HEADS).reshape(1, SW_KV_WIDTH),
        "cos": cos, "sin": sin,
        "pb": _na_bias_table(rpb_a),
        "swm": _sw_mask_table(),
        "sink": sink_col.reshape(SW_KV_HEADS, SW_GROUP * SW_BLOCK, 1),
        "hm": hm,
        "woa": w_out_a.astype(bf16), "wob": w_out_b.astype(bf16), "wo": w_o.astype(bf16),
    }


def kernel(x_prompt, x_sample, norm_g, w_in, qn_a, kn_a, rpb_a, qn_b, kn_b, sink_b, w_out_a, w_out_b, w_o):
    depth = norm_g.shape[0]
    y_prompt, y_sample = x_prompt, x_sample
    S = max(x_prompt.shape[1], x_sample.shape[1])
    for l in range(depth):
        p = _prepare(S, norm_g[l], w_in[l], qn_a[l], kn_a[l], rpb_a[l], qn_b[l], kn_b[l], sink_b[l],
                     w_out_a[l], w_out_b[l], w_o[l])
        y_prompt = _layer(y_prompt, p)
        y_sample = _layer(y_sample, p)
    return (y_prompt, y_sample)
```

```python
import functools

import jax
import jax.numpy as jnp
from jax import lax
from jax.experimental import pallas as pl
from jax.experimental.pallas import tpu as pltpu

D_MODEL = 1024
HEAD_DIM = 64
GRID_W = 64
NA_HEADS = 8
NA_KH = 8
NA_KW = 16
NA_WIDTH = NA_HEADS * HEAD_DIM
SW_HEADS = 8
SW_KV_HEADS = 2
SW_GROUP = SW_HEADS // SW_KV_HEADS
SW_WIDTH = SW_HEADS * HEAD_DIM
SW_KV_WIDTH = SW_KV_HEADS * HEAD_DIM
WINDOW = 128
ROPE_THETA = 10000.0
NORM_EPS = 1e-6

C_QA = 0
C_KA = C_QA + NA_WIDTH
C_VA = C_KA + NA_WIDTH
C_ZA = C_VA + NA_WIDTH
C_QB = C_ZA + NA_WIDTH
C_KB = C_QB + SW_WIDTH
C_VB = C_KB + SW_KV_WIDTH
C_ZB = C_VB + SW_KV_WIDTH
C_GA = C_ZB + SW_WIDTH
C_GB = C_GA + D_MODEL
IN_WIDTH = C_GB + D_MODEL

LANES = 128
MXU_DIM = 256
HEADS_PER_MXU = MXU_DIM // HEAD_DIM
NEG_BIG = -1e30
VMEM_LIMIT = 56 * 1024 * 1024

PROJ_TILE = 512
ATT_TILE = 512
ATT_ROWS = ATT_TILE // GRID_W
NA_HALO = 256
SW_BLOCK = 128
SW_HALO = WINDOW


def _segment_mean_sq(t, bd):
    w = t.shape[1]
    sq = (t * t).astype(jnp.bfloat16)
    outs = []
    for c in range(0, w, MXU_DIM):
        cw = min(MXU_DIM, w - c)
        outs.append(jnp.dot(sq[:, c:c + cw], bd[:cw, :cw], preferred_element_type=jnp.float32))
    return outs[0] if len(outs) == 1 else jnp.concatenate(outs, axis=1)


def _swap_halves(t):
    w = t.shape[1]
    lane = lax.broadcasted_iota(jnp.int32, (t.shape[0], LANES), 1)
    low = (lane % HEAD_DIM) < (HEAD_DIM // 2)
    outs = []
    for c in range(0, w, LANES):
        blk = t[:, c:c + LANES]
        fwd = pltpu.roll(blk, HEAD_DIM // 2, 1)
        bwd = pltpu.roll(blk, LANES - HEAD_DIM // 2, 1)
        outs.append(jnp.where(low, bwd, fwd))
    return outs[0] if len(outs) == 1 else jnp.concatenate(outs, axis=1)


def _tile_lanes(t, reps):
    return t if reps == 1 else jnp.concatenate([t] * reps, axis=1)


def _proj_kernel(x_ref, g_ref, w_ref, bd_ref, qna_ref, kna_ref, qnb_ref, knb_ref, cos_ref, sin_ref,
                 qa_o, ka_o, va_o, sza_o, qb_o, kb_o, vb_o, szb_o, sga_o, sgb_o):
    f32, bf16 = jnp.float32, jnp.bfloat16
    x = x_ref[0]
    ms = jnp.mean(x * x, axis=-1, keepdims=True)
    h = (x * lax.rsqrt(ms + NORM_EPS) * g_ref[...]).astype(bf16)
    bd = bd_ref[...]
    scale = HEAD_DIM ** -0.5

    def proj(c0, c1):
        return jnp.dot(h, w_ref[:, c0:c1], preferred_element_type=f32)

    def head_norm(t, gain):
        return t * lax.rsqrt(_segment_mean_sq(t, bd) + NORM_EPS) * gain

    def rotary(t):
        reps = t.shape[1] // LANES
        cos = _tile_lanes(cos_ref[...], reps)
        sin = _tile_lanes(sin_ref[...], reps)
        return t * cos + _swap_halves(t) * sin

    qa_o[0] = (head_norm(proj(C_QA, C_KA), qna_ref[...]) * scale).astype(bf16)
    ka_o[0] = head_norm(proj(C_KA, C_VA), kna_ref[...]).astype(bf16)
    va_o[0] = proj(C_VA, C_ZA).astype(bf16)
    sza_o[0] = jax.nn.silu(proj(C_ZA, C_QB)).astype(bf16)
    qb_o[0] = (rotary(head_norm(proj(C_QB, C_KB), qnb_ref[...])) * scale).astype(bf16)

    kvb = proj(C_KB, C_ZB)
    kb = rotary(head_norm(kvb[:, :SW_KV_WIDTH], knb_ref[...]))
    vb = kvb[:, SW_KV_WIDTH:]
    lane = lax.broadcasted_iota(jnp.int32, kb.shape, 1)
    first = lane < HEAD_DIM

    def replicate(t):
        r = pltpu.roll(t, HEAD_DIM, 1)
        h0 = jnp.where(first, t, r)
        h1 = jnp.where(first, r, t)
        return jnp.concatenate([h0, h0, h1, h1], axis=1)

    kb_o[0] = replicate(kb).astype(bf16)
    vb_o[0] = replicate(vb).astype(bf16)
    szb_o[0] = jax.nn.silu(proj(C_ZB, C_GA)).astype(bf16)
    sga_o[0] = jax.nn.sigmoid(proj(C_GA, C_GB)).astype(bf16)
    sgb_o[0] = jax.nn.sigmoid(proj(C_GB, IN_WIDTH)).astype(bf16)


def _const_spec(shape):
    nd = len(shape)
    return pl.BlockSpec(shape, lambda b, i: (0,) * nd)


def _projection(x, g, w_bf, bd, qna, kna, qnb, knb, cos, sin):
    B, S, _ = x.shape
    tm = PROJ_TILE
    grid = (B, S // tm)
    tok = lambda w: pl.BlockSpec((1, tm, w), lambda b, i: (b, i, 0))
    out_widths = [NA_WIDTH, NA_WIDTH, NA_WIDTH, NA_WIDTH, SW_WIDTH, SW_WIDTH, SW_WIDTH, SW_WIDTH,
                  D_MODEL, D_MODEL]
    return pl.pallas_call(
        _proj_kernel,
        grid=grid,
        in_specs=[
            tok(D_MODEL),
            _const_spec((1, D_MODEL)),
            _const_spec((D_MODEL, IN_WIDTH)),
            _const_spec((MXU_DIM, MXU_DIM)),
            _const_spec((1, NA_WIDTH)),
            _const_spec((1, NA_WIDTH)),
            _const_spec((1, SW_WIDTH)),
            _const_spec((1, SW_KV_WIDTH)),
            pl.BlockSpec((tm, LANES), lambda b, i: (i, 0)),
            pl.BlockSpec((tm, LANES), lambda b, i: (i, 0)),
        ],
        out_specs=[tok(w) for w in out_widths],
        out_shape=[jax.ShapeDtypeStruct((B, S, w), jnp.bfloat16) for w in out_widths],
        compiler_params=pltpu.CompilerParams(
            dimension_semantics=("arbitrary", "arbitrary"), vmem_limit_bytes=VMEM_LIMIT),
        name="proj",
    )(x, g, w_bf, bd, qna, kna, qnb, knb, cos, sin)


def _stack_heads(q, hm_ref):
    rows = q.shape[0]
    return jnp.concatenate([q * hm_ref[hh, :rows, :] for hh in range(HEADS_PER_MXU)], axis=0)


def _take_diag(r, inv_l, rows):
    lane = lax.broadcasted_iota(jnp.int32, (rows, MXU_DIM), 1) // HEAD_DIM
    rn = r * inv_l
    out = rn[(HEADS_PER_MXU - 1) * rows:]
    for hh in range(HEADS_PER_MXU - 2, -1, -1):
        out = jnp.where(lane == hh, rn[hh * rows:(hh + 1) * rows], out)
    return out


def _attn_kernel(qa_ref, sza_ref, qb_ref, szb_ref, sga_ref, sgb_ref, x_ref,
                 kap_ref, kac_ref, kan_ref, vap_ref, vac_ref, van_ref,
                 kbp_ref, kbc_ref, kbn_ref, vbp_ref, vbc_ref, vbn_ref,
                 pb_ref, swm_ref, sink_ref, hm_ref, woa_ref, wob_ref, wo_ref,
                 y_ref, kaext, vaext, kbext, vbext, oa_sc, ob_sc, *, rows_total, blocks_total):
    f32, bf16 = jnp.float32, jnp.bfloat16
    i = pl.program_id(1)
    nt = (((1,), (1,)), ((), ()))

    kaext[0:NA_HALO] = kap_ref[0]
    kaext[NA_HALO:NA_HALO + ATT_TILE] = kac_ref[0]
    kaext[NA_HALO + ATT_TILE:] = kan_ref[0]
    vaext[0:NA_HALO] = vap_ref[0]
    vaext[NA_HALO:NA_HALO + ATT_TILE] = vac_ref[0]
    vaext[NA_HALO + ATT_TILE:] = van_ref[0]
    kbext[0:SW_HALO] = kbp_ref[0]
    kbext[SW_HALO:SW_HALO + ATT_TILE] = kbc_ref[0]
    kbext[SW_HALO + ATT_TILE:] = kbn_ref[0]
    vbext[0:SW_HALO] = vbp_ref[0]
    vbext[SW_HALO:SW_HALO + ATT_TILE] = vbc_ref[0]
    vbext[SW_HALO + ATT_TILE:] = vbn_ref[0]

    kh = min(NA_KH, rows_total)
    ext_row0 = i * ATT_ROWS - NA_HALO // GRID_W
    win = kh * GRID_W

    for rl in range(ATT_ROWS):
        r = i * ATT_ROWS + rl
        krow0 = jnp.clip(r - kh // 2, 0, rows_total - kh)
        off = pl.multiple_of((krow0 - ext_row0) * GRID_W, GRID_W)
        d0 = krow0 - r + (NA_KH - 1)
        q0 = rl * GRID_W
        for g in range(NA_HEADS // HEADS_PER_MXU):
            cols = slice(g * MXU_DIM, (g + 1) * MXU_DIM)
            qs = _stack_heads(qa_ref[0, q0:q0 + GRID_W, cols], hm_ref)
            s = lax.dot_general(qs, kaext[pl.ds(off, win), cols], nt,
                                preferred_element_type=f32)
            sb = [s[:, m * LANES:(m + 1) * LANES] + pb_ref[d0 + 2 * m, g]
                  for m in range(win // LANES)]
            mx = functools.reduce(jnp.maximum, sb)
            mx = jnp.max(mx, axis=-1, keepdims=True)
            p = [jnp.exp(t - mx) for t in sb]
            l = jnp.sum(functools.reduce(jnp.add, p), axis=-1, keepdims=True)
            pc = jnp.concatenate(p, axis=1).astype(bf16)
            rr = jnp.dot(pc, vaext[pl.ds(off, win), cols], preferred_element_type=f32)
            oa_sc[q0:q0 + GRID_W, cols] = _take_diag(rr, 1.0 / l, GRID_W)

    kwin = SW_BLOCK + 2 * SW_HALO
    for nl in range(ATT_TILE // SW_BLOCK):
        gblk = i * (ATT_TILE // SW_BLOCK) + nl
        var = jnp.where(gblk == 0, 0, jnp.where(gblk == blocks_total - 1, 2, 1))
        mask = swm_ref[var]
        mask = jnp.concatenate([mask] * SW_GROUP, axis=0)
        for j in range(SW_KV_HEADS):
            cols = slice(j * MXU_DIM, (j + 1) * MXU_DIM)
            qs = _stack_heads(qb_ref[0, nl * SW_BLOCK:(nl + 1) * SW_BLOCK, cols], hm_ref)
            s = lax.dot_general(qs, kbext[nl * SW_BLOCK:nl * SW_BLOCK + kwin, cols], nt,
                                preferred_element_type=f32) + mask
            sk = sink_ref[j]
            mx = jnp.maximum(jnp.max(s, axis=-1, keepdims=True), sk)
            p = jnp.exp(s - mx)
            l = jnp.sum(p, axis=-1, keepdims=True) + jnp.exp(sk - mx)
            rr = jnp.dot(p.astype(bf16), vbext[nl * SW_BLOCK:nl * SW_BLOCK + kwin, cols],
                         preferred_element_type=f32)
            ob_sc[nl * SW_BLOCK:(nl + 1) * SW_BLOCK, cols] = _take_diag(rr, 1.0 / l, SW_BLOCK)

    ya = (oa_sc[...] * sza_ref[0].astype(f32)).astype(bf16)
    yb = (ob_sc[...] * szb_ref[0].astype(f32)).astype(bf16)
    pa = jnp.dot(ya, woa_ref[...], preferred_element_type=f32)
    pbm = jnp.dot(yb, wob_ref[...], preferred_element_type=f32)
    merged = (sga_ref[0].astype(f32) * pa + sgb_ref[0].astype(f32) * pbm).astype(bf16)
    y_ref[0] = x_ref[0] + jnp.dot(merged, wo_ref[...], preferred_element_type=f32)


def _attention(x, qa, ka, va, sza, qb, kb, vb, szb, sga, sgb, pb, swm, sink_col, hm, woa, wob, wo):
    B, S, _ = x.shape
    tq = ATT_TILE
    grid = (B, S // tq)
    n_na = S // NA_HALO
    n_sw = S // SW_HALO
    na_per = tq // NA_HALO
    sw_per = tq // SW_HALO

    tok = lambda w: pl.BlockSpec((1, tq, w), lambda b, i: (b, i, 0))

    def halo(width, blk, per, nblk):
        prev = pl.BlockSpec((1, blk, width), lambda b, i: (b, jnp.maximum(i * per - 1, 0), 0))
        nxt = pl.BlockSpec((1, blk, width), lambda b, i: (b, jnp.minimum((i + 1) * per, nblk - 1), 0))
        return prev, tok(width), nxt

    ka_specs = halo(NA_WIDTH, NA_HALO, na_per, n_na)
    kb_specs = halo(SW_WIDTH, SW_HALO, sw_per, n_sw)

    kern = functools.partial(_attn_kernel, rows_total=S // GRID_W, blocks_total=S // SW_BLOCK)
    return pl.pallas_call(
        kern,
        grid=grid,
        in_specs=[
            tok(NA_WIDTH), tok(NA_WIDTH), tok(SW_WIDTH), tok(SW_WIDTH), tok(D_MODEL), tok(D_MODEL),
            tok(D_MODEL),
            *ka_specs, *ka_specs, *kb_specs, *kb_specs,
            _const_spec(pb.shape), _const_spec(swm.shape), _const_spec(sink_col.shape),
            _const_spec(hm.shape),
            _const_spec(woa.shape), _const_spec(wob.shape), _const_spec(wo.shape),
        ],
        out_specs=tok(D_MODEL),
        out_shape=jax.ShapeDtypeStruct((B, S, D_MODEL), jnp.float32),
        scratch_shapes=[
            pltpu.VMEM((tq + 2 * NA_HALO, NA_WIDTH), jnp.bfloat16),
            pltpu.VMEM((tq + 2 * NA_HALO, NA_WIDTH), jnp.bfloat16),
            pltpu.VMEM((tq + 2 * SW_HALO, SW_WIDTH), jnp.bfloat16),
            pltpu.VMEM((tq + 2 * SW_HALO, SW_WIDTH), jnp.bfloat16),
            pltpu.VMEM((tq, NA_WIDTH), jnp.float32),
            pltpu.VMEM((tq, SW_WIDTH), jnp.float32),
        ],
        compiler_params=pltpu.CompilerParams(
            dimension_semantics=("arbitrary", "arbitrary"), vmem_limit_bytes=VMEM_LIMIT),
        name="attn",
    )(qa, sza, qb, szb, sga, sgb, x, ka, ka, ka, va, va, va, kb, kb, kb, vb, vb, vb,
      pb, swm, sink_col, hm, woa, wob, wo)


def _na_bias_table(rpb):
    c = jnp.arange(GRID_W)
    cs = jnp.clip(c - NA_KW // 2, 0, GRID_W - NA_KW)
    col_in = (c[None, :] >= cs[:, None]) & (c[None, :] < cs[:, None] + NA_KW)
    dc = jnp.clip(c[None, :] - c[:, None], -(NA_KW - 1), NA_KW - 1) + (NA_KW - 1)
    b = rpb[:, :, dc].astype(jnp.float32)
    b = jnp.where(col_in[None, None], b, NEG_BIG)
    nd = 2 * NA_KH - 2
    pair = jnp.concatenate([b[:, :nd], b[:, 1:nd + 1]], axis=-1)
    pair = pair.reshape(NA_HEADS // HEADS_PER_MXU, HEADS_PER_MXU, nd, GRID_W, LANES)
    return pair.transpose(2, 0, 1, 3, 4).reshape(nd, NA_HEADS // HEADS_PER_MXU,
                                                 HEADS_PER_MXU * GRID_W, LANES)


def _sw_mask_table():
    qi = jnp.arange(SW_BLOCK)[:, None]
    kj = jnp.arange(SW_BLOCK + 2 * SW_HALO)[None, :]
    band = jnp.abs(kj - SW_HALO - qi) <= WINDOW
    first = band & (kj >= SW_HALO)
    last = band & (kj < SW_HALO + SW_BLOCK)
    m = jnp.stack([first, band, last])
    return jnp.where(m, 0.0, NEG_BIG).astype(jnp.float32)


def _layer(x, p):
    S = x.shape[1]
    outs = _projection(x, p["g"], p["w_in"], p["bd"], p["qna"], p["kna"], p["qnb"], p["knb"],
                       p["cos"][:S], p["sin"][:S])
    qa, ka, va, sza, qb, kb, vb, szb, sga, sgb = outs
    return _attention(x, qa, ka, va, sza, qb, kb, vb, szb, sga, sgb, p["pb"], p["swm"], p["sink"],
                      p["hm"], p["woa"], p["wob"], p["wo"])


def _prepare(S, norm_g, w_in, qn_a, kn_a, rpb_a, qn_b, kn_b, sink_b, w_out_a, w_out_b, w_o):
    f32, bf16 = jnp.float32, jnp.bfloat16
    seg = jnp.arange(MXU_DIM) // HEAD_DIM
    bd = jnp.where(seg[:, None] == seg[None, :], 1.0 / HEAD_DIM, 0.0).astype(bf16)
    half = HEAD_DIM // 2
    inv = ROPE_THETA ** (-jnp.arange(half, dtype=f32) / half)
    ang = jnp.arange(S, dtype=f32)[:, None] * inv[None, :]
    cos, sin = jnp.cos(ang), jnp.sin(ang)
    cos = jnp.tile(jnp.concatenate([cos, cos], axis=1), (1, LANES // HEAD_DIM))
    sin = jnp.tile(jnp.concatenate([-sin, sin], axis=1), (1, LANES // HEAD_DIM))
    lane_head = jnp.arange(MXU_DIM) // HEAD_DIM
    hm = (lane_head[None, None, :] == jnp.arange(HEADS_PER_MXU)[:, None, None])
    hm = jnp.broadcast_to(hm, (HEADS_PER_MXU, SW_BLOCK, MXU_DIM)).astype(bf16)
    sink_col = jnp.repeat(sink_b.astype(f32).reshape(SW_KV_HEADS, SW_GROUP), SW_BLOCK, axis=1)
    return {
        "g": norm_g.astype(f32).reshape(1, D_MODEL),
        "w_in": w_in.astype(bf16),
        "bd": bd,
        "qna": jnp.tile(qn_a.astype(f32), NA_HEADS).reshape(1, NA_WIDTH),
        "kna": jnp.tile(kn_a.astype(f32), NA_HEADS).reshape(1, NA_WIDTH),
        "qnb": jnp.tile(qn_b.astype(f32), SW_HEADS).reshape(1, SW_WIDTH),
        "knb": jnp.tile(kn_b.astype(f32), SW_KV_HEADS).reshape(1, SW_KV_WIDTH),
        "cos": cos, "sin": sin,
        "pb": _na_bias_table(rpb_a),
        "swm": _sw_mask_table(),
        "sink": sink_col.reshape(SW_KV_HEADS, SW_GROUP * SW_BLOCK, 1),
        "hm": hm,
        "woa": w_out_a.astype(bf16), "wob": w_out_b.astype(bf16), "wo": w_o.astype(bf16),
    }


def kernel(x_prompt, x_sample, norm_g, w_in, qn_a, kn_a, rpb_a, qn_b, kn_b, sink_b, w_out_a, w_out_b, w_o):
    depth = norm_g.shape[0]
    y_prompt, y_sample = x_prompt, x_sample
    S = max(x_prompt.shape[1], x_sample.shape[1])
    for l in range(depth):
        p = _prepare(S, norm_g[l], w_in[l], qn_a[l], kn_a[l], rpb_a[l], qn_b[l], kn_b[l], sink_b[l],
                     w_out_a[l], w_out_b[l], w_o[l])
        y_prompt = _layer(y_prompt, p)
        y_sample = _layer(y_sample, p)
    return (y_prompt, y_sample)
```

```python
import functools

import jax
import jax.numpy as jnp
from jax import lax
from jax.experimental import pallas as pl
from jax.experimental.pallas import tpu as pltpu

D_MODEL = 1024
HEAD_DIM = 64
GRID_W = 64
NA_HEADS = 8
NA_KH = 8
NA_KW = 16
NA_WIDTH = NA_HEADS * HEAD_DIM
SW_HEADS = 8
SW_KV_HEADS = 2
SW_GROUP = SW_HEADS // SW_KV_HEADS
SW_WIDTH = SW_HEADS * HEAD_DIM
SW_KV_WIDTH = SW_KV_HEADS * HEAD_DIM
WINDOW = 128
ROPE_THETA = 10000.0
NORM_EPS = 1e-6

C_QA = 0
C_KA = C_QA + NA_WIDTH
C_VA = C_KA + NA_WIDTH
C_ZA = C_VA + NA_WIDTH
C_QB = C_ZA + NA_WIDTH
C_KB = C_QB + SW_WIDTH
C_VB = C_KB + SW_KV_WIDTH
C_ZB = C_VB + SW_KV_WIDTH
C_GA = C_ZB + SW_WIDTH
C_GB = C_GA + D_MODEL
IN_WIDTH = C_GB + D_MODEL

LANES = 128
MXU_DIM = 256
HEADS_PER_MXU = MXU_DIM // HEAD_DIM
NEG_BIG = -1e30
VMEM_LIMIT = 56 * 1024 * 1024

PROJ_TILE = 512
ATT_TILE = 512
ATT_ROWS = ATT_TILE // GRID_W
MERGE_ROWS = ATT_TILE
SM_ROWS = 256
NA_HALO = 256
SW_BLOCK = 64
SW_KWIN = 384
SW_PREV = WINDOW
SW_NEXT = 256
SW_EDGE = WINDOW // SW_BLOCK


def _segment_mean_sq(t, bd):
    w = t.shape[1]
    sq = (t * t).astype(jnp.bfloat16)
    outs = []
    for c in range(0, w, MXU_DIM):
        cw = min(MXU_DIM, w - c)
        outs.append(jnp.dot(sq[:, c:c + cw], bd[:cw, :cw], preferred_element_type=jnp.float32))
    return outs[0] if len(outs) == 1 else jnp.concatenate(outs, axis=1)


def _swap_halves(t):
    w = t.shape[1]
    lane = lax.broadcasted_iota(jnp.int32, (t.shape[0], LANES), 1)
    low = (lane % HEAD_DIM) < (HEAD_DIM // 2)
    outs = []
    for c in range(0, w, LANES):
        blk = t[:, c:c + LANES]
        fwd = pltpu.roll(blk, HEAD_DIM // 2, 1)
        bwd = pltpu.roll(blk, LANES - HEAD_DIM // 2, 1)
        outs.append(jnp.where(low, bwd, fwd))
    return outs[0] if len(outs) == 1 else jnp.concatenate(outs, axis=1)


def _tile_lanes(t, reps):
    return t if reps == 1 else jnp.concatenate([t] * reps, axis=1)


def _proj_kernel(x_ref, g_ref, w_ref, bd_ref, qna_ref, kna_ref, qnb_ref, knb_ref, cos_ref, sin_ref,
                 qa_o, ka_o, va_o, sza_o, qb_o, kb_o, vb_o, szb_o, sga_o, sgb_o):
    f32, bf16 = jnp.float32, jnp.bfloat16
    x = x_ref[0]
    ms = jnp.mean(x * x, axis=-1, keepdims=True)
    h = (x * lax.rsqrt(ms + NORM_EPS) * g_ref[...]).astype(bf16)
    bd = bd_ref[...]
    scale = HEAD_DIM ** -0.5

    def proj(c0, c1):
        return jnp.dot(h, w_ref[:, c0:c1], preferred_element_type=f32)

    def head_norm(t, gain):
        return t * lax.rsqrt(_segment_mean_sq(t, bd) + NORM_EPS) * gain

    def rotary(t):
        reps = t.shape[1] // LANES
        cos = _tile_lanes(cos_ref[...], reps)
        sin = _tile_lanes(sin_ref[...], reps)
        return t * cos + _swap_halves(t) * sin

    qa_o[0] = (head_norm(proj(C_QA, C_KA), qna_ref[...]) * scale).astype(bf16)
    ka_o[0] = head_norm(proj(C_KA, C_VA), kna_ref[...]).astype(bf16)
    va_o[0] = proj(C_VA, C_ZA).astype(bf16)
    sza_o[0] = jax.nn.silu(proj(C_ZA, C_QB)).astype(bf16)
    qb_o[0] = (rotary(head_norm(proj(C_QB, C_KB), qnb_ref[...])) * scale).astype(bf16)

    kvb = proj(C_KB, C_ZB)
    kb = rotary(head_norm(kvb[:, :SW_KV_WIDTH], knb_ref[...]))
    vb = kvb[:, SW_KV_WIDTH:]
    lane = lax.broadcasted_iota(jnp.int32, kb.shape, 1)
    first = lane < HEAD_DIM

    def replicate(t):
        r = pltpu.roll(t, HEAD_DIM, 1)
        h0 = jnp.where(first, t, r)
        h1 = jnp.where(first, r, t)
        return jnp.concatenate([h0, h0, h1, h1], axis=1)

    kb_o[0] = replicate(kb).astype(bf16)
    vb_o[0] = replicate(vb).astype(bf16)
    szb_o[0] = jax.nn.silu(proj(C_ZB, C_GA)).astype(bf16)
    sga_o[0] = jax.nn.sigmoid(proj(C_GA, C_GB)).astype(bf16)
    sgb_o[0] = jax.nn.sigmoid(proj(C_GB, IN_WIDTH)).astype(bf16)


def _const_spec(shape):
    nd = len(shape)
    return pl.BlockSpec(shape, lambda b, i: (0,) * nd)


def _projection(x, g, w_bf, bd, qna, kna, qnb, knb, cos, sin):
    B, S, _ = x.shape
    tm = PROJ_TILE
    grid = (B, S // tm)
    tok = lambda w: pl.BlockSpec((1, tm, w), lambda b, i: (b, i, 0))
    out_widths = [NA_WIDTH, NA_WIDTH, NA_WIDTH, NA_WIDTH, SW_WIDTH, SW_WIDTH, SW_WIDTH, SW_WIDTH,
                  D_MODEL, D_MODEL]
    return pl.pallas_call(
        _proj_kernel,
        grid=grid,
        in_specs=[
            tok(D_MODEL),
            _const_spec((1, D_MODEL)),
            _const_spec((D_MODEL, IN_WIDTH)),
            _const_spec((MXU_DIM, MXU_DIM)),
            _const_spec((1, NA_WIDTH)),
            _const_spec((1, NA_WIDTH)),
            _const_spec((1, SW_WIDTH)),
            _const_spec((1, SW_KV_WIDTH)),
            pl.BlockSpec((tm, LANES), lambda b, i: (i, 0)),
            pl.BlockSpec((tm, LANES), lambda b, i: (i, 0)),
        ],
        out_specs=[tok(w) for w in out_widths],
        out_shape=[jax.ShapeDtypeStruct((B, S, w), jnp.bfloat16) for w in out_widths],
        compiler_params=pltpu.CompilerParams(
            dimension_semantics=("arbitrary", "arbitrary"), vmem_limit_bytes=VMEM_LIMIT),
        name="proj",
    )(x, g, w_bf, bd, qna, kna, qnb, knb, cos, sin)


def _stack_heads(q, hm_ref):
    rows = q.shape[0]
    return jnp.concatenate([q * hm_ref[hh, :rows, :] for hh in range(HEADS_PER_MXU)], axis=0)


def _take_diag(r, inv_l, rows):
    lane = lax.broadcasted_iota(jnp.int32, (rows, MXU_DIM), 1) // HEAD_DIM
    rn = r * inv_l
    out = rn[(HEADS_PER_MXU - 1) * rows:]
    for hh in range(HEADS_PER_MXU - 2, -1, -1):
        out = jnp.where(lane == hh, rn[hh * rows:(hh + 1) * rows], out)
    return out


def _softmax_rows(s, add, floor=None):
    n = s.shape[1] // LANES
    sb = [s[:, m * LANES:(m + 1) * LANES] + add(m) for m in range(n)]
    mx = jnp.max(functools.reduce(jnp.maximum, sb), axis=-1, keepdims=True)
    if floor is not None:
        mx = jnp.maximum(mx, floor)
    p = [jnp.exp(t - mx) for t in sb]
    l = jnp.sum(functools.reduce(jnp.add, p), axis=-1, keepdims=True)
    if floor is not None:
        l = l + jnp.exp(floor - mx)
    return jnp.concatenate(p, axis=1).astype(jnp.bfloat16), 1.0 / l


def _attn_kernel(qa_ref, sza_ref, qb_ref, szb_ref, sga_ref, sgb_ref, x_ref,
                 kap_ref, kac_ref, kan_ref, vap_ref, vac_ref, van_ref,
                 kbp_ref, kbc_ref, kbn_ref, vbp_ref, vbc_ref, vbn_ref,
                 pb_ref, swm_ref, sink_ref, hm_ref, woa_ref, wob_ref, wo_ref,
                 y_ref, kaext, vaext, kbext, vbext, oa_sc, ob_sc, *, rows_total, blocks_total):
    f32, bf16 = jnp.float32, jnp.bfloat16
    i = pl.program_id(1)
    nt = (((1,), (1,)), ((), ()))

    kaext[0:NA_HALO] = kap_ref[0]
    kaext[NA_HALO:NA_HALO + ATT_TILE] = kac_ref[0]
    kaext[NA_HALO + ATT_TILE:] = kan_ref[0]
    vaext[0:NA_HALO] = vap_ref[0]
    vaext[NA_HALO:NA_HALO + ATT_TILE] = vac_ref[0]
    vaext[NA_HALO + ATT_TILE:] = van_ref[0]
    kbext[0:SW_PREV] = kbp_ref[0]
    kbext[SW_PREV:SW_PREV + ATT_TILE] = kbc_ref[0]
    kbext[SW_PREV + ATT_TILE:] = kbn_ref[0]
    vbext[0:SW_PREV] = vbp_ref[0]
    vbext[SW_PREV:SW_PREV + ATT_TILE] = vbc_ref[0]
    vbext[SW_PREV + ATT_TILE:] = vbn_ref[0]

    kh = min(NA_KH, rows_total)
    ext_row0 = i * ATT_ROWS - NA_HALO // GRID_W
    win = kh * GRID_W

    def na_row(rl):
        r = i * ATT_ROWS + rl
        krow0 = jnp.clip(r - kh // 2, 0, rows_total - kh)
        off = pl.multiple_of((krow0 - ext_row0) * GRID_W, GRID_W)
        d0 = krow0 - r + (NA_KH - 1)
        q0 = rl * GRID_W
        for g in range(NA_HEADS // HEADS_PER_MXU):
            cols = slice(g * MXU_DIM, (g + 1) * MXU_DIM)
            qs = _stack_heads(qa_ref[0, q0:q0 + GRID_W, cols], hm_ref)
            s = lax.dot_general(qs, kaext[pl.ds(off, win), cols], nt,
                                preferred_element_type=f32)
            ps, ils = [], []
            for c in range(s.shape[0] // SM_ROWS):
                rows = slice(c * SM_ROWS, (c + 1) * SM_ROWS)
                p, il = _softmax_rows(s[rows], lambda m: pb_ref[d0 + 2 * m, g, rows, :])
                ps.append(p)
                ils.append(il)
            rr = jnp.dot(jnp.concatenate(ps, axis=0), vaext[pl.ds(off, win), cols],
                         preferred_element_type=f32)
            oa_sc[q0:q0 + GRID_W, cols] = _take_diag(rr, jnp.concatenate(ils, axis=0), GRID_W)

    def sw_block(nl):
        gblk = i * (ATT_TILE // SW_BLOCK) + nl
        var = jnp.where(gblk < SW_EDGE, gblk,
                        jnp.where(gblk >= blocks_total - SW_EDGE,
                                  gblk - (blocks_total - 2 * SW_EDGE - 1), SW_EDGE))
        k0 = nl * SW_BLOCK
        for j in range(SW_KV_HEADS):
            cols = slice(j * MXU_DIM, (j + 1) * MXU_DIM)
            qs = _stack_heads(qb_ref[0, nl * SW_BLOCK:(nl + 1) * SW_BLOCK, cols], hm_ref)
            s = lax.dot_general(qs, kbext[k0:k0 + SW_KWIN, cols], nt,
                                preferred_element_type=f32)
            ps, ils = [], []
            for c in range(s.shape[0] // SM_ROWS):
                rows = slice(c * SM_ROWS, (c + 1) * SM_ROWS)
                p, il = _softmax_rows(
                    s[rows],
                    lambda m: jnp.concatenate(
                        [swm_ref[var, :, m * LANES:(m + 1) * LANES]] * (SM_ROWS // SW_BLOCK), axis=0),
                    floor=sink_ref[j, rows, :])
                ps.append(p)
                ils.append(il)
            rr = jnp.dot(jnp.concatenate(ps, axis=0), vbext[k0:k0 + SW_KWIN, cols],
                         preferred_element_type=f32)
            ob_sc[nl * SW_BLOCK:(nl + 1) * SW_BLOCK, cols] = _take_diag(
                rr, jnp.concatenate(ils, axis=0), SW_BLOCK)

    def merge_out(t0, t1):
        ya = (oa_sc[t0:t1] * sza_ref[0, t0:t1].astype(f32)).astype(bf16)
        yb = (ob_sc[t0:t1] * szb_ref[0, t0:t1].astype(f32)).astype(bf16)
        pa = jnp.dot(ya, woa_ref[...], preferred_element_type=f32)
        pbm = jnp.dot(yb, wob_ref[...], preferred_element_type=f32)
        merged = (sga_ref[0, t0:t1].astype(f32) * pa + sgb_ref[0, t0:t1].astype(f32) * pbm).astype(bf16)
        y_ref[0, t0:t1] = x_ref[0, t0:t1] + jnp.dot(merged, wo_ref[...], preferred_element_type=f32)

    for rl in range(ATT_TILE // GRID_W):
        na_row(rl)
    for nl in range(ATT_TILE // SW_BLOCK):
        sw_block(nl)
    for t0 in range(0, ATT_TILE, MERGE_ROWS):
        merge_out(t0, t0 + MERGE_ROWS)


def _attention(x, qa, ka, va, sza, qb, kb, vb, szb, sga, sgb, pb, swm, sink_col, hm, woa, wob, wo):
    B, S, _ = x.shape
    tq = ATT_TILE
    grid = (B, S // tq)

    tok = lambda w: pl.BlockSpec((1, tq, w), lambda b, i: (b, i, 0))

    def halo(width, prev_blk, next_blk):
        prev = pl.BlockSpec((1, prev_blk, width),
                            lambda b, i: (b, jnp.maximum(i * (tq // prev_blk) - 1, 0), 0))
        nxt = pl.BlockSpec((1, next_blk, width),
                           lambda b, i: (b, jnp.minimum((i + 1) * (tq // next_blk), S // next_blk - 1), 0))
        return prev, tok(width), nxt

    ka_specs = halo(NA_WIDTH, NA_HALO, NA_HALO)
    kb_specs = halo(SW_WIDTH, SW_PREV, SW_NEXT)

    kern = functools.partial(_attn_kernel, rows_total=S // GRID_W, blocks_total=S // SW_BLOCK)
    return pl.pallas_call(
        kern,
        grid=grid,
        in_specs=[
            tok(NA_WIDTH), tok(NA_WIDTH), tok(SW_WIDTH), tok(SW_WIDTH), tok(D_MODEL), tok(D_MODEL),
            tok(D_MODEL),
            *ka_specs, *ka_specs, *kb_specs, *kb_specs,
            _const_spec(pb.shape), _const_spec(swm.shape), _const_spec(sink_col.shape),
            _const_spec(hm.shape),
            _const_spec(woa.shape), _const_spec(wob.shape), _const_spec(wo.shape),
        ],
        out_specs=tok(D_MODEL),
        out_shape=jax.ShapeDtypeStruct((B, S, D_MODEL), jnp.float32),
        scratch_shapes=[
            pltpu.VMEM((tq + 2 * NA_HALO, NA_WIDTH), jnp.bfloat16),
            pltpu.VMEM((tq + 2 * NA_HALO, NA_WIDTH), jnp.bfloat16),
            pltpu.VMEM((SW_PREV + tq + SW_NEXT, SW_WIDTH), jnp.bfloat16),
            pltpu.VMEM((SW_PREV + tq + SW_NEXT, SW_WIDTH), jnp.bfloat16),
            pltpu.VMEM((tq, NA_WIDTH), jnp.float32),
            pltpu.VMEM((tq, SW_WIDTH), jnp.float32),
        ],
        compiler_params=pltpu.CompilerParams(
            dimension_semantics=("arbitrary", "arbitrary"), vmem_limit_bytes=VMEM_LIMIT),
        name="attn",
    )(qa, sza, qb, szb, sga, sgb, x, ka, ka, ka, va, va, va, kb, kb, kb, vb, vb, vb,
      pb, swm, sink_col, hm, woa, wob, wo)


def _na_bias_table(rpb):
    c = jnp.arange(GRID_W)
    cs = jnp.clip(c - NA_KW // 2, 0, GRID_W - NA_KW)
    col_in = (c[None, :] >= cs[:, None]) & (c[None, :] < cs[:, None] + NA_KW)
    dc = jnp.clip(c[None, :] - c[:, None], -(NA_KW - 1), NA_KW - 1) + (NA_KW - 1)
    b = rpb[:, :, dc].astype(jnp.float32)
    b = jnp.where(col_in[None, None], b, NEG_BIG)
    nd = 2 * NA_KH - 2
    pair = jnp.concatenate([b[:, :nd], b[:, 1:nd + 1]], axis=-1)
    pair = pair.reshape(NA_HEADS // HEADS_PER_MXU, HEADS_PER_MXU, nd, GRID_W, LANES)
    return pair.transpose(2, 0, 1, 3, 4).reshape(nd, NA_HEADS // HEADS_PER_MXU,
                                                 HEADS_PER_MXU * GRID_W, LANES)


def _sw_mask_table():
    qi = jnp.arange(SW_BLOCK)[:, None]
    kj = jnp.arange(SW_KWIN)[None, :]
    band = jnp.abs(kj - WINDOW - qi) <= WINDOW
    first = [band & (kj >= WINDOW - e * SW_BLOCK) for e in range(SW_EDGE)]
    last = [band & (kj < (SW_EDGE - e) * SW_BLOCK + WINDOW) for e in range(SW_EDGE)]
    m = jnp.stack(first + [band] + last)
    return jnp.where(m, 0.0, NEG_BIG).astype(jnp.float32)


def _layer(x, p):
    S = x.shape[1]
    outs = _projection(x, p["g"], p["w_in"], p["bd"], p["qna"], p["kna"], p["qnb"], p["knb"],
                       p["cos"][:S], p["sin"][:S])
    qa, ka, va, sza, qb, kb, vb, szb, sga, sgb = outs
    return _attention(x, qa, ka, va, sza, qb, kb, vb, szb, sga, sgb, p["pb"], p["swm"], p["sink"],
                      p["hm"], p["woa"], p["wob"], p["wo"])


def _prepare(S, norm_g, w_in, qn_a, kn_a, rpb_a, qn_b, kn_b, sink_b, w_out_a, w_out_b, w_o):
    f32, bf16 = jnp.float32, jnp.bfloat16
    seg = jnp.arange(MXU_DIM) // HEAD_DIM
    bd = jnp.where(seg[:, None] == seg[None, :], 1.0 / HEAD_DIM, 0.0).astype(bf16)
    half = HEAD_DIM // 2
    inv = ROPE_THETA ** (-jnp.arange(half, dtype=f32) / half)
    ang = jnp.arange(S, dtype=f32)[:, None] * inv[None, :]
    cos, sin = jnp.cos(ang), jnp.sin(ang)
    cos = jnp.tile(jnp.concatenate([cos, cos], axis=1), (1, LANES // HEAD_DIM))
    sin = jnp.tile(jnp.concatenate([-sin, sin], axis=1), (1, LANES // HEAD_DIM))
    lane_head = jnp.arange(MXU_DIM) // HEAD_DIM
    hm = (lane_head[None, None, :] == jnp.arange(HEADS_PER_MXU)[:, None, None])
    hm = jnp.broadcast_to(hm, (HEADS_PER_MXU, max(SW_BLOCK, GRID_W), MXU_DIM)).astype(bf16)
    sink_col = jnp.repeat(sink_b.astype(f32).reshape(SW_KV_HEADS, SW_GROUP), SW_BLOCK, axis=1)
    return {
        "g": norm_g.astype(f32).reshape(1, D_MODEL),
        "w_in": w_in.astype(bf16),
        "bd": bd,
        "qna": jnp.tile(qn_a.astype(f32), NA_HEADS).reshape(1, NA_WIDTH),
        "kna": jnp.tile(kn_a.astype(f32), NA_HEADS).reshape(1, NA_WIDTH),
        "qnb": jnp.tile(qn_b.astype(f32), SW_HEADS).reshape(1, SW_WIDTH),
        "knb": jnp.tile(kn_b.astype(f32), SW_KV_HEADS).reshape(1, SW_KV_WIDTH),
        "cos": cos, "sin": sin,
        "pb": _na_bias_table(rpb_a),
        "swm": _sw_mask_table(),
        "sink": sink_col.reshape(SW_KV_HEADS, SW_GROUP * SW_BLOCK, 1),
        "hm": hm,
        "woa": w_out_a.astype(bf16), "wob": w_out_b.astype(bf16), "wo": w_o.astype(bf16),
    }


def kernel(x_prompt, x_sample, norm_g, w_in, qn_a, kn_a, rpb_a, qn_b, kn_b, sink_b, w_out_a, w_out_b, w_o):
    depth = norm_g.shape[0]
    y_prompt, y_sample = x_prompt, x_sample
    S = max(x_prompt.shape[1], x_sample.shape[1])
    for l in range(depth):
        p = _prepare(S, norm_g[l], w_in[l], qn_a[l], kn_a[l], rpb_a[l], qn_b[l], kn_b[l], sink_b[l],
                     w_out_a[l], w_out_b[l], w_o[l])
        y_prompt = _layer(y_prompt, p)
        y_sample = _layer(y_sample, p)
    return (y_prompt, y_sample)
```

```python
import functools

import jax
import jax.numpy as jnp
from jax import lax
from jax.experimental import pallas as pl
from jax.experimental.pallas import tpu as pltpu

D_MODEL = 1024
HEAD_DIM = 64
GRID_W = 64
NA_HEADS = 8
NA_KH = 8
NA_KW = 16
NA_WIDTH = NA_HEADS * HEAD_DIM
SW_HEADS = 8
SW_KV_HEADS = 2
SW_GROUP = SW_HEADS // SW_KV_HEADS
SW_WIDTH = SW_HEADS * HEAD_DIM
SW_KV_WIDTH = SW_KV_HEADS * HEAD_DIM
WINDOW = 128
ROPE_THETA = 10000.0
NORM_EPS = 1e-6

C_QA = 0
C_KA = C_QA + NA_WIDTH
C_VA = C_KA + NA_WIDTH
C_ZA = C_VA + NA_WIDTH
C_QB = C_ZA + NA_WIDTH
C_KB = C_QB + SW_WIDTH
C_VB = C_KB + SW_KV_WIDTH
C_ZB = C_VB + SW_KV_WIDTH
C_GA = C_ZB + SW_WIDTH
C_GB = C_GA + D_MODEL
IN_WIDTH = C_GB + D_MODEL

LANES = 128
MXU_DIM = 256
HEADS_PER_MXU = MXU_DIM // HEAD_DIM
NEG_BIG = -1e30
LOG2_E = 1.4426950408889634
VMEM_LIMIT = 56 * 1024 * 1024

PROJ_TILE = 512
ATT_TILE = 512
ATT_ROWS = ATT_TILE // GRID_W
MERGE_ROWS = ATT_TILE
SM_ROWS = 256
NA_HALO = 256
SW_BLOCK = 64
SW_KWIN = 384
SW_PREV = WINDOW
SW_NEXT = 256
SW_EDGE = WINDOW // SW_BLOCK


def _segment_mean_sq(t, bd):
    w = t.shape[1]
    sq = (t * t).astype(jnp.bfloat16)
    outs = []
    for c in range(0, w, MXU_DIM):
        cw = min(MXU_DIM, w - c)
        outs.append(jnp.dot(sq[:, c:c + cw], bd[:cw, :cw], preferred_element_type=jnp.float32))
    return outs[0] if len(outs) == 1 else jnp.concatenate(outs, axis=1)


def _swap_halves(t):
    w = t.shape[1]
    lane = lax.broadcasted_iota(jnp.int32, (t.shape[0], LANES), 1)
    low = (lane % HEAD_DIM) < (HEAD_DIM // 2)
    outs = []
    for c in range(0, w, LANES):
        blk = t[:, c:c + LANES]
        fwd = pltpu.roll(blk, HEAD_DIM // 2, 1)
        bwd = pltpu.roll(blk, LANES - HEAD_DIM // 2, 1)
        outs.append(jnp.where(low, bwd, fwd))
    return outs[0] if len(outs) == 1 else jnp.concatenate(outs, axis=1)


def _tile_lanes(t, reps):
    return t if reps == 1 else jnp.concatenate([t] * reps, axis=1)


def _proj_kernel(x_ref, g_ref, w_ref, bd_ref, qna_ref, kna_ref, qnb_ref, knb_ref, cos_ref, sin_ref,
                 qa_o, ka_o, va_o, sza_o, qb_o, kb_o, vb_o, szb_o, sga_o, sgb_o):
    f32, bf16 = jnp.float32, jnp.bfloat16
    x = x_ref[0]
    ms = jnp.mean(x * x, axis=-1, keepdims=True)
    h = (x * lax.rsqrt(ms + NORM_EPS) * g_ref[...]).astype(bf16)
    bd = bd_ref[...]
    scale = HEAD_DIM ** -0.5 * LOG2_E

    def proj(c0, c1):
        return jnp.dot(h, w_ref[:, c0:c1], preferred_element_type=f32)

    def head_norm(t, gain):
        return t * lax.rsqrt(_segment_mean_sq(t, bd) + NORM_EPS) * gain

    def rotary(t):
        reps = t.shape[1] // LANES
        cos = _tile_lanes(cos_ref[...], reps)
        sin = _tile_lanes(sin_ref[...], reps)
        return t * cos + _swap_halves(t) * sin

    qa_o[0] = (head_norm(proj(C_QA, C_KA), qna_ref[...]) * scale).astype(bf16)
    ka_o[0] = head_norm(proj(C_KA, C_VA), kna_ref[...]).astype(bf16)
    va_o[0] = proj(C_VA, C_ZA).astype(bf16)
    sza_o[0] = jax.nn.silu(proj(C_ZA, C_QB)).astype(bf16)
    qb_o[0] = (rotary(head_norm(proj(C_QB, C_KB), qnb_ref[...])) * scale).astype(bf16)

    kvb = proj(C_KB, C_ZB)
    kb = rotary(head_norm(kvb[:, :SW_KV_WIDTH], knb_ref[...]))
    vb = kvb[:, SW_KV_WIDTH:]
    lane = lax.broadcasted_iota(jnp.int32, kb.shape, 1)
    first = lane < HEAD_DIM

    def replicate(t):
        r = pltpu.roll(t, HEAD_DIM, 1)
        h0 = jnp.where(first, t, r)
        h1 = jnp.where(first, r, t)
        return jnp.concatenate([h0, h0, h1, h1], axis=1)

    kb_o[0] = replicate(kb).astype(bf16)
    vb_o[0] = replicate(vb).astype(bf16)
    szb_o[0] = jax.nn.silu(proj(C_ZB, C_GA)).astype(bf16)
    sga_o[0] = jax.nn.sigmoid(proj(C_GA, C_GB)).astype(bf16)
    sgb_o[0] = jax.nn.sigmoid(proj(C_GB, IN_WIDTH)).astype(bf16)


def _const_spec(shape):
    nd = len(shape)
    return pl.BlockSpec(shape, lambda b, i: (0,) * nd)


def _projection(x, g, w_bf, bd, qna, kna, qnb, knb, cos, sin):
    B, S, _ = x.shape
    tm = PROJ_TILE
    grid = (B, S // tm)
    tok = lambda w: pl.BlockSpec((1, tm, w), lambda b, i: (b, i, 0))
    out_widths = [NA_WIDTH, NA_WIDTH, NA_WIDTH, NA_WIDTH, SW_WIDTH, SW_WIDTH, SW_WIDTH, SW_WIDTH,
                  D_MODEL, D_MODEL]
    return pl.pallas_call(
        _proj_kernel,
        grid=grid,
        in_specs=[
            tok(D_MODEL),
            _const_spec((1, D_MODEL)),
            _const_spec((D_MODEL, IN_WIDTH)),
            _const_spec((MXU_DIM, MXU_DIM)),
            _const_spec((1, NA_WIDTH)),
            _const_spec((1, NA_WIDTH)),
            _const_spec((1, SW_WIDTH)),
            _const_spec((1, SW_KV_WIDTH)),
            pl.BlockSpec((tm, LANES), lambda b, i: (i, 0)),
            pl.BlockSpec((tm, LANES), lambda b, i: (i, 0)),
        ],
        out_specs=[tok(w) for w in out_widths],
        out_shape=[jax.ShapeDtypeStruct((B, S, w), jnp.bfloat16) for w in out_widths],
        compiler_params=pltpu.CompilerParams(
            dimension_semantics=("arbitrary", "arbitrary"), vmem_limit_bytes=VMEM_LIMIT),
        name="proj",
    )(x, g, w_bf, bd, qna, kna, qnb, knb, cos, sin)


def _stack_heads(q, hm_ref):
    rows = q.shape[0]
    return jnp.concatenate([q * hm_ref[hh, :rows, :] for hh in range(HEADS_PER_MXU)], axis=0)


def _take_diag(r, inv_l, rows):
    lane = lax.broadcasted_iota(jnp.int32, (rows, MXU_DIM), 1) // HEAD_DIM
    rn = r * inv_l
    out = rn[(HEADS_PER_MXU - 1) * rows:]
    for hh in range(HEADS_PER_MXU - 2, -1, -1):
        out = jnp.where(lane == hh, rn[hh * rows:(hh + 1) * rows], out)
    return out


def _softmax_rows(s, add, floor=None):
    n = s.shape[1] // LANES
    sb = [s[:, m * LANES:(m + 1) * LANES] + add(m) for m in range(n)]
    mx = jnp.max(functools.reduce(jnp.maximum, sb), axis=-1, keepdims=True)
    if floor is not None:
        mx = jnp.maximum(mx, floor)
    p = [jnp.exp2(t - mx) for t in sb]
    l = jnp.sum(functools.reduce(jnp.add, p), axis=-1, keepdims=True)
    if floor is not None:
        l = l + jnp.exp2(floor - mx)
    return jnp.concatenate(p, axis=1).astype(jnp.bfloat16), 1.0 / l


def _attn_kernel(qa_ref, sza_ref, qb_ref, szb_ref, sga_ref, sgb_ref, x_ref,
                 kap_ref, kac_ref, kan_ref, vap_ref, vac_ref, van_ref,
                 kbp_ref, kbc_ref, kbn_ref, vbp_ref, vbc_ref, vbn_ref,
                 pb_ref, swm_ref, sink_ref, hm_ref, woa_ref, wob_ref, wo_ref,
                 y_ref, kaext, vaext, kbext, vbext, oa_sc, ob_sc, *, rows_total, blocks_total):
    f32, bf16 = jnp.float32, jnp.bfloat16
    i = pl.program_id(1)
    nt = (((1,), (1,)), ((), ()))

    kaext[0:NA_HALO] = kap_ref[0]
    kaext[NA_HALO:NA_HALO + ATT_TILE] = kac_ref[0]
    kaext[NA_HALO + ATT_TILE:] = kan_ref[0]
    vaext[0:NA_HALO] = vap_ref[0]
    vaext[NA_HALO:NA_HALO + ATT_TILE] = vac_ref[0]
    vaext[NA_HALO + ATT_TILE:] = van_ref[0]
    kbext[0:SW_PREV] = kbp_ref[0]
    kbext[SW_PREV:SW_PREV + ATT_TILE] = kbc_ref[0]
    kbext[SW_PREV + ATT_TILE:] = kbn_ref[0]
    vbext[0:SW_PREV] = vbp_ref[0]
    vbext[SW_PREV:SW_PREV + ATT_TILE] = vbc_ref[0]
    vbext[SW_PREV + ATT_TILE:] = vbn_ref[0]

    kh = min(NA_KH, rows_total)
    ext_row0 = i * ATT_ROWS - NA_HALO // GRID_W
    win = kh * GRID_W

    def na_row(rl):
        r = i * ATT_ROWS + rl
        krow0 = jnp.clip(r - kh // 2, 0, rows_total - kh)
        off = pl.multiple_of((krow0 - ext_row0) * GRID_W, GRID_W)
        d0 = krow0 - r + (NA_KH - 1)
        q0 = rl * GRID_W
        for g in range(NA_HEADS // HEADS_PER_MXU):
            cols = slice(g * MXU_DIM, (g + 1) * MXU_DIM)
            qs = _stack_heads(qa_ref[0, q0:q0 + GRID_W, cols], hm_ref)
            s = lax.dot_general(qs, kaext[pl.ds(off, win), cols], nt,
                                preferred_element_type=f32)
            ps, ils = [], []
            for c in range(s.shape[0] // SM_ROWS):
                rows = slice(c * SM_ROWS, (c + 1) * SM_ROWS)
                p, il = _softmax_rows(s[rows], lambda m: pb_ref[d0 + 2 * m, g, rows, :])
                ps.append(p)
                ils.append(il)
            rr = jnp.dot(jnp.concatenate(ps, axis=0), vaext[pl.ds(off, win), cols],
                         preferred_element_type=f32)
            oa_sc[q0:q0 + GRID_W, cols] = _take_diag(rr, jnp.concatenate(ils, axis=0), GRID_W)

    def sw_block(nl):
        gblk = i * (ATT_TILE // SW_BLOCK) + nl
        var = jnp.where(gblk < SW_EDGE, gblk,
                        jnp.where(gblk >= blocks_total - SW_EDGE,
                                  gblk - (blocks_total - 2 * SW_EDGE - 1), SW_EDGE))
        k0 = nl * SW_BLOCK
        for j in range(SW_KV_HEADS):
            cols = slice(j * MXU_DIM, (j + 1) * MXU_DIM)
            qs = _stack_heads(qb_ref[0, nl * SW_BLOCK:(nl + 1) * SW_BLOCK, cols], hm_ref)
            s = lax.dot_general(qs, kbext[k0:k0 + SW_KWIN, cols], nt,
                                preferred_element_type=f32)
            ps, ils = [], []
            for c in range(s.shape[0] // SM_ROWS):
                rows = slice(c * SM_ROWS, (c + 1) * SM_ROWS)
                p, il = _softmax_rows(
                    s[rows],
                    lambda m: jnp.concatenate(
                        [swm_ref[var, :, m * LANES:(m + 1) * LANES]] * (SM_ROWS // SW_BLOCK), axis=0),
                    floor=sink_ref[j, rows, :])
                ps.append(p)
                ils.append(il)
            rr = jnp.dot(jnp.concatenate(ps, axis=0), vbext[k0:k0 + SW_KWIN, cols],
                         preferred_element_type=f32)
            ob_sc[nl * SW_BLOCK:(nl + 1) * SW_BLOCK, cols] = _take_diag(
                rr, jnp.concatenate(ils, axis=0), SW_BLOCK)

    def merge_out(t0, t1):
        ya = (oa_sc[t0:t1] * sza_ref[0, t0:t1].astype(f32)).astype(bf16)
        yb = (ob_sc[t0:t1] * szb_ref[0, t0:t1].astype(f32)).astype(bf16)
        pa = jnp.dot(ya, woa_ref[...], preferred_element_type=f32)
        pbm = jnp.dot(yb, wob_ref[...], preferred_element_type=f32)
        merged = (sga_ref[0, t0:t1].astype(f32) * pa + sgb_ref[0, t0:t1].astype(f32) * pbm).astype(bf16)
        y_ref[0, t0:t1] = x_ref[0, t0:t1] + jnp.dot(merged, wo_ref[...], preferred_element_type=f32)

    for rl in range(ATT_TILE // GRID_W):
        na_row(rl)
    for nl in range(ATT_TILE // SW_BLOCK):
        sw_block(nl)
    for t0 in range(0, ATT_TILE, MERGE_ROWS):
        merge_out(t0, t0 + MERGE_ROWS)


def _attention(x, qa, ka, va, sza, qb, kb, vb, szb, sga, sgb, pb, swm, sink_col, hm, woa, wob, wo):
    B, S, _ = x.shape
    tq = ATT_TILE
    grid = (B, S // tq)

    tok = lambda w: pl.BlockSpec((1, tq, w), lambda b, i: (b, i, 0))

    def halo(width, prev_blk, next_blk):
        prev = pl.BlockSpec((1, prev_blk, width),
                            lambda b, i: (b, jnp.maximum(i * (tq // prev_blk) - 1, 0), 0))
        nxt = pl.BlockSpec((1, next_blk, width),
                           lambda b, i: (b, jnp.minimum((i + 1) * (tq // next_blk), S // next_blk - 1), 0))
        return prev, tok(width), nxt

    ka_specs = halo(NA_WIDTH, NA_HALO, NA_HALO)
    kb_specs = halo(SW_WIDTH, SW_PREV, SW_NEXT)

    kern = functools.partial(_attn_kernel, rows_total=S // GRID_W, blocks_total=S // SW_BLOCK)
    return pl.pallas_call(
        kern,
        grid=grid,
        in_specs=[
            tok(NA_WIDTH), tok(NA_WIDTH), tok(SW_WIDTH), tok(SW_WIDTH), tok(D_MODEL), tok(D_MODEL),
            tok(D_MODEL),
            *ka_specs, *ka_specs, *kb_specs, *kb_specs,
            _const_spec(pb.shape), _const_spec(swm.shape), _const_spec(sink_col.shape),
            _const_spec(hm.shape),
            _const_spec(woa.shape), _const_spec(wob.shape), _const_spec(wo.shape),
        ],
        out_specs=tok(D_MODEL),
        out_shape=jax.ShapeDtypeStruct((B, S, D_MODEL), jnp.float32),
        scratch_shapes=[
            pltpu.VMEM((tq + 2 * NA_HALO, NA_WIDTH), jnp.bfloat16),
            pltpu.VMEM((tq + 2 * NA_HALO, NA_WIDTH), jnp.bfloat16),
            pltpu.VMEM((SW_PREV + tq + SW_NEXT, SW_WIDTH), jnp.bfloat16),
            pltpu.VMEM((SW_PREV + tq + SW_NEXT, SW_WIDTH), jnp.bfloat16),
            pltpu.VMEM((tq, NA_WIDTH), jnp.float32),
            pltpu.VMEM((tq, SW_WIDTH), jnp.float32),
        ],
        compiler_params=pltpu.CompilerParams(
            dimension_semantics=("arbitrary", "arbitrary"), vmem_limit_bytes=VMEM_LIMIT),
        name="attn",
    )(qa, sza, qb, szb, sga, sgb, x, ka, ka, ka, va, va, va, kb, kb, kb, vb, vb, vb,
      pb, swm, sink_col, hm, woa, wob, wo)


def _na_bias_table(rpb):
    c = jnp.arange(GRID_W)
    cs = jnp.clip(c - NA_KW // 2, 0, GRID_W - NA_KW)
    col_in = (c[None, :] >= cs[:, None]) & (c[None, :] < cs[:, None] + NA_KW)
    pad = GRID_W - NA_KW
    rp = jnp.pad(rpb.astype(jnp.float32) * LOG2_E, ((0, 0), (0, 0), (pad, pad)))
    b = jnp.stack([rp[:, :, GRID_W - 1 - q:2 * GRID_W - 1 - q] for q in range(GRID_W)], axis=2)
    b = jnp.where(col_in[None, None], b, NEG_BIG)
    nd = 2 * NA_KH - 2
    pair = jnp.concatenate([b[:, :nd], b[:, 1:nd + 1]], axis=-1)
    pair = pair.reshape(NA_HEADS // HEADS_PER_MXU, HEADS_PER_MXU, nd, GRID_W, LANES)
    return pair.transpose(2, 0, 1, 3, 4).reshape(nd, NA_HEADS // HEADS_PER_MXU,
                                                 HEADS_PER_MXU * GRID_W, LANES)


def _sw_mask_table():
    qi = jnp.arange(SW_BLOCK)[:, None]
    kj = jnp.arange(SW_KWIN)[None, :]
    band = jnp.abs(kj - WINDOW - qi) <= WINDOW
    first = [band & (kj >= WINDOW - e * SW_BLOCK) for e in range(SW_EDGE)]
    last = [band & (kj < (SW_EDGE - e) * SW_BLOCK + WINDOW) for e in range(SW_EDGE)]
    m = jnp.stack(first + [band] + last)
    return jnp.where(m, 0.0, NEG_BIG).astype(jnp.float32)


def _layer(x, p):
    S = x.shape[1]
    outs = _projection(x, p["g"], p["w_in"], p["bd"], p["qna"], p["kna"], p["qnb"], p["knb"],
                       p["cos"][:S], p["sin"][:S])
    qa, ka, va, sza, qb, kb, vb, szb, sga, sgb = outs
    return _attention(x, qa, ka, va, sza, qb, kb, vb, szb, sga, sgb, p["pb"], p["swm"], p["sink"],
                      p["hm"], p["woa"], p["wob"], p["wo"])


def _prepare(S, norm_g, w_in, qn_a, kn_a, rpb_a, qn_b, kn_b, sink_b, w_out_a, w_out_b, w_o):
    f32, bf16 = jnp.float32, jnp.bfloat16
    seg = jnp.arange(MXU_DIM) // HEAD_DIM
    bd = jnp.where(seg[:, None] == seg[None, :], 1.0 / HEAD_DIM, 0.0).astype(bf16)
    half = HEAD_DIM // 2
    inv = ROPE_THETA ** (-jnp.arange(half, dtype=f32) / half)
    ang = jnp.arange(S, dtype=f32)[:, None] * inv[None, :]
    cos, sin = jnp.cos(ang), jnp.sin(ang)
    cos = jnp.tile(jnp.concatenate([cos, cos], axis=1), (1, LANES // HEAD_DIM))
    sin = jnp.tile(jnp.concatenate([-sin, sin], axis=1), (1, LANES // HEAD_DIM))
    lane_head = jnp.arange(MXU_DIM) // HEAD_DIM
    hm = (lane_head[None, None, :] == jnp.arange(HEADS_PER_MXU)[:, None, None])
    hm = jnp.broadcast_to(hm, (HEADS_PER_MXU, max(SW_BLOCK, GRID_W), MXU_DIM)).astype(bf16)
    sink_col = jnp.repeat((sink_b.astype(f32) * LOG2_E).reshape(SW_KV_HEADS, SW_GROUP), SW_BLOCK, axis=1)
    return {
        "g": norm_g.astype(f32).reshape(1, D_MODEL),
        "w_in": w_in.astype(bf16),
        "bd": bd,
        "qna": jnp.tile(qn_a.astype(f32), NA_HEADS).reshape(1, NA_WIDTH),
        "kna": jnp.tile(kn_a.astype(f32), NA_HEADS).reshape(1, NA_WIDTH),
        "qnb": jnp.tile(qn_b.astype(f32), SW_HEADS).reshape(1, SW_WIDTH),
        "knb": jnp.tile(kn_b.astype(f32), SW_KV_HEADS).reshape(1, SW_KV_WIDTH),
        "cos": cos, "sin": sin,
        "pb": _na_bias_table(rpb_a),
        "swm": _sw_mask_table(),
        "sink": sink_col.reshape(SW_KV_HEADS, SW_GROUP * SW_BLOCK, 1),
        "hm": hm,
        "woa": w_out_a.astype(bf16), "wob": w_out_b.astype(bf16), "wo": w_o.astype(bf16),
    }


def kernel(x_prompt, x_sample, norm_g, w_in, qn_a, kn_a, rpb_a, qn_b, kn_b, sink_b, w_out_a, w_out_b, w_o):
    depth = norm_g.shape[0]
    y_prompt, y_sample = x_prompt, x_sample
    S = max(x_prompt.shape[1], x_sample.shape[1])
    for l in range(depth):
        p = _prepare(S, norm_g[l], w_in[l], qn_a[l], kn_a[l], rpb_a[l], qn_b[l], kn_b[l], sink_b[l],
                     w_out_a[l], w_out_b[l], w_o[l])
        y_prompt = _layer(y_prompt, p)
        y_sample = _layer(y_sample, p)
    return (y_prompt, y_sample)
```

```python
import functools

import jax
import jax.numpy as jnp
from jax import lax
from jax.experimental import pallas as pl
from jax.experimental.pallas import tpu as pltpu

D_MODEL = 1024
HEAD_DIM = 64
GRID_W = 64
NA_HEADS = 8
NA_KH = 8
NA_KW = 16
NA_WIDTH = NA_HEADS * HEAD_DIM
SW_HEADS = 8
SW_KV_HEADS = 2
SW_GROUP = SW_HEADS // SW_KV_HEADS
SW_WIDTH = SW_HEADS * HEAD_DIM
SW_KV_WIDTH = SW_KV_HEADS * HEAD_DIM
WINDOW = 128
ROPE_THETA = 10000.0
NORM_EPS = 1e-6

C_QA = 0
C_KA = C_QA + NA_WIDTH
C_VA = C_KA + NA_WIDTH
C_ZA = C_VA + NA_WIDTH
C_QB = C_ZA + NA_WIDTH
C_KB = C_QB + SW_WIDTH
C_VB = C_KB + SW_KV_WIDTH
C_ZB = C_VB + SW_KV_WIDTH
C_GA = C_ZB + SW_WIDTH
C_GB = C_GA + D_MODEL
IN_WIDTH = C_GB + D_MODEL

LANES = 128
MXU_DIM = 256
HEADS_PER_MXU = MXU_DIM // HEAD_DIM
NEG_BIG = -1e30
LOG2_E = 1.4426950408889634
MAX_UNSHIFTED_LOGIT = 64.0
VMEM_LIMIT = 56 * 1024 * 1024

PROJ_TILE = 512
ATT_TILE = 1024
ATT_ROWS = ATT_TILE // GRID_W
MERGE_TILE = 1024
NA_HALO = 256
SW_BLOCK = 64
SW_KWIN = 384
SW_PREV = WINDOW
SW_NEXT = 256
SW_EDGE = WINDOW // SW_BLOCK


def _segment_mean_sq(t, bd):
    w = t.shape[1]
    sq = (t * t).astype(jnp.bfloat16)
    outs = []
    for c in range(0, w, MXU_DIM):
        cw = min(MXU_DIM, w - c)
        outs.append(jnp.dot(sq[:, c:c + cw], bd[:cw, :cw], preferred_element_type=jnp.float32))
    return outs[0] if len(outs) == 1 else jnp.concatenate(outs, axis=1)


def _swap_halves(t):
    w = t.shape[1]
    lane = lax.broadcasted_iota(jnp.int32, (t.shape[0], LANES), 1)
    low = (lane % HEAD_DIM) < (HEAD_DIM // 2)
    outs = []
    for c in range(0, w, LANES):
        blk = t[:, c:c + LANES]
        fwd = pltpu.roll(blk, HEAD_DIM // 2, 1)
        bwd = pltpu.roll(blk, LANES - HEAD_DIM // 2, 1)
        outs.append(jnp.where(low, bwd, fwd))
    return outs[0] if len(outs) == 1 else jnp.concatenate(outs, axis=1)


def _tile_lanes(t, reps):
    return t if reps == 1 else jnp.concatenate([t] * reps, axis=1)


def _proj_kernel(x_ref, g_ref, w_ref, bd_ref, qna_ref, kna_ref, qnb_ref, knb_ref, cos_ref, sin_ref,
                 qa_o, ka_o, va_o, sza_o, qb_o, kb_o, vb_o, szb_o, sga_o, sgb_o):
    f32, bf16 = jnp.float32, jnp.bfloat16
    x = x_ref[0]
    ms = jnp.mean(x * x, axis=-1, keepdims=True)
    h = (x * lax.rsqrt(ms + NORM_EPS) * g_ref[...]).astype(bf16)
    bd = bd_ref[...]
    scale = HEAD_DIM ** -0.5 * LOG2_E

    def proj(c0, c1):
        return jnp.dot(h, w_ref[:, c0:c1], preferred_element_type=f32)

    def head_norm(t, gain):
        return t * lax.rsqrt(_segment_mean_sq(t, bd) + NORM_EPS) * gain

    def rotary(t):
        reps = t.shape[1] // LANES
        cos = _tile_lanes(cos_ref[...], reps)
        sin = _tile_lanes(sin_ref[...], reps)
        return t * cos + _swap_halves(t) * sin

    qa_o[0] = (head_norm(proj(C_QA, C_KA), qna_ref[...]) * scale).astype(bf16)
    ka_o[0] = head_norm(proj(C_KA, C_VA), kna_ref[...]).astype(bf16)
    va_o[0] = proj(C_VA, C_ZA).astype(bf16)
    sza_o[0] = jax.nn.silu(proj(C_ZA, C_QB)).astype(bf16)
    qb_o[0] = (rotary(head_norm(proj(C_QB, C_KB), qnb_ref[...])) * scale).astype(bf16)

    kvb = proj(C_KB, C_ZB)
    kb = rotary(head_norm(kvb[:, :SW_KV_WIDTH], knb_ref[...]))
    vb = kvb[:, SW_KV_WIDTH:]
    lane = lax.broadcasted_iota(jnp.int32, kb.shape, 1)
    first = lane < HEAD_DIM

    def replicate(t):
        r = pltpu.roll(t, HEAD_DIM, 1)
        h0 = jnp.where(first, t, r)
        h1 = jnp.where(first, r, t)
        return jnp.concatenate([h0, h0, h1, h1], axis=1)

    kb_o[0] = replicate(kb).astype(bf16)
    vb_o[0] = replicate(vb).astype(bf16)
    szb_o[0] = jax.nn.silu(proj(C_ZB, C_GA)).astype(bf16)
    sga_o[0] = jax.nn.sigmoid(proj(C_GA, C_GB)).astype(bf16)
    sgb_o[0] = jax.nn.sigmoid(proj(C_GB, IN_WIDTH)).astype(bf16)


def _const_spec(shape):
    nd = len(shape)
    return pl.BlockSpec(shape, lambda b, i: (0,) * nd)


def _projection(x, g, w_bf, bd, qna, kna, qnb, knb, cos, sin):
    B, S, _ = x.shape
    tm = PROJ_TILE
    grid = (B, S // tm)
    tok = lambda w: pl.BlockSpec((1, tm, w), lambda b, i: (b, i, 0))
    out_widths = [NA_WIDTH, NA_WIDTH, NA_WIDTH, NA_WIDTH, SW_WIDTH, SW_WIDTH, SW_WIDTH, SW_WIDTH,
                  D_MODEL, D_MODEL]
    return pl.pallas_call(
        _proj_kernel,
        grid=grid,
        in_specs=[
            tok(D_MODEL),
            _const_spec((1, D_MODEL)),
            _const_spec((D_MODEL, IN_WIDTH)),
            _const_spec((MXU_DIM, MXU_DIM)),
            _const_spec((1, NA_WIDTH)),
            _const_spec((1, NA_WIDTH)),
            _const_spec((1, SW_WIDTH)),
            _const_spec((1, SW_KV_WIDTH)),
            pl.BlockSpec((tm, LANES), lambda b, i: (i, 0)),
            pl.BlockSpec((tm, LANES), lambda b, i: (i, 0)),
        ],
        out_specs=[tok(w) for w in out_widths],
        out_shape=[jax.ShapeDtypeStruct((B, S, w), jnp.bfloat16) for w in out_widths],
        compiler_params=pltpu.CompilerParams(
            dimension_semantics=("arbitrary", "arbitrary"), vmem_limit_bytes=VMEM_LIMIT),
        name="proj",
    )(x, g, w_bf, bd, qna, kna, qnb, knb, cos, sin)


def _stack_heads(q, hm_ref):
    rows = q.shape[0]
    return jnp.concatenate([q * hm_ref[hh, :rows, :] for hh in range(HEADS_PER_MXU)], axis=0)


def _take_diag(r, l, rows):
    lane = lax.broadcasted_iota(jnp.int32, (rows, MXU_DIM), 1) // HEAD_DIM
    last = HEADS_PER_MXU - 1
    num = r[last * rows:]
    den = jnp.broadcast_to(l[last * rows:], (rows, MXU_DIM))
    for hh in range(last - 1, -1, -1):
        num = jnp.where(lane == hh, r[hh * rows:(hh + 1) * rows], num)
        den = jnp.where(lane == hh, l[hh * rows:(hh + 1) * rows], den)
    return num / den


def _softmax_rows(s, add, floor, shift):
    n = s.shape[1] // LANES
    sb = [s[:, m * LANES:(m + 1) * LANES] + add(m) for m in range(n)]
    if shift:
        mx = jnp.max(functools.reduce(jnp.maximum, sb), axis=-1, keepdims=True)
        if floor is not None:
            mx = jnp.maximum(mx, floor)
            floor = floor - mx
        sb = [t - mx for t in sb]
    p = [jnp.exp2(t) for t in sb]
    l = jnp.sum(functools.reduce(jnp.add, p), axis=-1, keepdims=True)
    if floor is not None:
        l = l + jnp.exp2(floor)
    return jnp.concatenate(p, axis=1).astype(jnp.bfloat16), l


def _attn_kernel(bounded_ref, qa_ref, sza_ref, qb_ref, szb_ref,
                 kap_ref, kac_ref, kan_ref, vap_ref, vac_ref, van_ref,
                 kbp_ref, kbc_ref, kbn_ref, vbp_ref, vbc_ref, vbn_ref,
                 pb_ref, swm_ref, sink_ref, hm_ref,
                 ya_ref, yb_ref, kaext, vaext, kbext, vbext, *, rows_total, blocks_total):
    f32, bf16 = jnp.float32, jnp.bfloat16
    i = pl.program_id(1)
    nt = (((1,), (1,)), ((), ()))

    kaext[0:NA_HALO] = kap_ref[0]
    kaext[NA_HALO:NA_HALO + ATT_TILE] = kac_ref[0]
    kaext[NA_HALO + ATT_TILE:] = kan_ref[0]
    vaext[0:NA_HALO] = vap_ref[0]
    vaext[NA_HALO:NA_HALO + ATT_TILE] = vac_ref[0]
    vaext[NA_HALO + ATT_TILE:] = van_ref[0]
    kbext[0:SW_PREV] = kbp_ref[0]
    kbext[SW_PREV:SW_PREV + ATT_TILE] = kbc_ref[0]
    kbext[SW_PREV + ATT_TILE:] = kbn_ref[0]
    vbext[0:SW_PREV] = vbp_ref[0]
    vbext[SW_PREV:SW_PREV + ATT_TILE] = vbc_ref[0]
    vbext[SW_PREV + ATT_TILE:] = vbn_ref[0]

    kh = min(NA_KH, rows_total)
    ext_row0 = i * ATT_ROWS - NA_HALO // GRID_W
    win = kh * GRID_W

    def na_row(rl, shift):
        r = i * ATT_ROWS + rl
        krow0 = jnp.clip(r - kh // 2, 0, rows_total - kh)
        off = pl.multiple_of((krow0 - ext_row0) * GRID_W, GRID_W)
        d0 = krow0 - r + (NA_KH - 1)
        qrows = pl.ds(pl.multiple_of(rl * GRID_W, GRID_W), GRID_W)
        for g in range(NA_HEADS // HEADS_PER_MXU):
            cols = slice(g * MXU_DIM, (g + 1) * MXU_DIM)
            qs = _stack_heads(qa_ref[0, qrows, cols], hm_ref)
            s = lax.dot_general(qs, kaext[pl.ds(off, win), cols], nt,
                                preferred_element_type=f32)
            p, l = _softmax_rows(s, lambda m: pb_ref[d0 + 2 * m, g], None, shift)
            rr = jnp.dot(p, vaext[pl.ds(off, win), cols], preferred_element_type=f32)
            o = _take_diag(rr, l, GRID_W)
            ya_ref[0, qrows, cols] = (o * sza_ref[0, qrows, cols].astype(f32)).astype(bf16)

    def sw_block(nl, shift):
        gblk = i * (ATT_TILE // SW_BLOCK) + nl
        var = jnp.where(gblk < SW_EDGE, gblk,
                        jnp.where(gblk >= blocks_total - SW_EDGE,
                                  gblk - (blocks_total - 2 * SW_EDGE - 1), SW_EDGE))
        qrows = pl.ds(pl.multiple_of(nl * SW_BLOCK, SW_BLOCK), SW_BLOCK)
        krows = pl.ds(pl.multiple_of(nl * SW_BLOCK, SW_BLOCK), SW_KWIN)
        for j in range(SW_KV_HEADS):
            cols = slice(j * MXU_DIM, (j + 1) * MXU_DIM)
            qs = _stack_heads(qb_ref[0, qrows, cols], hm_ref)
            s = lax.dot_general(qs, kbext[krows, cols], nt, preferred_element_type=f32)
            p, l = _softmax_rows(
                s,
                lambda m: jnp.concatenate([swm_ref[var, :, m * LANES:(m + 1) * LANES]] * SW_GROUP, axis=0),
                sink_ref[j], shift)
            rr = jnp.dot(p, vbext[krows, cols], preferred_element_type=f32)
            o = _take_diag(rr, l, SW_BLOCK)
            yb_ref[0, qrows, cols] = (o * szb_ref[0, qrows, cols].astype(f32)).astype(bf16)

    @pl.when(bounded_ref[0] != 0)
    def _():
        for rl in range(ATT_TILE // GRID_W):
            na_row(rl, False)
        for nl in range(ATT_TILE // SW_BLOCK):
            sw_block(nl, False)

    @pl.when(bounded_ref[0] == 0)
    def _():
        def na_body(rl, carry):
            na_row(rl, True)
            return carry

        def sw_body(nl, carry):
            sw_block(nl, True)
            return carry

        lax.fori_loop(0, ATT_TILE // GRID_W, na_body, 0)
        lax.fori_loop(0, ATT_TILE // SW_BLOCK, sw_body, 0)


def _attention(bounded, qa, ka, va, sza, qb, kb, vb, szb, pb, swm, sink_col, hm):
    B, S, _ = qa.shape
    tq = ATT_TILE
    grid = (B, S // tq)

    tok = lambda w: pl.BlockSpec((1, tq, w), lambda b, i: (b, i, 0))

    def halo(width, prev_blk, next_blk):
        prev = pl.BlockSpec((1, prev_blk, width),
                            lambda b, i: (b, jnp.maximum(i * (tq // prev_blk) - 1, 0), 0))
        nxt = pl.BlockSpec((1, next_blk, width),
                           lambda b, i: (b, jnp.minimum((i + 1) * (tq // next_blk), S // next_blk - 1), 0))
        return prev, tok(width), nxt

    ka_specs = halo(NA_WIDTH, NA_HALO, NA_HALO)
    kb_specs = halo(SW_WIDTH, SW_PREV, SW_NEXT)

    kern = functools.partial(_attn_kernel, rows_total=S // GRID_W, blocks_total=S // SW_BLOCK)
    return pl.pallas_call(
        kern,
        grid=grid,
        in_specs=[
            pl.BlockSpec(memory_space=pltpu.SMEM),
            tok(NA_WIDTH), tok(NA_WIDTH), tok(SW_WIDTH), tok(SW_WIDTH),
            *ka_specs, *ka_specs, *kb_specs, *kb_specs,
            _const_spec(pb.shape), _const_spec(swm.shape), _const_spec(sink_col.shape),
            _const_spec(hm.shape),
        ],
        out_specs=[tok(NA_WIDTH), tok(SW_WIDTH)],
        out_shape=[jax.ShapeDtypeStruct((B, S, NA_WIDTH), jnp.bfloat16),
                   jax.ShapeDtypeStruct((B, S, SW_WIDTH), jnp.bfloat16)],
        scratch_shapes=[
            pltpu.VMEM((tq + 2 * NA_HALO, NA_WIDTH), jnp.bfloat16),
            pltpu.VMEM((tq + 2 * NA_HALO, NA_WIDTH), jnp.bfloat16),
            pltpu.VMEM((SW_PREV + tq + SW_NEXT, SW_WIDTH), jnp.bfloat16),
            pltpu.VMEM((SW_PREV + tq + SW_NEXT, SW_WIDTH), jnp.bfloat16),
        ],
        compiler_params=pltpu.CompilerParams(
            dimension_semantics=("arbitrary", "arbitrary"), vmem_limit_bytes=VMEM_LIMIT),
        name="attn",
    )(bounded, qa, sza, qb, szb, ka, ka, ka, va, va, va, kb, kb, kb, vb, vb, vb, pb, swm, sink_col, hm)


def _merge_kernel(ya_ref, yb_ref, sga_ref, sgb_ref, x_ref, woa_ref, wob_ref, wo_ref, y_ref):
    f32 = jnp.float32
    pa = jnp.dot(ya_ref[0], woa_ref[...], preferred_element_type=f32)
    pbm = jnp.dot(yb_ref[0], wob_ref[...], preferred_element_type=f32)
    merged = (sga_ref[0].astype(f32) * pa + sgb_ref[0].astype(f32) * pbm).astype(jnp.bfloat16)
    y_ref[0] = x_ref[0] + jnp.dot(merged, wo_ref[...], preferred_element_type=f32)


def _merge(x, ya, yb, sga, sgb, woa, wob, wo):
    B, S, _ = x.shape
    tm = MERGE_TILE
    tok = lambda w: pl.BlockSpec((1, tm, w), lambda b, i: (b, i, 0))
    return pl.pallas_call(
        _merge_kernel,
        grid=(B, S // tm),
        in_specs=[tok(NA_WIDTH), tok(SW_WIDTH), tok(D_MODEL), tok(D_MODEL), tok(D_MODEL),
                  _const_spec(woa.shape), _const_spec(wob.shape), _const_spec(wo.shape)],
        out_specs=tok(D_MODEL),
        out_shape=jax.ShapeDtypeStruct((B, S, D_MODEL), jnp.float32),
        compiler_params=pltpu.CompilerParams(
            dimension_semantics=("arbitrary", "arbitrary"), vmem_limit_bytes=VMEM_LIMIT),
        name="merge",
    )(ya, yb, sga, sgb, x, woa, wob, wo)


def _na_bias_table(rpb):
    c = jnp.arange(GRID_W)
    cs = jnp.clip(c - NA_KW // 2, 0, GRID_W - NA_KW)
    col_in = (c[None, :] >= cs[:, None]) & (c[None, :] < cs[:, None] + NA_KW)
    pad = GRID_W - NA_KW
    rp = jnp.pad(rpb.astype(jnp.float32) * LOG2_E, ((0, 0), (0, 0), (pad, pad)))
    b = jnp.stack([rp[:, :, GRID_W - 1 - q:2 * GRID_W - 1 - q] for q in range(GRID_W)], axis=2)
    b = jnp.where(col_in[None, None], b, NEG_BIG)
    nd = 2 * NA_KH - 2
    pair = jnp.concatenate([b[:, :nd], b[:, 1:nd + 1]], axis=-1)
    pair = pair.reshape(NA_HEADS // HEADS_PER_MXU, HEADS_PER_MXU, nd, GRID_W, LANES)
    return pair.transpose(2, 0, 1, 3, 4).reshape(nd, NA_HEADS // HEADS_PER_MXU,
                                                 HEADS_PER_MXU * GRID_W, LANES)


def _sw_mask_table():
    qi = jnp.arange(SW_BLOCK)[:, None]
    kj = jnp.arange(SW_KWIN)[None, :]
    band = jnp.abs(kj - WINDOW - qi) <= WINDOW
    first = [band & (kj >= WINDOW - e * SW_BLOCK) for e in range(SW_EDGE)]
    last = [band & (kj < (SW_EDGE - e) * SW_BLOCK + WINDOW) for e in range(SW_EDGE)]
    m = jnp.stack(first + [band] + last)
    return jnp.where(m, 0.0, NEG_BIG).astype(jnp.float32)


def _layer(x, p):
    S = x.shape[1]
    outs = _projection(x, p["g"], p["w_in"], p["bd"], p["qna"], p["kna"], p["qnb"], p["knb"],
                       p["cos"][:S], p["sin"][:S])
    qa, ka, va, sza, qb, kb, vb, szb, sga, sgb = outs
    ya, yb = _attention(p["bounded"], qa, ka, va, sza, qb, kb, vb, szb, p["pb"], p["swm"], p["sink"],
                        p["hm"])
    return _merge(x, ya, yb, sga, sgb, p["woa"], p["wob"], p["wo"])


def _prepare(S, norm_g, w_in, qn_a, kn_a, rpb_a, qn_b, kn_b, sink_b, w_out_a, w_out_b, w_o):
    f32, bf16 = jnp.float32, jnp.bfloat16
    seg = jnp.arange(MXU_DIM) // HEAD_DIM
    bd = jnp.where(seg[:, None] == seg[None, :], 1.0 / HEAD_DIM, 0.0).astype(bf16)
    half = HEAD_DIM // 2
    inv = ROPE_THETA ** (-jnp.arange(half, dtype=f32) / half)
    ang = jnp.arange(S, dtype=f32)[:, None] * inv[None, :]
    cos, sin = jnp.cos(ang), jnp.sin(ang)
    cos = jnp.tile(jnp.concatenate([cos, cos], axis=1), (1, LANES // HEAD_DIM))
    sin = jnp.tile(jnp.concatenate([-sin, sin], axis=1), (1, LANES // HEAD_DIM))
    lane_head = jnp.arange(MXU_DIM) // HEAD_DIM
    hm = (lane_head[None, None, :] == jnp.arange(HEADS_PER_MXU)[:, None, None])
    hm = jnp.broadcast_to(hm, (HEADS_PER_MXU, max(SW_BLOCK, GRID_W), MXU_DIM)).astype(bf16)
    sink_col = jnp.repeat((sink_b.astype(f32) * LOG2_E).reshape(SW_KV_HEADS, SW_GROUP), SW_BLOCK, axis=1)
    gain = lambda t: jnp.max(jnp.abs(t.astype(f32)))
    qk_bound = HEAD_DIM ** 0.5 * LOG2_E * (1.0 + 2.0 ** -6)
    bound_a = qk_bound * gain(qn_a) * gain(kn_a) + LOG2_E * gain(rpb_a)
    bound_b = jnp.maximum(qk_bound * gain(qn_b) * gain(kn_b), LOG2_E * gain(sink_b))
    bounded = (jnp.maximum(bound_a, bound_b) <= MAX_UNSHIFTED_LOGIT).astype(jnp.int32).reshape(1)
    return {
        "bounded": bounded,
        "g": norm_g.astype(f32).reshape(1, D_MODEL),
        "w_in": w_in.astype(bf16),
        "bd": bd,
        "qna": jnp.tile(qn_a.astype(f32), NA_HEADS).reshape(1, NA_WIDTH),
        "kna": jnp.tile(kn_a.astype(f32), NA_HEADS).reshape(1, NA_WIDTH),
        "qnb": jnp.tile(qn_b.astype(f32), SW_HEADS).reshape(1, SW_WIDTH),
        "knb": jnp.tile(kn_b.astype(f32), SW_KV_HEADS).reshape(1, SW_KV_WIDTH),
        "cos": cos, "sin": sin,
        "pb": _na_bias_table(rpb_a),
        "swm": _sw_mask_table(),
        "sink": sink_col.reshape(SW_KV_HEADS, SW_GROUP * SW_BLOCK, 1),
        "hm": hm,
        "woa": w_out_a.astype(bf16), "wob": w_out_b.astype(bf16), "wo": w_o.astype(bf16),
    }


def kernel(x_prompt, x_sample, norm_g, w_in, qn_a, kn_a, rpb_a, qn_b, kn_b, sink_b, w_out_a, w_out_b, w_o):
    depth = norm_g.shape[0]
    y_prompt, y_sample = x_prompt, x_sample
    S = max(x_prompt.shape[1], x_sample.shape[1])
    for l in range(depth):
        p = _prepare(S, norm_g[l], w_in[l], qn_a[l], kn_a[l], rpb_a[l], qn_b[l], kn_b[l], sink_b[l],
                     w_out_a[l], w_out_b[l], w_o[l])
        y_prompt = _layer(y_prompt, p)
        y_sample = _layer(y_sample, p)
    return (y_prompt, y_sample)
```

```python
import functools

import jax
import jax.numpy as jnp
from jax import lax
from jax.experimental import pallas as pl
from jax.experimental.pallas import tpu as pltpu

D_MODEL = 1024
HEAD_DIM = 64
GRID_W = 64
NA_HEADS = 8
NA_KH = 8
NA_KW = 16
NA_WIDTH = NA_HEADS * HEAD_DIM
SW_HEADS = 8
SW_KV_HEADS = 2
SW_GROUP = SW_HEADS // SW_KV_HEADS
SW_WIDTH = SW_HEADS * HEAD_DIM
SW_KV_WIDTH = SW_KV_HEADS * HEAD_DIM
WINDOW = 128
ROPE_THETA = 10000.0
NORM_EPS = 1e-6

C_QA = 0
C_KA = C_QA + NA_WIDTH
C_VA = C_KA + NA_WIDTH
C_ZA = C_VA + NA_WIDTH
C_QB = C_ZA + NA_WIDTH
C_KB = C_QB + SW_WIDTH
C_VB = C_KB + SW_KV_WIDTH
C_ZB = C_VB + SW_KV_WIDTH
C_GA = C_ZB + SW_WIDTH
C_GB = C_GA + D_MODEL
IN_WIDTH = C_GB + D_MODEL

LANES = 128
MXU_DIM = 256
HEADS_PER_MXU = MXU_DIM // HEAD_DIM
NEG_BIG = -1e30
LOG2_E = 1.4426950408889634
MAX_UNSHIFTED_LOGIT = 64.0
VMEM_LIMIT = 56 * 1024 * 1024

PROJ_TILE = 1024
ATT_TILE = 1024
ATT_ROWS = ATT_TILE // GRID_W
MERGE_TILE = 1024
NA_HALO = 256
SW_BLOCK = 64
SW_KWIN = 384
SW_PREV = WINDOW
SW_NEXT = 256
SW_EDGE = WINDOW // SW_BLOCK


def _segment_mean_sq(t, bd):
    w = t.shape[1]
    sq = (t * t).astype(jnp.bfloat16)
    outs = []
    for c in range(0, w, MXU_DIM):
        cw = min(MXU_DIM, w - c)
        outs.append(jnp.dot(sq[:, c:c + cw], bd[:cw, :cw], preferred_element_type=jnp.float32))
    return outs[0] if len(outs) == 1 else jnp.concatenate(outs, axis=1)


def _swap_halves(t):
    w = t.shape[1]
    lane = lax.broadcasted_iota(jnp.int32, (t.shape[0], LANES), 1)
    low = (lane % HEAD_DIM) < (HEAD_DIM // 2)
    outs = []
    for c in range(0, w, LANES):
        blk = t[:, c:c + LANES]
        fwd = pltpu.roll(blk, HEAD_DIM // 2, 1)
        bwd = pltpu.roll(blk, LANES - HEAD_DIM // 2, 1)
        outs.append(jnp.where(low, bwd, fwd))
    return outs[0] if len(outs) == 1 else jnp.concatenate(outs, axis=1)


def _tile_lanes(t, reps):
    return t if reps == 1 else jnp.concatenate([t] * reps, axis=1)


def _proj_kernel(x_ref, g_ref, w_ref, bd_ref, qna_ref, kna_ref, qnb_ref, knb_ref, cos_ref, sin_ref,
                 qa_o, ka_o, va_o, sza_o, qb_o, kb_o, vb_o, szb_o):
    f32, bf16 = jnp.float32, jnp.bfloat16
    x = x_ref[0]
    ms = jnp.mean(x * x, axis=-1, keepdims=True)
    h = (x * lax.rsqrt(ms + NORM_EPS) * g_ref[...]).astype(bf16)
    bd = bd_ref[...]
    scale = HEAD_DIM ** -0.5 * LOG2_E

    def proj(c0, c1):
        return jnp.dot(h, w_ref[:, c0:c1], preferred_element_type=f32)

    def head_norm(t, gain):
        return t * lax.rsqrt(_segment_mean_sq(t, bd) + NORM_EPS) * gain

    def rotary(t):
        reps = t.shape[1] // LANES
        cos = _tile_lanes(cos_ref[...], reps)
        sin = _tile_lanes(sin_ref[...], reps)
        return t * cos + _swap_halves(t) * sin

    qa_o[0] = (head_norm(proj(C_QA, C_KA), qna_ref[...]) * scale).astype(bf16)
    ka_o[0] = head_norm(proj(C_KA, C_VA), kna_ref[...]).astype(bf16)
    va_o[0] = proj(C_VA, C_ZA).astype(bf16)
    sza_o[0] = jax.nn.silu(proj(C_ZA, C_QB)).astype(bf16)
    qb_o[0] = (rotary(head_norm(proj(C_QB, C_KB), qnb_ref[...])) * scale).astype(bf16)

    kvb = proj(C_KB, C_ZB)
    kb = rotary(head_norm(kvb[:, :SW_KV_WIDTH], knb_ref[...]))
    vb = kvb[:, SW_KV_WIDTH:]
    lane = lax.broadcasted_iota(jnp.int32, kb.shape, 1)
    first = lane < HEAD_DIM

    def replicate(t):
        r = pltpu.roll(t, HEAD_DIM, 1)
        h0 = jnp.where(first, t, r)
        h1 = jnp.where(first, r, t)
        return jnp.concatenate([h0, h0, h1, h1], axis=1)

    kb_o[0] = replicate(kb).astype(bf16)
    vb_o[0] = replicate(vb).astype(bf16)
    szb_o[0] = jax.nn.silu(proj(C_ZB, C_GA)).astype(bf16)


def _const_spec(shape):
    nd = len(shape)
    return pl.BlockSpec(shape, lambda b, i: (0,) * nd, pipeline_mode=pl.Buffered(1))


def _projection(x, g, w_bf, bd, qna, kna, qnb, knb, cos, sin):
    B, S, _ = x.shape
    tm = PROJ_TILE
    grid = (B, S // tm)
    tok = lambda w: pl.BlockSpec((1, tm, w), lambda b, i: (b, i, 0))
    out_widths = [NA_WIDTH, NA_WIDTH, NA_WIDTH, NA_WIDTH, SW_WIDTH, SW_WIDTH, SW_WIDTH, SW_WIDTH]
    return pl.pallas_call(
        _proj_kernel,
        grid=grid,
        in_specs=[
            tok(D_MODEL),
            _const_spec((1, D_MODEL)),
            _const_spec((D_MODEL, C_GA)),
            _const_spec((MXU_DIM, MXU_DIM)),
            _const_spec((1, NA_WIDTH)),
            _const_spec((1, NA_WIDTH)),
            _const_spec((1, SW_WIDTH)),
            _const_spec((1, SW_KV_WIDTH)),
            pl.BlockSpec((tm, LANES), lambda b, i: (i, 0)),
            pl.BlockSpec((tm, LANES), lambda b, i: (i, 0)),
        ],
        out_specs=[tok(w) for w in out_widths],
        out_shape=[jax.ShapeDtypeStruct((B, S, w), jnp.bfloat16) for w in out_widths],
        compiler_params=pltpu.CompilerParams(
            dimension_semantics=("arbitrary", "arbitrary"), vmem_limit_bytes=VMEM_LIMIT),
        name="proj",
    )(x, g, w_bf, bd, qna, kna, qnb, knb, cos, sin)


def _stack_heads(q, hm_ref):
    rows = q.shape[0]
    return jnp.concatenate([q * hm_ref[hh, :rows, :] for hh in range(HEADS_PER_MXU)], axis=0)


def _take_diag(r, l, rows):
    lane = lax.broadcasted_iota(jnp.int32, (rows, MXU_DIM), 1) // HEAD_DIM
    last = HEADS_PER_MXU - 1
    num = r[last * rows:]
    den = jnp.broadcast_to(l[last * rows:], (rows, MXU_DIM))
    for hh in range(last - 1, -1, -1):
        num = jnp.where(lane == hh, r[hh * rows:(hh + 1) * rows], num)
        den = jnp.where(lane == hh, l[hh * rows:(hh + 1) * rows], den)
    return num / den


def _softmax_rows(s, add, floor, shift):
    n = s.shape[1] // LANES
    sb = [s[:, m * LANES:(m + 1) * LANES] + add(m) for m in range(n)]
    if shift:
        mx = jnp.max(functools.reduce(jnp.maximum, sb), axis=-1, keepdims=True)
        if floor is not None:
            mx = jnp.maximum(mx, floor)
            floor = floor - mx
        sb = [t - mx for t in sb]
    p = [jnp.exp2(t) for t in sb]
    l = jnp.sum(functools.reduce(jnp.add, p), axis=-1, keepdims=True)
    if floor is not None:
        l = l + jnp.exp2(floor)
    return jnp.concatenate(p, axis=1).astype(jnp.bfloat16), l


def _attn_kernel(bounded_ref, qa_ref, sza_ref, qb_ref, szb_ref,
                 kap_ref, kac_ref, kan_ref, vap_ref, vac_ref, van_ref,
                 kbp_ref, kbc_ref, kbn_ref, vbp_ref, vbc_ref, vbn_ref,
                 pb_ref, swm_ref, sink_ref, hm_ref,
                 ya_ref, yb_ref, kaext, vaext, kbext, vbext, *, rows_total, blocks_total):
    f32, bf16 = jnp.float32, jnp.bfloat16
    i = pl.program_id(1)
    nt = (((1,), (1,)), ((), ()))

    kaext[0:NA_HALO] = kap_ref[0]
    kaext[NA_HALO:NA_HALO + ATT_TILE] = kac_ref[0]
    kaext[NA_HALO + ATT_TILE:] = kan_ref[0]
    vaext[0:NA_HALO] = vap_ref[0]
    vaext[NA_HALO:NA_HALO + ATT_TILE] = vac_ref[0]
    vaext[NA_HALO + ATT_TILE:] = van_ref[0]
    kbext[0:SW_PREV] = kbp_ref[0]
    kbext[SW_PREV:SW_PREV + ATT_TILE] = kbc_ref[0]
    kbext[SW_PREV + ATT_TILE:] = kbn_ref[0]
    vbext[0:SW_PREV] = vbp_ref[0]
    vbext[SW_PREV:SW_PREV + ATT_TILE] = vbc_ref[0]
    vbext[SW_PREV + ATT_TILE:] = vbn_ref[0]

    kh = min(NA_KH, rows_total)
    ext_row0 = i * ATT_ROWS - NA_HALO // GRID_W
    win = kh * GRID_W

    def na_row(rl, shift):
        r = i * ATT_ROWS + rl
        krow0 = jnp.clip(r - kh // 2, 0, rows_total - kh)
        off = pl.multiple_of((krow0 - ext_row0) * GRID_W, GRID_W)
        d0 = krow0 - r + (NA_KH - 1)
        qrows = pl.ds(pl.multiple_of(rl * GRID_W, GRID_W), GRID_W)
        for g in range(NA_HEADS // HEADS_PER_MXU):
            cols = slice(g * MXU_DIM, (g + 1) * MXU_DIM)
            qs = _stack_heads(qa_ref[0, qrows, cols], hm_ref)
            s = lax.dot_general(qs, kaext[pl.ds(off, win), cols], nt,
                                preferred_element_type=f32)
            p, l = _softmax_rows(s, lambda m: pb_ref[d0 + 2 * m, g], None, shift)
            rr = jnp.dot(p, vaext[pl.ds(off, win), cols], preferred_element_type=f32)
            o = _take_diag(rr, l, GRID_W)
            ya_ref[0, qrows, cols] = (o * sza_ref[0, qrows, cols].astype(f32)).astype(bf16)

    def sw_block(nl, shift):
        gblk = i * (ATT_TILE // SW_BLOCK) + nl
        var = jnp.where(gblk < SW_EDGE, gblk,
                        jnp.where(gblk >= blocks_total - SW_EDGE,
                                  gblk - (blocks_total - 2 * SW_EDGE - 1), SW_EDGE))
        qrows = pl.ds(pl.multiple_of(nl * SW_BLOCK, SW_BLOCK), SW_BLOCK)
        krows = pl.ds(pl.multiple_of(nl * SW_BLOCK, SW_BLOCK), SW_KWIN)
        for j in range(SW_KV_HEADS):
            cols = slice(j * MXU_DIM, (j + 1) * MXU_DIM)
            qs = _stack_heads(qb_ref[0, qrows, cols], hm_ref)
            s = lax.dot_general(qs, kbext[krows, cols], nt, preferred_element_type=f32)
            p, l = _softmax_rows(
                s,
                lambda m: jnp.concatenate([swm_ref[var, :, m * LANES:(m + 1) * LANES]] * SW_GROUP, axis=0),
                sink_ref[j], shift)
            rr = jnp.dot(p, vbext[krows, cols], preferred_element_type=f32)
            o = _take_diag(rr, l, SW_BLOCK)
            yb_ref[0, qrows, cols] = (o * szb_ref[0, qrows, cols].astype(f32)).astype(bf16)

    @pl.when(bounded_ref[0] != 0)
    def _():
        for rl in range(ATT_TILE // GRID_W):
            na_row(rl, False)
        for nl in range(ATT_TILE // SW_BLOCK):
            sw_block(nl, False)

    @pl.when(bounded_ref[0] == 0)
    def _():
        def na_body(rl, carry):
            na_row(rl, True)
            return carry

        def sw_body(nl, carry):
            sw_block(nl, True)
            return carry

        lax.fori_loop(0, ATT_TILE // GRID_W, na_body, 0)
        lax.fori_loop(0, ATT_TILE // SW_BLOCK, sw_body, 0)


def _attention(bounded, qa, ka, va, sza, qb, kb, vb, szb, pb, swm, sink_col, hm):
    B, S, _ = qa.shape
    tq = ATT_TILE
    grid = (B, S // tq)

    tok = lambda w: pl.BlockSpec((1, tq, w), lambda b, i: (b, i, 0))

    def halo(width, prev_blk, next_blk):
        prev = pl.BlockSpec((1, prev_blk, width),
                            lambda b, i: (b, jnp.maximum(i * (tq // prev_blk) - 1, 0), 0))
        nxt = pl.BlockSpec((1, next_blk, width),
                           lambda b, i: (b, jnp.minimum((i + 1) * (tq // next_blk), S // next_blk - 1), 0))
        return prev, tok(width), nxt

    ka_specs = halo(NA_WIDTH, NA_HALO, NA_HALO)
    kb_specs = halo(SW_WIDTH, SW_PREV, SW_NEXT)

    kern = functools.partial(_attn_kernel, rows_total=S // GRID_W, blocks_total=S // SW_BLOCK)
    return pl.pallas_call(
        kern,
        grid=grid,
        in_specs=[
            pl.BlockSpec(memory_space=pltpu.SMEM),
            tok(NA_WIDTH), tok(NA_WIDTH), tok(SW_WIDTH), tok(SW_WIDTH),
            *ka_specs, *ka_specs, *kb_specs, *kb_specs,
            _const_spec(pb.shape), _const_spec(swm.shape), _const_spec(sink_col.shape),
            _const_spec(hm.shape),
        ],
        out_specs=[tok(NA_WIDTH), tok(SW_WIDTH)],
        out_shape=[jax.ShapeDtypeStruct((B, S, NA_WIDTH), jnp.bfloat16),
                   jax.ShapeDtypeStruct((B, S, SW_WIDTH), jnp.bfloat16)],
        scratch_shapes=[
            pltpu.VMEM((tq + 2 * NA_HALO, NA_WIDTH), jnp.bfloat16),
            pltpu.VMEM((tq + 2 * NA_HALO, NA_WIDTH), jnp.bfloat16),
            pltpu.VMEM((SW_PREV + tq + SW_NEXT, SW_WIDTH), jnp.bfloat16),
            pltpu.VMEM((SW_PREV + tq + SW_NEXT, SW_WIDTH), jnp.bfloat16),
        ],
        compiler_params=pltpu.CompilerParams(
            dimension_semantics=("arbitrary", "arbitrary"), vmem_limit_bytes=VMEM_LIMIT),
        name="attn",
    )(bounded, qa, sza, qb, szb, ka, ka, ka, va, va, va, kb, kb, kb, vb, vb, vb, pb, swm, sink_col, hm)


def _merge_kernel(ya_ref, yb_ref, x_ref, g_ref, wg_ref, woa_ref, wob_ref, wo_ref, y_ref):
    f32, bf16 = jnp.float32, jnp.bfloat16
    x = x_ref[0]
    ms = jnp.mean(x * x, axis=-1, keepdims=True)
    h = (x * lax.rsqrt(ms + NORM_EPS) * g_ref[...]).astype(bf16)
    ga = jax.nn.sigmoid(jnp.dot(h, wg_ref[:, :D_MODEL], preferred_element_type=f32))
    gb = jax.nn.sigmoid(jnp.dot(h, wg_ref[:, D_MODEL:], preferred_element_type=f32))
    pa = jnp.dot(ya_ref[0], woa_ref[...], preferred_element_type=f32)
    pbm = jnp.dot(yb_ref[0], wob_ref[...], preferred_element_type=f32)
    merged = (ga * pa + gb * pbm).astype(bf16)
    y_ref[0] = x + jnp.dot(merged, wo_ref[...], preferred_element_type=f32)


def _merge(x, ya, yb, g, wg, woa, wob, wo):
    B, S, _ = x.shape
    tm = MERGE_TILE
    tok = lambda w: pl.BlockSpec((1, tm, w), lambda b, i: (b, i, 0))
    return pl.pallas_call(
        _merge_kernel,
        grid=(B, S // tm),
        in_specs=[tok(NA_WIDTH), tok(SW_WIDTH), tok(D_MODEL), _const_spec(g.shape),
                  _const_spec(wg.shape), _const_spec(woa.shape), _const_spec(wob.shape),
                  _const_spec(wo.shape)],
        out_specs=tok(D_MODEL),
        out_shape=jax.ShapeDtypeStruct((B, S, D_MODEL), jnp.float32),
        compiler_params=pltpu.CompilerParams(
            dimension_semantics=("arbitrary", "arbitrary"), vmem_limit_bytes=VMEM_LIMIT),
        name="merge",
    )(ya, yb, x, g, wg, woa, wob, wo)


def _na_bias_table(rpb):
    c = jnp.arange(GRID_W)
    cs = jnp.clip(c - NA_KW // 2, 0, GRID_W - NA_KW)
    col_in = (c[None, :] >= cs[:, None]) & (c[None, :] < cs[:, None] + NA_KW)
    pad = GRID_W - NA_KW
    rp = jnp.pad(rpb.astype(jnp.float32) * LOG2_E, ((0, 0), (0, 0), (pad, pad)))
    b = jnp.stack([rp[:, :, GRID_W - 1 - q:2 * GRID_W - 1 - q] for q in range(GRID_W)], axis=2)
    b = jnp.where(col_in[None, None], b, NEG_BIG)
    nd = 2 * NA_KH - 2
    pair = jnp.concatenate([b[:, :nd], b[:, 1:nd + 1]], axis=-1)
    pair = pair.reshape(NA_HEADS // HEADS_PER_MXU, HEADS_PER_MXU, nd, GRID_W, LANES)
    return pair.transpose(2, 0, 1, 3, 4).reshape(nd, NA_HEADS // HEADS_PER_MXU,
                                                 HEADS_PER_MXU * GRID_W, LANES)


def _sw_mask_table():
    qi = jnp.arange(SW_BLOCK)[:, None]
    kj = jnp.arange(SW_KWIN)[None, :]
    band = jnp.abs(kj - WINDOW - qi) <= WINDOW
    first = [band & (kj >= WINDOW - e * SW_BLOCK) for e in range(SW_EDGE)]
    last = [band & (kj < (SW_EDGE - e) * SW_BLOCK + WINDOW) for e in range(SW_EDGE)]
    m = jnp.stack(first + [band] + last)
    return jnp.where(m, 0.0, NEG_BIG).astype(jnp.float32)


def _layer(x, p):
    S = x.shape[1]
    outs = _projection(x, p["g"], p["w_in"], p["bd"], p["qna"], p["kna"], p["qnb"], p["knb"],
                       p["cos"][:S], p["sin"][:S])
    qa, ka, va, sza, qb, kb, vb, szb = outs
    ya, yb = _attention(p["bounded"], qa, ka, va, sza, qb, kb, vb, szb, p["pb"], p["swm"], p["sink"],
                        p["hm"])
    return _merge(x, ya, yb, p["g"], p["w_gate"], p["woa"], p["wob"], p["wo"])


def _prepare(S, norm_g, w_in, qn_a, kn_a, rpb_a, qn_b, kn_b, sink_b, w_out_a, w_out_b, w_o):
    f32, bf16 = jnp.float32, jnp.bfloat16
    seg = jnp.arange(MXU_DIM) // HEAD_DIM
    bd = jnp.where(seg[:, None] == seg[None, :], 1.0 / HEAD_DIM, 0.0).astype(bf16)
    half = HEAD_DIM // 2
    inv = ROPE_THETA ** (-jnp.arange(half, dtype=f32) / half)
    ang = jnp.arange(S, dtype=f32)[:, None] * inv[None, :]
    cos, sin = jnp.cos(ang), jnp.sin(ang)
    cos = jnp.tile(jnp.concatenate([cos, cos], axis=1), (1, LANES // HEAD_DIM))
    sin = jnp.tile(jnp.concatenate([-sin, sin], axis=1), (1, LANES // HEAD_DIM))
    lane_head = jnp.arange(MXU_DIM) // HEAD_DIM
    hm = (lane_head[None, None, :] == jnp.arange(HEADS_PER_MXU)[:, None, None])
    hm = jnp.broadcast_to(hm, (HEADS_PER_MXU, max(SW_BLOCK, GRID_W), MXU_DIM)).astype(bf16)
    sink_col = jnp.repeat((sink_b.astype(f32) * LOG2_E).reshape(SW_KV_HEADS, SW_GROUP), SW_BLOCK, axis=1)
    gain = lambda t: jnp.max(jnp.abs(t.astype(f32)))
    qk_bound = HEAD_DIM ** 0.5 * LOG2_E * (1.0 + 2.0 ** -6)
    bound_a = qk_bound * gain(qn_a) * gain(kn_a) + LOG2_E * gain(rpb_a)
    bound_b = jnp.maximum(qk_bound * gain(qn_b) * gain(kn_b), LOG2_E * gain(sink_b))
    bounded = (jnp.maximum(bound_a, bound_b) <= MAX_UNSHIFTED_LOGIT).astype(jnp.int32).reshape(1)
    return {
        "bounded": bounded,
        "g": norm_g.astype(f32).reshape(1, D_MODEL),
        "w_in": w_in[:, :C_GA].astype(bf16),
        "w_gate": w_in[:, C_GA:].astype(bf16),
        "bd": bd,
        "qna": jnp.tile(qn_a.astype(f32), NA_HEADS).reshape(1, NA_WIDTH),
        "kna": jnp.tile(kn_a.astype(f32), NA_HEADS).reshape(1, NA_WIDTH),
        "qnb": jnp.tile(qn_b.astype(f32), SW_HEADS).reshape(1, SW_WIDTH),
        "knb": jnp.tile(kn_b.astype(f32), SW_KV_HEADS).reshape(1, SW_KV_WIDTH),
        "cos": cos, "sin": sin,
        "pb": _na_bias_table(rpb_a),
        "swm": _sw_mask_table(),
        "sink": sink_col.reshape(SW_KV_HEADS, SW_GROUP * SW_BLOCK, 1),
        "hm": hm,
        "woa": w_out_a.astype(bf16), "wob": w_out_b.astype(bf16), "wo": w_o.astype(bf16),
    }


def kernel(x_prompt, x_sample, norm_g, w_in, qn_a, kn_a, rpb_a, qn_b, kn_b, sink_b, w_out_a, w_out_b, w_o):
    depth = norm_g.shape[0]
    y_prompt, y_sample = x_prompt, x_sample
    S = max(x_prompt.shape[1], x_sample.shape[1])
    for l in range(depth):
        p = _prepare(S, norm_g[l], w_in[l], qn_a[l], kn_a[l], rpb_a[l], qn_b[l], kn_b[l], sink_b[l],
                     w_out_a[l], w_out_b[l], w_o[l])
        y_prompt = _layer(y_prompt, p)
        y_sample = _layer(y_sample, p)
    return (y_prompt, y_sample)
```

```python
import functools

import jax
import jax.numpy as jnp
from jax import lax
from jax.experimental import pallas as pl
from jax.experimental.pallas import tpu as pltpu

D_MODEL = 1024
HEAD_DIM = 64
GRID_W = 64
NA_HEADS = 8
NA_KH = 8
NA_KW = 16
NA_WIDTH = NA_HEADS * HEAD_DIM
SW_HEADS = 8
SW_KV_HEADS = 2
SW_GROUP = SW_HEADS // SW_KV_HEADS
SW_WIDTH = SW_HEADS * HEAD_DIM
SW_KV_WIDTH = SW_KV_HEADS * HEAD_DIM
WINDOW = 128
ROPE_THETA = 10000.0
NORM_EPS = 1e-6

C_QA = 0
C_KA = C_QA + NA_WIDTH
C_VA = C_KA + NA_WIDTH
C_ZA = C_VA + NA_WIDTH
C_QB = C_ZA + NA_WIDTH
C_KB = C_QB + SW_WIDTH
C_VB = C_KB + SW_KV_WIDTH
C_ZB = C_VB + SW_KV_WIDTH
C_GA = C_ZB + SW_WIDTH
C_GB = C_GA + D_MODEL
IN_WIDTH = C_GB + D_MODEL

LANES = 128
MXU_DIM = 256
HEADS_PER_MXU = MXU_DIM // HEAD_DIM
NEG_BIG = -1e30
LOG2_E = 1.4426950408889634
MAX_UNSHIFTED_LOGIT = 64.0
VMEM_LIMIT = 56 * 1024 * 1024

PROJ_TILE = 1024
ATT_TILE = 1024
ATT_ROWS = ATT_TILE // GRID_W
MERGE_TILE = 1024
NA_HALO = 256
NA_EXT = ATT_TILE + 2 * NA_HALO
SW_BLOCK = 64
SW_KWIN = 384
SW_EXT = ATT_TILE + SW_KWIN - SW_BLOCK
SW_SHIFTS = (SW_KWIN - SW_BLOCK) // SW_BLOCK + 1


def _segment_mean_sq(t, bd):
    w = t.shape[1]
    sq = (t * t).astype(jnp.bfloat16)
    outs = []
    for c in range(0, w, MXU_DIM):
        cw = min(MXU_DIM, w - c)
        outs.append(jnp.dot(sq[:, c:c + cw], bd[:cw, :cw], preferred_element_type=jnp.float32))
    return outs[0] if len(outs) == 1 else jnp.concatenate(outs, axis=1)


def _swap_halves(t):
    w = t.shape[1]
    lane = lax.broadcasted_iota(jnp.int32, (t.shape[0], LANES), 1)
    low = (lane % HEAD_DIM) < (HEAD_DIM // 2)
    outs = []
    for c in range(0, w, LANES):
        blk = t[:, c:c + LANES]
        fwd = pltpu.roll(blk, HEAD_DIM // 2, 1)
        bwd = pltpu.roll(blk, LANES - HEAD_DIM // 2, 1)
        outs.append(jnp.where(low, bwd, fwd))
    return outs[0] if len(outs) == 1 else jnp.concatenate(outs, axis=1)


def _tile_lanes(t, reps):
    return t if reps == 1 else jnp.concatenate([t] * reps, axis=1)


def _proj_kernel(x_ref, g_ref, w_ref, bd_ref, qna_ref, kna_ref, qnb_ref, knb_ref, cos_ref, sin_ref,
                 qa_o, ka_o, va_o, sza_o, qb_o, kb_o, vb_o, szb_o):
    f32, bf16 = jnp.float32, jnp.bfloat16
    x = x_ref[0]
    ms = jnp.mean(x * x, axis=-1, keepdims=True)
    h = (x * lax.rsqrt(ms + NORM_EPS) * g_ref[...]).astype(bf16)
    bd = bd_ref[...]
    scale = HEAD_DIM ** -0.5 * LOG2_E

    def proj(c0, c1):
        return jnp.dot(h, w_ref[:, c0:c1], preferred_element_type=f32)

    def head_norm(t, gain):
        return t * lax.rsqrt(_segment_mean_sq(t, bd) + NORM_EPS) * gain

    def rotary(t):
        reps = t.shape[1] // LANES
        cos = _tile_lanes(cos_ref[...], reps)
        sin = _tile_lanes(sin_ref[...], reps)
        return t * cos + _swap_halves(t) * sin

    qa_o[0] = (head_norm(proj(C_QA, C_KA), qna_ref[...]) * scale).astype(bf16)
    ka_o[0] = head_norm(proj(C_KA, C_VA), kna_ref[...]).astype(bf16)
    va_o[0] = proj(C_VA, C_ZA).astype(bf16)
    sza_o[0] = jax.nn.silu(proj(C_ZA, C_QB)).astype(bf16)
    qb_o[0] = (rotary(head_norm(proj(C_QB, C_KB), qnb_ref[...])) * scale).astype(bf16)

    kvb = proj(C_KB, C_ZB)
    kb = rotary(head_norm(kvb[:, :SW_KV_WIDTH], knb_ref[...]))
    vb = kvb[:, SW_KV_WIDTH:]
    lane = lax.broadcasted_iota(jnp.int32, kb.shape, 1)
    first = lane < HEAD_DIM

    def replicate(t):
        r = pltpu.roll(t, HEAD_DIM, 1)
        h0 = jnp.where(first, t, r)
        h1 = jnp.where(first, r, t)
        return jnp.concatenate([h0, h0, h1, h1], axis=1)

    kb_o[0] = replicate(kb).astype(bf16)
    vb_o[0] = replicate(vb).astype(bf16)
    szb_o[0] = jax.nn.silu(proj(C_ZB, C_GA)).astype(bf16)


def _const_spec(shape, single=False):
    nd = len(shape)
    mode = {"pipeline_mode": pl.Buffered(1)} if single else {}
    return pl.BlockSpec(shape, lambda b, i: (0,) * nd, **mode)


def _projection(x, g, w_bf, bd, qna, kna, qnb, knb, cos, sin):
    B, S, _ = x.shape
    tm = PROJ_TILE
    grid = (B, S // tm)
    tok = lambda w: pl.BlockSpec((1, tm, w), lambda b, i: (b, i, 0))
    out_widths = [NA_WIDTH, NA_WIDTH, NA_WIDTH, NA_WIDTH, SW_WIDTH, SW_WIDTH, SW_WIDTH, SW_WIDTH]
    return pl.pallas_call(
        _proj_kernel,
        grid=grid,
        in_specs=[
            tok(D_MODEL),
            _const_spec((1, D_MODEL)),
            _const_spec((D_MODEL, C_GA), single=True),
            _const_spec((MXU_DIM, MXU_DIM)),
            _const_spec((1, NA_WIDTH)),
            _const_spec((1, NA_WIDTH)),
            _const_spec((1, SW_WIDTH)),
            _const_spec((1, SW_KV_WIDTH)),
            pl.BlockSpec((tm, LANES), lambda b, i: (i, 0)),
            pl.BlockSpec((tm, LANES), lambda b, i: (i, 0)),
        ],
        out_specs=[tok(w) for w in out_widths],
        out_shape=[jax.ShapeDtypeStruct((B, S, w), jnp.bfloat16) for w in out_widths],
        compiler_params=pltpu.CompilerParams(
            dimension_semantics=("arbitrary", "arbitrary"), vmem_limit_bytes=VMEM_LIMIT),
        name="proj",
    )(x, g, w_bf, bd, qna, kna, qnb, knb, cos, sin)


def _stack_heads(q, hm_ref):
    rows = q.shape[0]
    return jnp.concatenate([q * hm_ref[hh, :rows, :] for hh in range(HEADS_PER_MXU)], axis=0)


def _take_diag(r, l, rows):
    lane = lax.broadcasted_iota(jnp.int32, (rows, MXU_DIM), 1) // HEAD_DIM
    last = HEADS_PER_MXU - 1
    num = r[last * rows:]
    den = jnp.broadcast_to(l[last * rows:], (rows, MXU_DIM))
    for hh in range(last - 1, -1, -1):
        num = jnp.where(lane == hh, r[hh * rows:(hh + 1) * rows], num)
        den = jnp.where(lane == hh, l[hh * rows:(hh + 1) * rows], den)
    return num / den


def _softmax_rows(s, add, floor, shift):
    n = s.shape[1] // LANES
    sb = [s[:, m * LANES:(m + 1) * LANES] + add(m) for m in range(n)]
    if shift:
        mx = jnp.max(functools.reduce(jnp.maximum, sb), axis=-1, keepdims=True)
        if floor is not None:
            mx = jnp.maximum(mx, floor)
            floor = floor - mx
        sb = [t - mx for t in sb]
    p = [jnp.exp2(t) for t in sb]
    l = jnp.sum(functools.reduce(jnp.add, p), axis=-1, keepdims=True)
    if floor is not None:
        l = l + jnp.exp2(floor)
    return jnp.concatenate(p, axis=1).astype(jnp.bfloat16), l


def _attn_kernel(bounded_ref, qa_ref, sza_ref, qb_ref, szb_ref, ka_ref, va_ref, kb_ref, vb_ref,
                 pb_ref, swm_ref, sink_ref, hm_ref, ya_ref, yb_ref, *, seq_len):
    f32, bf16 = jnp.float32, jnp.bfloat16
    i = pl.program_id(1)
    nt = (((1,), (1,)), ((), ()))
    rows_total = seq_len // GRID_W
    kh = min(NA_KH, rows_total)
    win = kh * GRID_W
    na_start = _window_start(i, NA_HALO, NA_EXT, seq_len)
    sw_start = _window_start(i, WINDOW, SW_EXT, seq_len)

    def na_row(rl, shift):
        r = i * ATT_ROWS + rl
        krow0 = jnp.clip(r - kh // 2, 0, rows_total - kh)
        off = pl.multiple_of(krow0 * GRID_W - na_start, GRID_W)
        d0 = krow0 - r + (NA_KH - 1)
        qrows = pl.ds(pl.multiple_of(rl * GRID_W, GRID_W), GRID_W)
        for g in range(NA_HEADS // HEADS_PER_MXU):
            cols = slice(g * MXU_DIM, (g + 1) * MXU_DIM)
            qs = _stack_heads(qa_ref[0, qrows, cols], hm_ref)
            s = lax.dot_general(qs, ka_ref[pl.ds(off, win), cols], nt,
                                preferred_element_type=f32)
            p, l = _softmax_rows(s, lambda m: pb_ref[d0 + 2 * m, g], None, shift)
            rr = jnp.dot(p, va_ref[pl.ds(off, win), cols], preferred_element_type=f32)
            o = _take_diag(rr, l, GRID_W)
            ya_ref[0, qrows, cols] = (o * sza_ref[0, qrows, cols].astype(f32)).astype(bf16)

    def sw_block(nl, shift):
        q0 = i * ATT_TILE + nl * SW_BLOCK
        w0 = jnp.clip(q0 - WINDOW, 0, seq_len - SW_KWIN)
        var = (q0 - w0) // SW_BLOCK
        qrows = pl.ds(pl.multiple_of(nl * SW_BLOCK, SW_BLOCK), SW_BLOCK)
        krows = pl.ds(pl.multiple_of(w0 - sw_start, SW_BLOCK), SW_KWIN)
        for j in range(SW_KV_HEADS):
            cols = slice(j * MXU_DIM, (j + 1) * MXU_DIM)
            qs = _stack_heads(qb_ref[0, qrows, cols], hm_ref)
            s = lax.dot_general(qs, kb_ref[krows, cols], nt, preferred_element_type=f32)
            p, l = _softmax_rows(
                s,
                lambda m: jnp.concatenate([swm_ref[var, :, m * LANES:(m + 1) * LANES]] * SW_GROUP, axis=0),
                sink_ref[j], shift)
            rr = jnp.dot(p, vb_ref[krows, cols], preferred_element_type=f32)
            o = _take_diag(rr, l, SW_BLOCK)
            yb_ref[0, qrows, cols] = (o * szb_ref[0, qrows, cols].astype(f32)).astype(bf16)

    @pl.when(bounded_ref[0] != 0)
    def _():
        for rl in range(ATT_TILE // GRID_W):
            na_row(rl, False)
        for nl in range(ATT_TILE // SW_BLOCK):
            sw_block(nl, False)

    @pl.when(bounded_ref[0] == 0)
    def _():
        def na_body(rl, carry):
            na_row(rl, True)
            return carry

        def sw_body(nl, carry):
            sw_block(nl, True)
            return carry

        lax.fori_loop(0, ATT_TILE // GRID_W, na_body, 0)
        lax.fori_loop(0, ATT_TILE // SW_BLOCK, sw_body, 0)


def _window_start(i, halo, ext, seq_len):
    return jnp.clip(i * ATT_TILE - halo, 0, seq_len - ext)


def _attention(bounded, qa, ka, va, sza, qb, kb, vb, szb, pb, swm, sink_col, hm):
    B, S, _ = qa.shape
    tq = ATT_TILE
    grid = (B, S // tq)

    tok = lambda w: pl.BlockSpec((1, tq, w), lambda b, i: (b, i, 0))

    def window(width, halo, ext):
        return pl.BlockSpec(
            (pl.Element(ext), pl.Element(width)),
            lambda b, i: (pl.multiple_of(b * S + _window_start(i, halo, ext, S), SW_BLOCK), 0))

    flat = lambda t: t.reshape(B * S, t.shape[-1])
    kern = functools.partial(_attn_kernel, seq_len=S)
    return pl.pallas_call(
        kern,
        grid=grid,
        in_specs=[
            pl.BlockSpec(memory_space=pltpu.SMEM),
            tok(NA_WIDTH), tok(NA_WIDTH), tok(SW_WIDTH), tok(SW_WIDTH),
            window(NA_WIDTH, NA_HALO, NA_EXT), window(NA_WIDTH, NA_HALO, NA_EXT),
            window(SW_WIDTH, WINDOW, SW_EXT), window(SW_WIDTH, WINDOW, SW_EXT),
            _const_spec(pb.shape), _const_spec(swm.shape), _const_spec(sink_col.shape),
            _const_spec(hm.shape),
        ],
        out_specs=[tok(NA_WIDTH), tok(SW_WIDTH)],
        out_shape=[jax.ShapeDtypeStruct((B, S, NA_WIDTH), jnp.bfloat16),
                   jax.ShapeDtypeStruct((B, S, SW_WIDTH), jnp.bfloat16)],
        compiler_params=pltpu.CompilerParams(
            dimension_semantics=("arbitrary", "arbitrary"), vmem_limit_bytes=VMEM_LIMIT),
        name="attn",
    )(bounded, qa, sza, qb, szb, flat(ka), flat(va), flat(kb), flat(vb), pb, swm, sink_col, hm)


def _merge_kernel(ya_ref, yb_ref, x_ref, g_ref, wg_ref, woa_ref, wob_ref, wo_ref, y_ref):
    f32, bf16 = jnp.float32, jnp.bfloat16
    x = x_ref[0]
    ms = jnp.mean(x * x, axis=-1, keepdims=True)
    h = (x * lax.rsqrt(ms + NORM_EPS) * g_ref[...]).astype(bf16)
    ga = jax.nn.sigmoid(jnp.dot(h, wg_ref[:, :D_MODEL], preferred_element_type=f32))
    gb = jax.nn.sigmoid(jnp.dot(h, wg_ref[:, D_MODEL:], preferred_element_type=f32))
    pa = jnp.dot(ya_ref[0], woa_ref[...], preferred_element_type=f32)
    pbm = jnp.dot(yb_ref[0], wob_ref[...], preferred_element_type=f32)
    merged = (ga * pa + gb * pbm).astype(bf16)
    y_ref[0] = x + jnp.dot(merged, wo_ref[...], preferred_element_type=f32)


def _merge(x, ya, yb, g, wg, woa, wob, wo):
    B, S, _ = x.shape
    tm = MERGE_TILE
    tok = lambda w: pl.BlockSpec((1, tm, w), lambda b, i: (b, i, 0))
    return pl.pallas_call(
        _merge_kernel,
        grid=(B, S // tm),
        in_specs=[tok(NA_WIDTH), tok(SW_WIDTH), tok(D_MODEL), _const_spec(g.shape),
                  _const_spec(wg.shape, single=True), _const_spec(woa.shape, single=True),
                  _const_spec(wob.shape, single=True), _const_spec(wo.shape, single=True)],
        out_specs=tok(D_MODEL),
        out_shape=jax.ShapeDtypeStruct((B, S, D_MODEL), jnp.float32),
        compiler_params=pltpu.CompilerParams(
            dimension_semantics=("arbitrary", "arbitrary"), vmem_limit_bytes=VMEM_LIMIT),
        name="merge",
    )(ya, yb, x, g, wg, woa, wob, wo)


def _na_bias_table(rpb):
    c = jnp.arange(GRID_W)
    cs = jnp.clip(c - NA_KW // 2, 0, GRID_W - NA_KW)
    col_in = (c[None, :] >= cs[:, None]) & (c[None, :] < cs[:, None] + NA_KW)
    pad = GRID_W - NA_KW
    rp = jnp.pad(rpb.astype(jnp.float32) * LOG2_E, ((0, 0), (0, 0), (pad, pad)))
    b = jnp.stack([rp[:, :, GRID_W - 1 - q:2 * GRID_W - 1 - q] for q in range(GRID_W)], axis=2)
    b = jnp.where(col_in[None, None], b, NEG_BIG)
    nd = 2 * NA_KH - 2
    pair = jnp.concatenate([b[:, :nd], b[:, 1:nd + 1]], axis=-1)
    pair = pair.reshape(NA_HEADS // HEADS_PER_MXU, HEADS_PER_MXU, nd, GRID_W, LANES)
    return pair.transpose(2, 0, 1, 3, 4).reshape(nd, NA_HEADS // HEADS_PER_MXU,
                                                 HEADS_PER_MXU * GRID_W, LANES)


def _sw_mask_table():
    qi = jnp.arange(SW_BLOCK)[None, :, None]
    kj = jnp.arange(SW_KWIN)[None, None, :]
    d = jnp.arange(SW_SHIFTS)[:, None, None]
    band = jnp.abs(kj - d * SW_BLOCK - qi) <= WINDOW
    return jnp.where(band, 0.0, NEG_BIG).astype(jnp.float32)


def _layer(x, p):
    S = x.shape[1]
    outs = _projection(x, p["g"], p["w_in"], p["bd"], p["qna"], p["kna"], p["qnb"], p["knb"],
                       p["cos"][:S], p["sin"][:S])
    qa, ka, va, sza, qb, kb, vb, szb = outs
    ya, yb = _attention(p["bounded"], qa, ka, va, sza, qb, kb, vb, szb, p["pb"], p["swm"], p["sink"],
                        p["hm"])
    return _merge(x, ya, yb, p["g"], p["w_gate"], p["woa"], p["wob"], p["wo"])


def _prepare(S, norm_g, w_in, qn_a, kn_a, rpb_a, qn_b, kn_b, sink_b, w_out_a, w_out_b, w_o):
    f32, bf16 = jnp.float32, jnp.bfloat16
    seg = jnp.arange(MXU_DIM) // HEAD_DIM
    bd = jnp.where(seg[:, None] == seg[None, :], 1.0 / HEAD_DIM, 0.0).astype(bf16)
    half = HEAD_DIM // 2
    inv = ROPE_THETA ** (-jnp.arange(half, dtype=f32) / half)
    ang = jnp.arange(S, dtype=f32)[:, None] * inv[None, :]
    cos, sin = jnp.cos(ang), jnp.sin(ang)
    cos = jnp.tile(jnp.concatenate([cos, cos], axis=1), (1, LANES // HEAD_DIM))
    sin = jnp.tile(jnp.concatenate([-sin, sin], axis=1), (1, LANES // HEAD_DIM))
    lane_head = jnp.arange(MXU_DIM) // HEAD_DIM
    hm = (lane_head[None, None, :] == jnp.arange(HEADS_PER_MXU)[:, None, None])
    hm = jnp.broadcast_to(hm, (HEADS_PER_MXU, max(SW_BLOCK, GRID_W), MXU_DIM)).astype(bf16)
    sink_col = jnp.repeat((sink_b.astype(f32) * LOG2_E).reshape(SW_KV_HEADS, SW_GROUP), SW_BLOCK, axis=1)
    gain = lambda t: jnp.max(jnp.abs(t.astype(f32)))
    qk_bound = HEAD_DIM ** 0.5 * LOG2_E * (1.0 + 2.0 ** -6)
    bound_a = qk_bound * gain(qn_a) * gain(kn_a) + LOG2_E * gain(rpb_a)
    bound_b = jnp.maximum(qk_bound * gain(qn_b) * gain(kn_b), LOG2_E * gain(sink_b))
    bounded = (jnp.maximum(bound_a, bound_b) <= MAX_UNSHIFTED_LOGIT).astype(jnp.int32).reshape(1)
    w_bf = w_in.astype(bf16)
    return {
        "bounded": bounded,
        "g": norm_g.astype(f32).reshape(1, D_MODEL),
        "w_in": w_bf[:, :C_GA],
        "w_gate": w_bf[:, C_GA:],
        "bd": bd,
        "qna": jnp.tile(qn_a.astype(f32), NA_HEADS).reshape(1, NA_WIDTH),
        "kna": jnp.tile(kn_a.astype(f32), NA_HEADS).reshape(1, NA_WIDTH),
        "qnb": jnp.tile(qn_b.astype(f32), SW_HEADS).reshape(1, SW_WIDTH),
        "knb": jnp.tile(kn_b.astype(f32), SW_KV_HEADS).reshape(1, SW_KV_WIDTH),
        "cos": cos, "sin": sin,
        "pb": _na_bias_table(rpb_a),
        "swm": _sw_mask_table(),
        "sink": sink_col.reshape(SW_KV_HEADS, SW_GROUP * SW_BLOCK, 1),
        "hm": hm,
        "woa": w_out_a.astype(bf16), "wob": w_out_b.astype(bf16), "wo": w_o.astype(bf16),
    }


def kernel(x_prompt, x_sample, norm_g, w_in, qn_a, kn_a, rpb_a, qn_b, kn_b, sink_b, w_out_a, w_out_b, w_o):
    depth = norm_g.shape[0]
    y_prompt, y_sample = x_prompt, x_sample
    S = max(x_prompt.shape[1], x_sample.shape[1])
    for l in range(depth):
        p = _prepare(S, norm_g[l], w_in[l], qn_a[l], kn_a[l], rpb_a[l], qn_b[l], kn_b[l], sink_b[l],
                     w_out_a[l], w_out_b[l], w_o[l])
        y_prompt = _layer(y_prompt, p)
        y_sample = _layer(y_sample, p)
    return (y_prompt, y_sample)
```

```python
import functools

import jax
import jax.numpy as jnp
from jax import lax
from jax.experimental import pallas as pl
from jax.experimental.pallas import tpu as pltpu

D_MODEL = 1024
HEAD_DIM = 64
GRID_W = 64
NA_HEADS = 8
NA_KH = 8
NA_KW = 16
NA_WIDTH = NA_HEADS * HEAD_DIM
SW_HEADS = 8
SW_KV_HEADS = 2
SW_GROUP = SW_HEADS // SW_KV_HEADS
SW_WIDTH = SW_HEADS * HEAD_DIM
SW_KV_WIDTH = SW_KV_HEADS * HEAD_DIM
WINDOW = 128
ROPE_THETA = 10000.0
NORM_EPS = 1e-6

C_QA = 0
C_KA = C_QA + NA_WIDTH
C_VA = C_KA + NA_WIDTH
C_ZA = C_VA + NA_WIDTH
C_QB = C_ZA + NA_WIDTH
C_KB = C_QB + SW_WIDTH
C_VB = C_KB + SW_KV_WIDTH
C_ZB = C_VB + SW_KV_WIDTH
C_GA = C_ZB + SW_WIDTH
C_GB = C_GA + D_MODEL
IN_WIDTH = C_GB + D_MODEL

LANES = 128
MXU_DIM = 256
HEADS_PER_MXU = MXU_DIM // HEAD_DIM
NEG_BIG = -1e30
LOG2_E = 1.4426950408889634
MAX_UNSHIFTED_LOGIT = 64.0
VMEM_LIMIT = 56 * 1024 * 1024

PROJ_TILE = 1024
ATT_TILE = 1024
ATT_ROWS = ATT_TILE // GRID_W
MERGE_TILE = 1024
NA_HALO = 256
SW_BLOCK = 64
SW_KWIN = 384
SW_PREV = WINDOW
SW_NEXT = 256
SW_EDGE = WINDOW // SW_BLOCK


def _segment_mean_sq(t, bd):
    w = t.shape[1]
    sq = (t * t).astype(jnp.bfloat16)
    outs = []
    for c in range(0, w, MXU_DIM):
        cw = min(MXU_DIM, w - c)
        outs.append(jnp.dot(sq[:, c:c + cw], bd[:cw, :cw], preferred_element_type=jnp.float32))
    return outs[0] if len(outs) == 1 else jnp.concatenate(outs, axis=1)


def _swap_halves(t):
    w = t.shape[1]
    lane = lax.broadcasted_iota(jnp.int32, (t.shape[0], LANES), 1)
    low = (lane % HEAD_DIM) < (HEAD_DIM // 2)
    outs = []
    for c in range(0, w, LANES):
        blk = t[:, c:c + LANES]
        fwd = pltpu.roll(blk, HEAD_DIM // 2, 1)
        bwd = pltpu.roll(blk, LANES - HEAD_DIM // 2, 1)
        outs.append(jnp.where(low, bwd, fwd))
    return outs[0] if len(outs) == 1 else jnp.concatenate(outs, axis=1)


def _tile_lanes(t, reps):
    return t if reps == 1 else jnp.concatenate([t] * reps, axis=1)


def _proj_kernel(x_ref, g_ref, w_ref, bd_ref, qna_ref, kna_ref, qnb_ref, knb_ref, cos_ref, sin_ref,
                 qa_o, ka_o, va_o, sza_o, qb_o, kb_o, vb_o, szb_o):
    f32, bf16 = jnp.float32, jnp.bfloat16
    x = x_ref[0]
    ms = jnp.mean(x * x, axis=-1, keepdims=True)
    h = (x * lax.rsqrt(ms + NORM_EPS) * g_ref[...]).astype(bf16)
    bd = bd_ref[...]
    scale = HEAD_DIM ** -0.5 * LOG2_E

    def proj(c0, c1):
        return jnp.dot(h, w_ref[:, c0:c1], preferred_element_type=f32)

    def head_norm(t, gain):
        return t * lax.rsqrt(_segment_mean_sq(t, bd) + NORM_EPS) * gain

    def rotary(t):
        reps = t.shape[1] // LANES
        cos = _tile_lanes(cos_ref[...], reps)
        sin = _tile_lanes(sin_ref[...], reps)
        return t * cos + _swap_halves(t) * sin

    qa_o[0] = (head_norm(proj(C_QA, C_KA), qna_ref[...]) * scale).astype(bf16)
    ka_o[0] = head_norm(proj(C_KA, C_VA), kna_ref[...]).astype(bf16)
    va_o[0] = proj(C_VA, C_ZA).astype(bf16)
    sza_o[0] = jax.nn.silu(proj(C_ZA, C_QB)).astype(bf16)
    qb_o[0] = (rotary(head_norm(proj(C_QB, C_KB), qnb_ref[...])) * scale).astype(bf16)

    kvb = proj(C_KB, C_ZB)
    kb = rotary(head_norm(kvb[:, :SW_KV_WIDTH], knb_ref[...]))
    vb = kvb[:, SW_KV_WIDTH:]
    lane = lax.broadcasted_iota(jnp.int32, kb.shape, 1)
    first = lane < HEAD_DIM

    def replicate(t):
        r = pltpu.roll(t, HEAD_DIM, 1)
        h0 = jnp.where(first, t, r)
        h1 = jnp.where(first, r, t)
        return jnp.concatenate([h0, h0, h1, h1], axis=1)

    kb_o[0] = replicate(kb).astype(bf16)
    vb_o[0] = replicate(vb).astype(bf16)
    szb_o[0] = jax.nn.silu(proj(C_ZB, C_GA)).astype(bf16)


def _const_spec(shape, single=False):
    nd = len(shape)
    mode = {"pipeline_mode": pl.Buffered(1)} if single else {}
    return pl.BlockSpec(shape, lambda b, i: (0,) * nd, **mode)


def _projection(x, g, w_bf, bd, qna, kna, qnb, knb, cos, sin):
    B, S, _ = x.shape
    tm = PROJ_TILE
    grid = (B, S // tm)
    tok = lambda w: pl.BlockSpec((1, tm, w), lambda b, i: (b, i, 0))
    out_widths = [NA_WIDTH, NA_WIDTH, NA_WIDTH, NA_WIDTH, SW_WIDTH, SW_WIDTH, SW_WIDTH, SW_WIDTH]
    return pl.pallas_call(
        _proj_kernel,
        grid=grid,
        in_specs=[
            tok(D_MODEL),
            _const_spec((1, D_MODEL)),
            _const_spec((D_MODEL, C_GA), single=True),
            _const_spec((MXU_DIM, MXU_DIM)),
            _const_spec((1, NA_WIDTH)),
            _const_spec((1, NA_WIDTH)),
            _const_spec((1, SW_WIDTH)),
            _const_spec((1, SW_KV_WIDTH)),
            pl.BlockSpec((tm, LANES), lambda b, i: (i, 0)),
            pl.BlockSpec((tm, LANES), lambda b, i: (i, 0)),
        ],
        out_specs=[tok(w) for w in out_widths],
        out_shape=[jax.ShapeDtypeStruct((B, S, w), jnp.bfloat16) for w in out_widths],
        compiler_params=pltpu.CompilerParams(
            dimension_semantics=("arbitrary", "arbitrary"), vmem_limit_bytes=VMEM_LIMIT),
        name="proj",
    )(x, g, w_bf, bd, qna, kna, qnb, knb, cos, sin)


def _stack_heads(q, hm_ref):
    rows = q.shape[0]
    return jnp.concatenate([q * hm_ref[hh, :rows, :] for hh in range(HEADS_PER_MXU)], axis=0)


def _take_diag(r, l, rows):
    lane = lax.broadcasted_iota(jnp.int32, (rows, MXU_DIM), 1) // HEAD_DIM
    last = HEADS_PER_MXU - 1
    num = r[last * rows:]
    den = jnp.broadcast_to(l[last * rows:], (rows, MXU_DIM))
    for hh in range(last - 1, -1, -1):
        num = jnp.where(lane == hh, r[hh * rows:(hh + 1) * rows], num)
        den = jnp.where(lane == hh, l[hh * rows:(hh + 1) * rows], den)
    return num / den


def _softmax_rows(s, add, floor, shift):
    n = s.shape[1] // LANES
    sb = [s[:, m * LANES:(m + 1) * LANES] + add(m) for m in range(n)]
    if shift:
        mx = jnp.max(functools.reduce(jnp.maximum, sb), axis=-1, keepdims=True)
        if floor is not None:
            mx = jnp.maximum(mx, floor)
            floor = floor - mx
        sb = [t - mx for t in sb]
    p = [jnp.exp2(t) for t in sb]
    l = jnp.sum(functools.reduce(jnp.add, p), axis=-1, keepdims=True)
    if floor is not None:
        l = l + jnp.exp2(floor)
    return jnp.concatenate(p, axis=1).astype(jnp.bfloat16), l


def _attn_kernel(bounded_ref, qa_ref, sza_ref, qb_ref, szb_ref,
                 kap_ref, kac_ref, kan_ref, vap_ref, vac_ref, van_ref,
                 kbp_ref, kbc_ref, kbn_ref, vbp_ref, vbc_ref, vbn_ref,
                 pb_ref, swm_ref, sink_ref, hm_ref,
                 ya_ref, yb_ref, kaext, vaext, kbext, vbext, *, rows_total, blocks_total):
    f32, bf16 = jnp.float32, jnp.bfloat16
    i = pl.program_id(1)
    nt = (((1,), (1,)), ((), ()))

    kaext[0:NA_HALO] = kap_ref[0]
    kaext[NA_HALO:NA_HALO + ATT_TILE] = kac_ref[0]
    kaext[NA_HALO + ATT_TILE:] = kan_ref[0]
    vaext[0:NA_HALO] = vap_ref[0]
    vaext[NA_HALO:NA_HALO + ATT_TILE] = vac_ref[0]
    vaext[NA_HALO + ATT_TILE:] = van_ref[0]
    kbext[0:SW_PREV] = kbp_ref[0]
    kbext[SW_PREV:SW_PREV + ATT_TILE] = kbc_ref[0]
    kbext[SW_PREV + ATT_TILE:] = kbn_ref[0]
    vbext[0:SW_PREV] = vbp_ref[0]
    vbext[SW_PREV:SW_PREV + ATT_TILE] = vbc_ref[0]
    vbext[SW_PREV + ATT_TILE:] = vbn_ref[0]

    kh = min(NA_KH, rows_total)
    ext_row0 = i * ATT_ROWS - NA_HALO // GRID_W
    win = kh * GRID_W

    def na_row(rl, shift):
        r = i * ATT_ROWS + rl
        krow0 = jnp.clip(r - kh // 2, 0, rows_total - kh)
        off = pl.multiple_of((krow0 - ext_row0) * GRID_W, GRID_W)
        d0 = krow0 - r + (NA_KH - 1)
        qrows = pl.ds(pl.multiple_of(rl * GRID_W, GRID_W), GRID_W)
        for g in range(NA_HEADS // HEADS_PER_MXU):
            cols = slice(g * MXU_DIM, (g + 1) * MXU_DIM)
            qs = _stack_heads(qa_ref[0, qrows, cols], hm_ref)
            s = lax.dot_general(qs, kaext[pl.ds(off, win), cols], nt,
                                preferred_element_type=f32)
            p, l = _softmax_rows(s, lambda m: pb_ref[d0 + 2 * m, g], None, shift)
            rr = jnp.dot(p, vaext[pl.ds(off, win), cols], preferred_element_type=f32)
            o = _take_diag(rr, l, GRID_W)
            ya_ref[0, qrows, cols] = (o * sza_ref[0, qrows, cols].astype(f32)).astype(bf16)

    def sw_block(nl, shift):
        gblk = i * (ATT_TILE // SW_BLOCK) + nl
        var = jnp.where(gblk < SW_EDGE, gblk,
                        jnp.where(gblk >= blocks_total - SW_EDGE,
                                  gblk - (blocks_total - 2 * SW_EDGE - 1), SW_EDGE))
        qrows = pl.ds(pl.multiple_of(nl * SW_BLOCK, SW_BLOCK), SW_BLOCK)
        krows = pl.ds(pl.multiple_of(nl * SW_BLOCK, SW_BLOCK), SW_KWIN)
        for j in range(SW_KV_HEADS):
            cols = slice(j * MXU_DIM, (j + 1) * MXU_DIM)
            qs = _stack_heads(qb_ref[0, qrows, cols], hm_ref)
            s = lax.dot_general(qs, kbext[krows, cols], nt, preferred_element_type=f32)
            p, l = _softmax_rows(
                s,
                lambda m: jnp.concatenate([swm_ref[var, :, m * LANES:(m + 1) * LANES]] * SW_GROUP, axis=0),
                sink_ref[j], shift)
            rr = jnp.dot(p, vbext[krows, cols], preferred_element_type=f32)
            o = _take_diag(rr, l, SW_BLOCK)
            yb_ref[0, qrows, cols] = (o * szb_ref[0, qrows, cols].astype(f32)).astype(bf16)

    @pl.when(bounded_ref[0] != 0)
    def _():
        for rl in range(ATT_TILE // GRID_W):
            na_row(rl, False)
        for nl in range(ATT_TILE // SW_BLOCK):
            sw_block(nl, False)

    @pl.when(bounded_ref[0] == 0)
    def _():
        def na_body(rl, carry):
            na_row(rl, True)
            return carry

        def sw_body(nl, carry):
            sw_block(nl, True)
            return carry

        lax.fori_loop(0, ATT_TILE // GRID_W, na_body, 0)
        lax.fori_loop(0, ATT_TILE // SW_BLOCK, sw_body, 0)


def _attention(bounded, qa, ka, va, sza, qb, kb, vb, szb, pb, swm, sink_col, hm):
    B, S, _ = qa.shape
    tq = ATT_TILE
    grid = (B, S // tq)

    tok = lambda w: pl.BlockSpec((1, tq, w), lambda b, i: (b, i, 0))

    def halo(width, prev_blk, next_blk):
        prev = pl.BlockSpec((1, prev_blk, width),
                            lambda b, i: (b, jnp.maximum(i * (tq // prev_blk) - 1, 0), 0))
        nxt = pl.BlockSpec((1, next_blk, width),
                           lambda b, i: (b, jnp.minimum((i + 1) * (tq // next_blk), S // next_blk - 1), 0))
        return prev, tok(width), nxt

    ka_specs = halo(NA_WIDTH, NA_HALO, NA_HALO)
    kb_specs = halo(SW_WIDTH, SW_PREV, SW_NEXT)

    kern = functools.partial(_attn_kernel, rows_total=S // GRID_W, blocks_total=S // SW_BLOCK)
    return pl.pallas_call(
        kern,
        grid=grid,
        in_specs=[
            pl.BlockSpec(memory_space=pltpu.SMEM),
            tok(NA_WIDTH), tok(NA_WIDTH), tok(SW_WIDTH), tok(SW_WIDTH),
            *ka_specs, *ka_specs, *kb_specs, *kb_specs,
            _const_spec(pb.shape), _const_spec(swm.shape), _const_spec(sink_col.shape),
            _const_spec(hm.shape),
        ],
        out_specs=[tok(NA_WIDTH), tok(SW_WIDTH)],
        out_shape=[jax.ShapeDtypeStruct((B, S, NA_WIDTH), jnp.bfloat16),
                   jax.ShapeDtypeStruct((B, S, SW_WIDTH), jnp.bfloat16)],
        scratch_shapes=[
            pltpu.VMEM((tq + 2 * NA_HALO, NA_WIDTH), jnp.bfloat16),
            pltpu.VMEM((tq + 2 * NA_HALO, NA_WIDTH), jnp.bfloat16),
            pltpu.VMEM((SW_PREV + tq + SW_NEXT, SW_WIDTH), jnp.bfloat16),
            pltpu.VMEM((SW_PREV + tq + SW_NEXT, SW_WIDTH), jnp.bfloat16),
        ],
        compiler_params=pltpu.CompilerParams(
            dimension_semantics=("arbitrary", "arbitrary"), vmem_limit_bytes=VMEM_LIMIT),
        name="attn",
    )(bounded, qa, sza, qb, szb, ka, ka, ka, va, va, va, kb, kb, kb, vb, vb, vb, pb, swm, sink_col, hm)


def _merge_kernel(ya_ref, yb_ref, x_ref, g_ref, wg_ref, woa_ref, wob_ref, wo_ref, y_ref):
    f32, bf16 = jnp.float32, jnp.bfloat16
    x = x_ref[0]
    ms = jnp.mean(x * x, axis=-1, keepdims=True)
    h = (x * lax.rsqrt(ms + NORM_EPS) * g_ref[...]).astype(bf16)
    ga = jax.nn.sigmoid(jnp.dot(h, wg_ref[:, :D_MODEL], preferred_element_type=f32))
    gb = jax.nn.sigmoid(jnp.dot(h, wg_ref[:, D_MODEL:], preferred_element_type=f32))
    pa = jnp.dot(ya_ref[0], woa_ref[...], preferred_element_type=f32)
    pbm = jnp.dot(yb_ref[0], wob_ref[...], preferred_element_type=f32)
    merged = (ga * pa + gb * pbm).astype(bf16)
    y_ref[0] = x + jnp.dot(merged, wo_ref[...], preferred_element_type=f32)


def _merge(x, ya, yb, g, wg, woa, wob, wo):
    B, S, _ = x.shape
    tm = MERGE_TILE
    tok = lambda w: pl.BlockSpec((1, tm, w), lambda b, i: (b, i, 0))
    return pl.pallas_call(
        _merge_kernel,
        grid=(B, S // tm),
        in_specs=[tok(NA_WIDTH), tok(SW_WIDTH), tok(D_MODEL), _const_spec(g.shape),
                  _const_spec(wg.shape, single=True), _const_spec(woa.shape, single=True),
                  _const_spec(wob.shape, single=True), _const_spec(wo.shape, single=True)],
        out_specs=tok(D_MODEL),
        out_shape=jax.ShapeDtypeStruct((B, S, D_MODEL), jnp.float32),
        compiler_params=pltpu.CompilerParams(
            dimension_semantics=("arbitrary", "arbitrary"), vmem_limit_bytes=VMEM_LIMIT),
        name="merge",
    )(ya, yb, x, g, wg, woa, wob, wo)


def _na_bias_table(rpb):
    c = jnp.arange(GRID_W)
    cs = jnp.clip(c - NA_KW // 2, 0, GRID_W - NA_KW)
    col_in = (c[None, :] >= cs[:, None]) & (c[None, :] < cs[:, None] + NA_KW)
    pad = GRID_W - NA_KW
    rp = jnp.pad(rpb.astype(jnp.float32) * LOG2_E, ((0, 0), (0, 0), (pad, pad)))
    b = jnp.stack([rp[:, :, GRID_W - 1 - q:2 * GRID_W - 1 - q] for q in range(GRID_W)], axis=2)
    b = jnp.where(col_in[None, None], b, NEG_BIG)
    nd = 2 * NA_KH - 2
    pair = jnp.concatenate([b[:, :nd], b[:, 1:nd + 1]], axis=-1)
    pair = pair.reshape(NA_HEADS // HEADS_PER_MXU, HEADS_PER_MXU, nd, GRID_W, LANES)
    return pair.transpose(2, 0, 1, 3, 4).reshape(nd, NA_HEADS // HEADS_PER_MXU,
                                                 HEADS_PER_MXU * GRID_W, LANES)


def _sw_mask_table():
    qi = jnp.arange(SW_BLOCK)[:, None]
    kj = jnp.arange(SW_KWIN)[None, :]
    band = jnp.abs(kj - WINDOW - qi) <= WINDOW
    first = [band & (kj >= WINDOW - e * SW_BLOCK) for e in range(SW_EDGE)]
    last = [band & (kj < (SW_EDGE - e) * SW_BLOCK + WINDOW) for e in range(SW_EDGE)]
    m = jnp.stack(first + [band] + last)
    return jnp.where(m, 0.0, NEG_BIG).astype(jnp.float32)


def _layer(x, p):
    S = x.shape[1]
    outs = _projection(x, p["g"], p["w_in"], p["bd"], p["qna"], p["kna"], p["qnb"], p["knb"],
                       p["cos"][:S], p["sin"][:S])
    qa, ka, va, sza, qb, kb, vb, szb = outs
    ya, yb = _attention(p["bounded"], qa, ka, va, sza, qb, kb, vb, szb, p["pb"], p["swm"], p["sink"],
                        p["hm"])
    return _merge(x, ya, yb, p["g"], p["w_gate"], p["woa"], p["wob"], p["wo"])


def _prepare(S, norm_g, w_in, qn_a, kn_a, rpb_a, qn_b, kn_b, sink_b, w_out_a, w_out_b, w_o):
    f32, bf16 = jnp.float32, jnp.bfloat16
    seg = jnp.arange(MXU_DIM) // HEAD_DIM
    bd = jnp.where(seg[:, None] == seg[None, :], 1.0 / HEAD_DIM, 0.0).astype(bf16)
    half = HEAD_DIM // 2
    inv = ROPE_THETA ** (-jnp.arange(half, dtype=f32) / half)
    ang = jnp.arange(S, dtype=f32)[:, None] * inv[None, :]
    cos, sin = jnp.cos(ang), jnp.sin(ang)
    cos = jnp.tile(jnp.concatenate([cos, cos], axis=1), (1, LANES // HEAD_DIM))
    sin = jnp.tile(jnp.concatenate([-sin, sin], axis=1), (1, LANES // HEAD_DIM))
    lane_head = jnp.arange(MXU_DIM) // HEAD_DIM
    hm = (lane_head[None, None, :] == jnp.arange(HEADS_PER_MXU)[:, None, None])
    hm = jnp.broadcast_to(hm, (HEADS_PER_MXU, max(SW_BLOCK, GRID_W), MXU_DIM)).astype(bf16)
    sink_col = jnp.repeat((sink_b.astype(f32) * LOG2_E).reshape(SW_KV_HEADS, SW_GROUP), SW_BLOCK, axis=1)
    gain = lambda t: jnp.max(jnp.abs(t.astype(f32)))
    qk_bound = HEAD_DIM ** 0.5 * LOG2_E * (1.0 + 2.0 ** -6)
    bound_a = qk_bound * gain(qn_a) * gain(kn_a) + LOG2_E * gain(rpb_a)
    bound_b = jnp.maximum(qk_bound * gain(qn_b) * gain(kn_b), LOG2_E * gain(sink_b))
    bounded = (jnp.maximum(bound_a, bound_b) <= MAX_UNSHIFTED_LOGIT).astype(jnp.int32).reshape(1)
    w_bf = w_in.astype(bf16)
    return {
        "bounded": bounded,
        "g": norm_g.astype(f32).reshape(1, D_MODEL),
        "w_in": w_bf[:, :C_GA],
        "w_gate": w_bf[:, C_GA:],
        "bd": bd,
        "qna": jnp.tile(qn_a.astype(f32), NA_HEADS).reshape(1, NA_WIDTH),
        "kna": jnp.tile(kn_a.astype(f32), NA_HEADS).reshape(1, NA_WIDTH),
        "qnb": jnp.tile(qn_b.astype(f32), SW_HEADS).reshape(1, SW_WIDTH),
        "knb": jnp.tile(kn_b.astype(f32), SW_KV_HEADS).reshape(1, SW_KV_WIDTH),
        "cos": cos, "sin": sin,
        "pb": _na_bias_table(rpb_a),
        "swm": _sw_mask_table(),
        "sink": sink_col.reshape(SW_KV_HEADS, SW_GROUP * SW_BLOCK, 1),
        "hm": hm,
        "woa": w_out_a.astype(bf16), "wob": w_out_b.astype(bf16), "wo": w_o.astype(bf16),
    }


def kernel(x_prompt, x_sample, norm_g, w_in, qn_a, kn_a, rpb_a, qn_b, kn_b, sink_b, w_out_a, w_out_b, w_o):
    depth = norm_g.shape[0]
    y_prompt, y_sample = x_prompt, x_sample
    S = max(x_prompt.shape[1], x_sample.shape[1])
    for l in range(depth):
        p = _prepare(S, norm_g[l], w_in[l], qn_a[l], kn_a[l], rpb_a[l], qn_b[l], kn_b[l], sink_b[l],
                     w_out_a[l], w_out_b[l], w_o[l])
        y_prompt = _layer(y_prompt, p)
        y_sample = _layer(y_sample, p)
    return (y_prompt, y_sample)
```

```python
import functools

import jax
import jax.numpy as jnp
from jax import lax
from jax.experimental import pallas as pl
from jax.experimental.pallas import tpu as pltpu

D_MODEL = 1024
HEAD_DIM = 64
GRID_W = 64
NA_HEADS = 8
NA_KH = 8
NA_KW = 16
NA_WIDTH = NA_HEADS * HEAD_DIM
SW_HEADS = 8
SW_KV_HEADS = 2
SW_GROUP = SW_HEADS // SW_KV_HEADS
SW_WIDTH = SW_HEADS * HEAD_DIM
SW_KV_WIDTH = SW_KV_HEADS * HEAD_DIM
WINDOW = 128
ROPE_THETA = 10000.0
NORM_EPS = 1e-6

C_QA = 0
C_KA = C_QA + NA_WIDTH
C_VA = C_KA + NA_WIDTH
C_ZA = C_VA + NA_WIDTH
C_QB = C_ZA + NA_WIDTH
C_KB = C_QB + SW_WIDTH
C_VB = C_KB + SW_KV_WIDTH
C_ZB = C_VB + SW_KV_WIDTH
C_GA = C_ZB + SW_WIDTH
C_GB = C_GA + D_MODEL
IN_WIDTH = C_GB + D_MODEL

LANES = 128
MXU_DIM = 256
HEADS_PER_MXU = MXU_DIM // HEAD_DIM
NEG_BIG = -1e30
LOG2_E = 1.4426950408889634
MAX_UNSHIFTED_LOGIT = 64.0
VMEM_LIMIT = 56 * 1024 * 1024

PROJ_TILE = 512
ATT_TILE = 1024
ATT_ROWS = ATT_TILE // GRID_W
MERGE_TILE = 1024
NA_HALO = 256
NA_EXT = ATT_TILE + 2 * NA_HALO
SW_BLOCK = 64
SW_KWIN = 384
SW_EXT = ATT_TILE + SW_KWIN - SW_BLOCK
SW_SHIFTS = (SW_KWIN - SW_BLOCK) // SW_BLOCK + 1


def _segment_mean_sq(t, bd):
    w = t.shape[1]
    sq = (t * t).astype(jnp.bfloat16)
    outs = []
    for c in range(0, w, MXU_DIM):
        cw = min(MXU_DIM, w - c)
        outs.append(jnp.dot(sq[:, c:c + cw], bd[:cw, :cw], preferred_element_type=jnp.float32))
    return outs[0] if len(outs) == 1 else jnp.concatenate(outs, axis=1)


def _swap_halves(t):
    w = t.shape[1]
    lane = lax.broadcasted_iota(jnp.int32, (t.shape[0], LANES), 1)
    low = (lane % HEAD_DIM) < (HEAD_DIM // 2)
    outs = []
    for c in range(0, w, LANES):
        blk = t[:, c:c + LANES]
        fwd = pltpu.roll(blk, HEAD_DIM // 2, 1)
        bwd = pltpu.roll(blk, LANES - HEAD_DIM // 2, 1)
        outs.append(jnp.where(low, bwd, fwd))
    return outs[0] if len(outs) == 1 else jnp.concatenate(outs, axis=1)


def _tile_lanes(t, reps):
    return t if reps == 1 else jnp.concatenate([t] * reps, axis=1)


def _proj_kernel(x_ref, g_ref, w_ref, bd_ref, qna_ref, kna_ref, qnb_ref, knb_ref, cos_ref, sin_ref,
                 qa_o, ka_o, va_o, sza_o, qb_o, kb_o, vb_o, szb_o, sga_o, sgb_o):
    f32, bf16 = jnp.float32, jnp.bfloat16
    x = x_ref[0]
    ms = jnp.mean(x * x, axis=-1, keepdims=True)
    h = (x * lax.rsqrt(ms + NORM_EPS) * g_ref[...]).astype(bf16)
    bd = bd_ref[...]
    scale = HEAD_DIM ** -0.5 * LOG2_E

    def proj(c0, c1):
        return jnp.dot(h, w_ref[:, c0:c1], preferred_element_type=f32)

    def head_norm(t, gain):
        return t * lax.rsqrt(_segment_mean_sq(t, bd) + NORM_EPS) * gain

    def rotary(t):
        reps = t.shape[1] // LANES
        cos = _tile_lanes(cos_ref[...], reps)
        sin = _tile_lanes(sin_ref[...], reps)
        return t * cos + _swap_halves(t) * sin

    qa_o[0] = (head_norm(proj(C_QA, C_KA), qna_ref[...]) * scale).astype(bf16)
    ka_o[0] = head_norm(proj(C_KA, C_VA), kna_ref[...]).astype(bf16)
    va_o[0] = proj(C_VA, C_ZA).astype(bf16)
    sza_o[0] = jax.nn.silu(proj(C_ZA, C_QB)).astype(bf16)
    qb_o[0] = (rotary(head_norm(proj(C_QB, C_KB), qnb_ref[...])) * scale).astype(bf16)

    kvb = proj(C_KB, C_ZB)
    kb = rotary(head_norm(kvb[:, :SW_KV_WIDTH], knb_ref[...]))
    vb = kvb[:, SW_KV_WIDTH:]
    lane = lax.broadcasted_iota(jnp.int32, kb.shape, 1)
    first = lane < HEAD_DIM

    def replicate(t):
        r = pltpu.roll(t, HEAD_DIM, 1)
        h0 = jnp.where(first, t, r)
        h1 = jnp.where(first, r, t)
        return jnp.concatenate([h0, h0, h1, h1], axis=1)

    kb_o[0] = replicate(kb).astype(bf16)
    vb_o[0] = replicate(vb).astype(bf16)
    szb_o[0] = jax.nn.silu(proj(C_ZB, C_GA)).astype(bf16)
    sga_o[0] = jax.nn.sigmoid(proj(C_GA, C_GB)).astype(bf16)
    sgb_o[0] = jax.nn.sigmoid(proj(C_GB, IN_WIDTH)).astype(bf16)


def _const_spec(shape):
    nd = len(shape)
    return pl.BlockSpec(shape, lambda b, i: (0,) * nd)


def _projection(x, g, w_bf, bd, qna, kna, qnb, knb, cos, sin):
    B, S, _ = x.shape
    tm = PROJ_TILE
    grid = (B, S // tm)
    tok = lambda w: pl.BlockSpec((1, tm, w), lambda b, i: (b, i, 0))
    out_widths = [NA_WIDTH, NA_WIDTH, NA_WIDTH, NA_WIDTH, SW_WIDTH, SW_WIDTH, SW_WIDTH, SW_WIDTH,
                  D_MODEL, D_MODEL]
    return pl.pallas_call(
        _proj_kernel,
        grid=grid,
        in_specs=[
            tok(D_MODEL),
            _const_spec((1, D_MODEL)),
            _const_spec((D_MODEL, IN_WIDTH)),
            _const_spec((MXU_DIM, MXU_DIM)),
            _const_spec((1, NA_WIDTH)),
            _const_spec((1, NA_WIDTH)),
            _const_spec((1, SW_WIDTH)),
            _const_spec((1, SW_KV_WIDTH)),
            pl.BlockSpec((tm, LANES), lambda b, i: (i, 0)),
            pl.BlockSpec((tm, LANES), lambda b, i: (i, 0)),
        ],
        out_specs=[tok(w) for w in out_widths],
        out_shape=[jax.ShapeDtypeStruct((B, S, w), jnp.bfloat16) for w in out_widths],
        compiler_params=pltpu.CompilerParams(
            dimension_semantics=("arbitrary", "arbitrary"), vmem_limit_bytes=VMEM_LIMIT),
        name="proj",
    )(x, g, w_bf, bd, qna, kna, qnb, knb, cos, sin)


def _stack_heads(q, hm_ref):
    rows = q.shape[0]
    return jnp.concatenate([q * hm_ref[hh, :rows, :] for hh in range(HEADS_PER_MXU)], axis=0)


def _take_diag(r, l, rows):
    lane = lax.broadcasted_iota(jnp.int32, (rows, MXU_DIM), 1) // HEAD_DIM
    last = HEADS_PER_MXU - 1
    num = r[last * rows:]
    den = jnp.broadcast_to(l[last * rows:], (rows, MXU_DIM))
    for hh in range(last - 1, -1, -1):
        num = jnp.where(lane == hh, r[hh * rows:(hh + 1) * rows], num)
        den = jnp.where(lane == hh, l[hh * rows:(hh + 1) * rows], den)
    return num / den


def _softmax_rows(s, add, floor, shift):
    n = s.shape[1] // LANES
    sb = [s[:, m * LANES:(m + 1) * LANES] + add(m) for m in range(n)]
    if shift:
        mx = jnp.max(functools.reduce(jnp.maximum, sb), axis=-1, keepdims=True)
        if floor is not None:
            mx = jnp.maximum(mx, floor)
            floor = floor - mx
        sb = [t - mx for t in sb]
    p = [jnp.exp2(t) for t in sb]
    l = jnp.sum(functools.reduce(jnp.add, p), axis=-1, keepdims=True)
    if floor is not None:
        l = l + jnp.exp2(floor)
    return jnp.concatenate(p, axis=1).astype(jnp.bfloat16), l


def _attn_kernel(bounded_ref, qa_ref, sza_ref, qb_ref, szb_ref, ka_ref, va_ref, kb_ref, vb_ref,
                 pb_ref, swm_ref, sink_ref, hm_ref, ya_ref, yb_ref, *, seq_len):
    f32, bf16 = jnp.float32, jnp.bfloat16
    i = pl.program_id(1)
    nt = (((1,), (1,)), ((), ()))
    rows_total = seq_len // GRID_W
    kh = min(NA_KH, rows_total)
    win = kh * GRID_W
    na_start = _window_start(i, NA_HALO, NA_EXT, seq_len)
    sw_start = _window_start(i, WINDOW, SW_EXT, seq_len)

    def na_row(rl, shift):
        r = i * ATT_ROWS + rl
        krow0 = jnp.clip(r - kh // 2, 0, rows_total - kh)
        off = pl.multiple_of(krow0 * GRID_W - na_start, GRID_W)
        d0 = krow0 - r + (NA_KH - 1)
        qrows = pl.ds(pl.multiple_of(rl * GRID_W, GRID_W), GRID_W)
        for g in range(NA_HEADS // HEADS_PER_MXU):
            cols = slice(g * MXU_DIM, (g + 1) * MXU_DIM)
            qs = _stack_heads(qa_ref[0, qrows, cols], hm_ref)
            s = lax.dot_general(qs, ka_ref[pl.ds(off, win), cols], nt,
                                preferred_element_type=f32)
            p, l = _softmax_rows(s, lambda m: pb_ref[d0 + 2 * m, g], None, shift)
            rr = jnp.dot(p, va_ref[pl.ds(off, win), cols], preferred_element_type=f32)
            o = _take_diag(rr, l, GRID_W)
            ya_ref[0, qrows, cols] = (o * sza_ref[0, qrows, cols].astype(f32)).astype(bf16)

    def sw_block(nl, shift):
        q0 = i * ATT_TILE + nl * SW_BLOCK
        w0 = jnp.clip(q0 - WINDOW, 0, seq_len - SW_KWIN)
        var = (q0 - w0) // SW_BLOCK
        qrows = pl.ds(pl.multiple_of(nl * SW_BLOCK, SW_BLOCK), SW_BLOCK)
        krows = pl.ds(pl.multiple_of(w0 - sw_start, SW_BLOCK), SW_KWIN)
        for j in range(SW_KV_HEADS):
            cols = slice(j * MXU_DIM, (j + 1) * MXU_DIM)
            qs = _stack_heads(qb_ref[0, qrows, cols], hm_ref)
            s = lax.dot_general(qs, kb_ref[krows, cols], nt, preferred_element_type=f32)
            p, l = _softmax_rows(
                s,
                lambda m: jnp.concatenate([swm_ref[var, :, m * LANES:(m + 1) * LANES]] * SW_GROUP, axis=0),
                sink_ref[j], shift)
            rr = jnp.dot(p, vb_ref[krows, cols], preferred_element_type=f32)
            o = _take_diag(rr, l, SW_BLOCK)
            yb_ref[0, qrows, cols] = (o * szb_ref[0, qrows, cols].astype(f32)).astype(bf16)

    @pl.when(bounded_ref[0] != 0)
    def _():
        for rl in range(ATT_TILE // GRID_W):
            na_row(rl, False)
        for nl in range(ATT_TILE // SW_BLOCK):
            sw_block(nl, False)

    @pl.when(bounded_ref[0] == 0)
    def _():
        def na_body(rl, carry):
            na_row(rl, True)
            return carry

        def sw_body(nl, carry):
            sw_block(nl, True)
            return carry

        lax.fori_loop(0, ATT_TILE // GRID_W, na_body, 0)
        lax.fori_loop(0, ATT_TILE // SW_BLOCK, sw_body, 0)


def _window_start(i, halo, ext, seq_len):
    return jnp.clip(i * ATT_TILE - halo, 0, seq_len - ext)


def _attention(bounded, qa, ka, va, sza, qb, kb, vb, szb, pb, swm, sink_col, hm):
    B, S, _ = qa.shape
    tq = ATT_TILE
    grid = (B, S // tq)

    tok = lambda w: pl.BlockSpec((1, tq, w), lambda b, i: (b, i, 0))

    def window(width, halo, ext):
        return pl.BlockSpec(
            (pl.Element(ext), pl.Element(width)),
            lambda b, i: (pl.multiple_of(b * S + _window_start(i, halo, ext, S), SW_BLOCK), 0))

    flat = lambda t: t.reshape(B * S, t.shape[-1])
    kern = functools.partial(_attn_kernel, seq_len=S)
    return pl.pallas_call(
        kern,
        grid=grid,
        in_specs=[
            pl.BlockSpec(memory_space=pltpu.SMEM),
            tok(NA_WIDTH), tok(NA_WIDTH), tok(SW_WIDTH), tok(SW_WIDTH),
            window(NA_WIDTH, NA_HALO, NA_EXT), window(NA_WIDTH, NA_HALO, NA_EXT),
            window(SW_WIDTH, WINDOW, SW_EXT), window(SW_WIDTH, WINDOW, SW_EXT),
            _const_spec(pb.shape), _const_spec(swm.shape), _const_spec(sink_col.shape),
            _const_spec(hm.shape),
        ],
        out_specs=[tok(NA_WIDTH), tok(SW_WIDTH)],
        out_shape=[jax.ShapeDtypeStruct((B, S, NA_WIDTH), jnp.bfloat16),
                   jax.ShapeDtypeStruct((B, S, SW_WIDTH), jnp.bfloat16)],
        compiler_params=pltpu.CompilerParams(
            dimension_semantics=("arbitrary", "arbitrary"), vmem_limit_bytes=VMEM_LIMIT),
        name="attn",
    )(bounded, qa, sza, qb, szb, flat(ka), flat(va), flat(kb), flat(vb), pb, swm, sink_col, hm)


def _merge_kernel(ya_ref, yb_ref, sga_ref, sgb_ref, x_ref, woa_ref, wob_ref, wo_ref, y_ref):
    f32 = jnp.float32
    pa = jnp.dot(ya_ref[0], woa_ref[...], preferred_element_type=f32)
    pbm = jnp.dot(yb_ref[0], wob_ref[...], preferred_element_type=f32)
    merged = (sga_ref[0].astype(f32) * pa + sgb_ref[0].astype(f32) * pbm).astype(jnp.bfloat16)
    y_ref[0] = x_ref[0] + jnp.dot(merged, wo_ref[...], preferred_element_type=f32)


def _merge(x, ya, yb, sga, sgb, woa, wob, wo):
    B, S, _ = x.shape
    tm = MERGE_TILE
    tok = lambda w: pl.BlockSpec((1, tm, w), lambda b, i: (b, i, 0))
    return pl.pallas_call(
        _merge_kernel,
        grid=(B, S // tm),
        in_specs=[tok(NA_WIDTH), tok(SW_WIDTH), tok(D_MODEL), tok(D_MODEL), tok(D_MODEL),
                  _const_spec(woa.shape), _const_spec(wob.shape), _const_spec(wo.shape)],
        out_specs=tok(D_MODEL),
        out_shape=jax.ShapeDtypeStruct((B, S, D_MODEL), jnp.float32),
        compiler_params=pltpu.CompilerParams(
            dimension_semantics=("arbitrary", "arbitrary"), vmem_limit_bytes=VMEM_LIMIT),
        name="merge",
    )(ya, yb, sga, sgb, x, woa, wob, wo)


def _na_bias_table(rpb):
    c = jnp.arange(GRID_W)
    cs = jnp.clip(c - NA_KW // 2, 0, GRID_W - NA_KW)
    col_in = (c[None, :] >= cs[:, None]) & (c[None, :] < cs[:, None] + NA_KW)
    pad = GRID_W - NA_KW
    rp = jnp.pad(rpb.astype(jnp.float32) * LOG2_E, ((0, 0), (0, 0), (pad, pad)))
    b = jnp.stack([rp[:, :, GRID_W - 1 - q:2 * GRID_W - 1 - q] for q in range(GRID_W)], axis=2)
    b = jnp.where(col_in[None, None], b, NEG_BIG)
    nd = 2 * NA_KH - 2
    pair = jnp.concatenate([b[:, :nd], b[:, 1:nd + 1]], axis=-1)
    pair = pair.reshape(NA_HEADS // HEADS_PER_MXU, HEADS_PER_MXU, nd, GRID_W, LANES)
    return pair.transpose(2, 0, 1, 3, 4).reshape(nd, NA_HEADS // HEADS_PER_MXU,
                                                 HEADS_PER_MXU * GRID_W, LANES)


def _sw_mask_table():
    qi = jnp.arange(SW_BLOCK)[None, :, None]
    kj = jnp.arange(SW_KWIN)[None, None, :]
    d = jnp.arange(SW_SHIFTS)[:, None, None]
    band = jnp.abs(kj - d * SW_BLOCK - qi) <= WINDOW
    return jnp.where(band, 0.0, NEG_BIG).astype(jnp.float32)


def _layer(x, p):
    S = x.shape[1]
    outs = _projection(x, p["g"], p["w_in"], p["bd"], p["qna"], p["kna"], p["qnb"], p["knb"],
                       p["cos"][:S], p["sin"][:S])
    qa, ka, va, sza, qb, kb, vb, szb, sga, sgb = outs
    ya, yb = _attention(p["bounded"], qa, ka, va, sza, qb, kb, vb, szb, p["pb"], p["swm"], p["sink"],
                        p["hm"])
    return _merge(x, ya, yb, sga, sgb, p["woa"], p["wob"], p["wo"])


def _prepare(S, norm_g, w_in, qn_a, kn_a, rpb_a, qn_b, kn_b, sink_b, w_out_a, w_out_b, w_o):
    f32, bf16 = jnp.float32, jnp.bfloat16
    seg = jnp.arange(MXU_DIM) // HEAD_DIM
    bd = jnp.where(seg[:, None] == seg[None, :], 1.0 / HEAD_DIM, 0.0).astype(bf16)
    half = HEAD_DIM // 2
    inv = ROPE_THETA ** (-jnp.arange(half, dtype=f32) / half)
    ang = jnp.arange(S, dtype=f32)[:, None] * inv[None, :]
    cos, sin = jnp.cos(ang), jnp.sin(ang)
    cos = jnp.tile(jnp.concatenate([cos, cos], axis=1), (1, LANES // HEAD_DIM))
    sin = jnp.tile(jnp.concatenate([-sin, sin], axis=1), (1, LANES // HEAD_DIM))
    lane_head = jnp.arange(MXU_DIM) // HEAD_DIM
    hm = (lane_head[None, None, :] == jnp.arange(HEADS_PER_MXU)[:, None, None])
    hm = jnp.broadcast_to(hm, (HEADS_PER_MXU, max(SW_BLOCK, GRID_W), MXU_DIM)).astype(bf16)
    sink_col = jnp.repeat((sink_b.astype(f32) * LOG2_E).reshape(SW_KV_HEADS, SW_GROUP), SW_BLOCK, axis=1)
    gain = lambda t: jnp.max(jnp.abs(t.astype(f32)))
    qk_bound = HEAD_DIM ** 0.5 * LOG2_E * (1.0 + 2.0 ** -6)
    bound_a = qk_bound * gain(qn_a) * gain(kn_a) + LOG2_E * gain(rpb_a)
    bound_b = jnp.maximum(qk_bound * gain(qn_b) * gain(kn_b), LOG2_E * gain(sink_b))
    bounded = (jnp.maximum(bound_a, bound_b) <= MAX_UNSHIFTED_LOGIT).astype(jnp.int32).reshape(1)
    return {
        "bounded": bounded,
        "g": norm_g.astype(f32).reshape(1, D_MODEL),
        "w_in": w_in.astype(bf16),
        "bd": bd,
        "qna": jnp.tile(qn_a.astype(f32), NA_HEADS).reshape(1, NA_WIDTH),
        "kna": jnp.tile(kn_a.astype(f32), NA_HEADS).reshape(1, NA_WIDTH),
        "qnb": jnp.tile(qn_b.astype(f32), SW_HEADS).reshape(1, SW_WIDTH),
        "knb": jnp.tile(kn_b.astype(f32), SW_KV_HEADS).reshape(1, SW_KV_WIDTH),
        "cos": cos, "sin": sin,
        "pb": _na_bias_table(rpb_a),
        "swm": _sw_mask_table(),
        "sink": sink_col.reshape(SW_KV_HEADS, SW_GROUP * SW_BLOCK, 1),
        "hm": hm,
        "woa": w_out_a.astype(bf16), "wob": w_out_b.astype(bf16), "wo": w_o.astype(bf16),
    }


def kernel(x_prompt, x_sample, norm_g, w_in, qn_a, kn_a, rpb_a, qn_b, kn_b, sink_b, w_out_a, w_out_b, w_o):
    depth = norm_g.shape[0]
    y_prompt, y_sample = x_prompt, x_sample
    S = max(x_prompt.shape[1], x_sample.shape[1])
    for l in range(depth):
        p = _prepare(S, norm_g[l], w_in[l], qn_a[l], kn_a[l], rpb_a[l], qn_b[l], kn_b[l], sink_b[l],
                     w_out_a[l], w_out_b[l], w_o[l])
        y_prompt = _layer(y_prompt, p)
        y_sample = _layer(y_sample, p)
    return (y_prompt, y_sample)
```

```python
import functools

import jax
import jax.numpy as jnp
import numpy as np
from jax import lax
from jax.experimental import pallas as pl
from jax.experimental.pallas import tpu as pltpu

D_MODEL = 1024
HEAD_DIM = 64
GRID_W = 64
NA_HEADS = 8
NA_KH = 8
NA_KW = 16
NA_WIDTH = NA_HEADS * HEAD_DIM
SW_HEADS = 8
SW_KV_HEADS = 2
SW_GROUP = SW_HEADS // SW_KV_HEADS
SW_WIDTH = SW_HEADS * HEAD_DIM
SW_KV_WIDTH = SW_KV_HEADS * HEAD_DIM
WINDOW = 128
ROPE_THETA = 10000.0
NORM_EPS = 1e-6

C_QA = 0
C_KA = C_QA + NA_WIDTH
C_VA = C_KA + NA_WIDTH
C_ZA = C_VA + NA_WIDTH
C_QB = C_ZA + NA_WIDTH
C_KB = C_QB + SW_WIDTH
C_VB = C_KB + SW_KV_WIDTH
C_ZB = C_VB + SW_KV_WIDTH
C_GA = C_ZB + SW_WIDTH
C_GB = C_GA + D_MODEL
IN_WIDTH = C_GB + D_MODEL

LANES = 128
MXU_DIM = 256
HEADS_PER_MXU = MXU_DIM // HEAD_DIM
NEG_BIG = -1e30
LOG2_E = 1.4426950408889634
MAX_UNSHIFTED_LOGIT = 32.0
VMEM_LIMIT = 56 * 1024 * 1024

PROJ_TILE = 512
ATT_TILE = 1024
ATT_ROWS = ATT_TILE // GRID_W
MERGE_TILE = 1024
NA_HALO = 256
NA_EXT = ATT_TILE + 2 * NA_HALO
SW_BLOCK = 64
SW_KWIN = 384
SW_EXT = ATT_TILE + SW_KWIN - SW_BLOCK
SW_SHIFTS = (SW_KWIN - SW_BLOCK) // SW_BLOCK + 1


def _segment_mean_sq(t, bd):
    w = t.shape[1]
    sq = (t * t).astype(jnp.bfloat16)
    outs = []
    for c in range(0, w, MXU_DIM):
        cw = min(MXU_DIM, w - c)
        outs.append(jnp.dot(sq[:, c:c + cw], bd[:cw, :cw], preferred_element_type=jnp.float32))
    return outs[0] if len(outs) == 1 else jnp.concatenate(outs, axis=1)


def _swap_halves(t):
    w = t.shape[1]
    lane = lax.broadcasted_iota(jnp.int32, (t.shape[0], LANES), 1)
    low = (lane % HEAD_DIM) < (HEAD_DIM // 2)
    outs = []
    for c in range(0, w, LANES):
        blk = t[:, c:c + LANES]
        fwd = pltpu.roll(blk, HEAD_DIM // 2, 1)
        bwd = pltpu.roll(blk, LANES - HEAD_DIM // 2, 1)
        outs.append(jnp.where(low, bwd, fwd))
    return outs[0] if len(outs) == 1 else jnp.concatenate(outs, axis=1)


def _tile_lanes(t, reps):
    return t if reps == 1 else jnp.concatenate([t] * reps, axis=1)


def _proj_kernel(x_ref, g_ref, w_ref, bd_ref, qna_ref, kna_ref, qnb_ref, knb_ref, cos_ref, sin_ref,
                 qa_o, ka_o, va_o, sza_o, qb_o, kb_o, vb_o, szb_o, sga_o, sgb_o):
    f32, bf16 = jnp.float32, jnp.bfloat16
    x = x_ref[0]
    ms = jnp.mean(x * x, axis=-1, keepdims=True)
    h = (x * lax.rsqrt(ms + NORM_EPS) * g_ref[...]).astype(bf16)
    bd = bd_ref[...]
    scale = HEAD_DIM ** -0.5 * LOG2_E

    def proj(c0, c1):
        return jnp.dot(h, w_ref[:, c0:c1], preferred_element_type=f32)

    def head_norm(t, gain):
        return t * lax.rsqrt(_segment_mean_sq(t, bd) + NORM_EPS) * gain

    def rotary(t):
        reps = t.shape[1] // LANES
        cos = _tile_lanes(cos_ref[...], reps)
        sin = _tile_lanes(sin_ref[...], reps)
        return t * cos + _swap_halves(t) * sin

    qa_o[0] = (head_norm(proj(C_QA, C_KA), qna_ref[...]) * scale).astype(bf16)
    ka_o[0] = head_norm(proj(C_KA, C_VA), kna_ref[...]).astype(bf16)
    va_o[0] = proj(C_VA, C_ZA).astype(bf16)
    sza_o[0] = jax.nn.silu(proj(C_ZA, C_QB)).astype(bf16)
    qb_o[0] = (rotary(head_norm(proj(C_QB, C_KB), qnb_ref[...])) * scale).astype(bf16)

    kvb = proj(C_KB, C_ZB)
    kb = rotary(head_norm(kvb[:, :SW_KV_WIDTH], knb_ref[...]))
    vb = kvb[:, SW_KV_WIDTH:]
    lane = lax.broadcasted_iota(jnp.int32, kb.shape, 1)
    first = lane < HEAD_DIM

    def replicate(t):
        r = pltpu.roll(t, HEAD_DIM, 1)
        h0 = jnp.where(first, t, r)
        h1 = jnp.where(first, r, t)
        return jnp.concatenate([h0, h0, h1, h1], axis=1)

    kb_o[0] = replicate(kb).astype(bf16)
    vb_o[0] = replicate(vb).astype(bf16)
    szb_o[0] = jax.nn.silu(proj(C_ZB, C_GA)).astype(bf16)
    sga_o[0] = jax.nn.sigmoid(proj(C_GA, C_GB)).astype(bf16)
    sgb_o[0] = jax.nn.sigmoid(proj(C_GB, IN_WIDTH)).astype(bf16)


def _const_spec(shape):
    nd = len(shape)
    return pl.BlockSpec(shape, lambda b, i: (0,) * nd)


def _projection(x, g, w_bf, bd, qna, kna, qnb, knb, cos, sin):
    B, S, _ = x.shape
    tm = PROJ_TILE
    grid = (B, S // tm)
    tok = lambda w: pl.BlockSpec((1, tm, w), lambda b, i: (b, i, 0))
    out_widths = [NA_WIDTH, NA_WIDTH, NA_WIDTH, NA_WIDTH, SW_WIDTH, SW_WIDTH, SW_WIDTH, SW_WIDTH,
                  D_MODEL, D_MODEL]
    return pl.pallas_call(
        _proj_kernel,
        grid=grid,
        in_specs=[
            tok(D_MODEL),
            _const_spec((1, D_MODEL)),
            _const_spec((D_MODEL, IN_WIDTH)),
            _const_spec((MXU_DIM, MXU_DIM)),
            _const_spec((1, NA_WIDTH)),
            _const_spec((1, NA_WIDTH)),
            _const_spec((1, SW_WIDTH)),
            _const_spec((1, SW_KV_WIDTH)),
            pl.BlockSpec((tm, LANES), lambda b, i: (i, 0)),
            pl.BlockSpec((tm, LANES), lambda b, i: (i, 0)),
        ],
        out_specs=[tok(w) for w in out_widths],
        out_shape=[jax.ShapeDtypeStruct((B, S, w), jnp.bfloat16) for w in out_widths],
        compiler_params=pltpu.CompilerParams(
            dimension_semantics=("arbitrary", "arbitrary"), vmem_limit_bytes=VMEM_LIMIT),
        name="proj",
    )(x, g, w_bf, bd, qna, kna, qnb, knb, cos, sin)


def _stack_heads(q, hm_ref):
    rows = q.shape[0]
    return jnp.concatenate([q * hm_ref[hh, :rows, :] for hh in range(HEADS_PER_MXU)], axis=0)


def _take_diag(r, l, rows):
    lane = lax.broadcasted_iota(jnp.int32, (rows, MXU_DIM), 1) // HEAD_DIM
    last = HEADS_PER_MXU - 1
    num = r[last * rows:]
    den = jnp.broadcast_to(l[last * rows:], (rows, MXU_DIM))
    for hh in range(last - 1, -1, -1):
        num = jnp.where(lane == hh, r[hh * rows:(hh + 1) * rows], num)
        den = jnp.where(lane == hh, l[hh * rows:(hh + 1) * rows], den)
    return num / den


def _softmax_rows(s, add, floor, shift):
    n = s.shape[1] // LANES
    sb = [s[:, m * LANES:(m + 1) * LANES] + add(m) for m in range(n)]
    if shift:
        mx = jnp.max(functools.reduce(jnp.maximum, sb), axis=-1, keepdims=True)
        if floor is not None:
            mx = jnp.maximum(mx, floor)
            floor = floor - mx
        sb = [t - mx for t in sb]
    p = [jnp.exp2(t) for t in sb]
    l = jnp.sum(functools.reduce(jnp.add, p), axis=-1, keepdims=True)
    if floor is not None:
        l = l + jnp.exp2(floor)
    return jnp.concatenate(p, axis=1).astype(jnp.bfloat16), l


def _attn_kernel(bounded_ref, qa_ref, sza_ref, qb_ref, szb_ref, ka_ref, va_ref, kb_ref, vb_ref,
                 pb_ref, swm_ref, sink_ref, hm_ref, ya_ref, yb_ref, *, seq_len):
    f32, bf16 = jnp.float32, jnp.bfloat16
    i = pl.program_id(1)
    nt = (((1,), (1,)), ((), ()))
    rows_total = seq_len // GRID_W
    kh = min(NA_KH, rows_total)
    win = kh * GRID_W
    na_start = _window_start(i, NA_HALO, NA_EXT, seq_len)
    sw_start = _window_start(i, WINDOW, SW_EXT, seq_len)

    def na_row(rl, shift):
        r = i * ATT_ROWS + rl
        krow0 = jnp.clip(r - kh // 2, 0, rows_total - kh)
        off = pl.multiple_of(krow0 * GRID_W - na_start, GRID_W)
        d0 = krow0 - r + (NA_KH - 1)
        qrows = pl.ds(pl.multiple_of(rl * GRID_W, GRID_W), GRID_W)
        for g in range(NA_HEADS // HEADS_PER_MXU):
            cols = slice(g * MXU_DIM, (g + 1) * MXU_DIM)
            qs = _stack_heads(qa_ref[0, qrows, cols], hm_ref)
            s = lax.dot_general(qs, ka_ref[pl.ds(off, win), cols], nt,
                                preferred_element_type=f32)
            p, l = _softmax_rows(s, lambda m: pb_ref[d0 + 2 * m, g], None, shift)
            rr = jnp.dot(p, va_ref[pl.ds(off, win), cols], preferred_element_type=f32)
            o = _take_diag(rr, l, GRID_W)
            ya_ref[0, qrows, cols] = (o * sza_ref[0, qrows, cols].astype(f32)).astype(bf16)

    def sw_block(nl, shift):
        q0 = i * ATT_TILE + nl * SW_BLOCK
        w0 = jnp.clip(q0 - WINDOW, 0, seq_len - SW_KWIN)
        var = (q0 - w0) // SW_BLOCK
        qrows = pl.ds(pl.multiple_of(nl * SW_BLOCK, SW_BLOCK), SW_BLOCK)
        krows = pl.ds(pl.multiple_of(w0 - sw_start, SW_BLOCK), SW_KWIN)
        for j in range(SW_KV_HEADS):
            cols = slice(j * MXU_DIM, (j + 1) * MXU_DIM)
            qs = _stack_heads(qb_ref[0, qrows, cols], hm_ref)
            s = lax.dot_general(qs, kb_ref[krows, cols], nt, preferred_element_type=f32)
            p, l = _softmax_rows(
                s,
                lambda m: jnp.concatenate([swm_ref[var, :, m * LANES:(m + 1) * LANES]] * SW_GROUP, axis=0),
                sink_ref[j], shift)
            rr = jnp.dot(p, vb_ref[krows, cols], preferred_element_type=f32)
            o = _take_diag(rr, l, SW_BLOCK)
            yb_ref[0, qrows, cols] = (o * szb_ref[0, qrows, cols].astype(f32)).astype(bf16)

    @pl.when(bounded_ref[0] != 0)
    def _():
        for rl in range(ATT_TILE // GRID_W):
            na_row(rl, False)
        for nl in range(ATT_TILE // SW_BLOCK):
            sw_block(nl, False)

    @pl.when(bounded_ref[0] == 0)
    def _():
        def na_body(rl, carry):
            na_row(rl, True)
            return carry

        def sw_body(nl, carry):
            sw_block(nl, True)
            return carry

        lax.fori_loop(0, ATT_TILE // GRID_W, na_body, 0)
        lax.fori_loop(0, ATT_TILE // SW_BLOCK, sw_body, 0)


def _window_start(i, halo, ext, seq_len):
    return jnp.clip(i * ATT_TILE - halo, 0, seq_len - ext)


def _attention(bounded, qa, ka, va, sza, qb, kb, vb, szb, pb, swm, sink_col, hm):
    B, S, _ = qa.shape
    tq = ATT_TILE
    grid = (B, S // tq)

    tok = lambda w: pl.BlockSpec((1, tq, w), lambda b, i: (b, i, 0))

    def window(width, halo, ext):
        return pl.BlockSpec(
            (pl.Element(ext), pl.Element(width)),
            lambda b, i: (pl.multiple_of(b * S + _window_start(i, halo, ext, S), SW_BLOCK), 0))

    flat = lambda t: t.reshape(B * S, t.shape[-1])
    kern = functools.partial(_attn_kernel, seq_len=S)
    return pl.pallas_call(
        kern,
        grid=grid,
        in_specs=[
            pl.BlockSpec(memory_space=pltpu.SMEM),
            tok(NA_WIDTH), tok(NA_WIDTH), tok(SW_WIDTH), tok(SW_WIDTH),
            window(NA_WIDTH, NA_HALO, NA_EXT), window(NA_WIDTH, NA_HALO, NA_EXT),
            window(SW_WIDTH, WINDOW, SW_EXT), window(SW_WIDTH, WINDOW, SW_EXT),
            _const_spec(pb.shape), _const_spec(swm.shape), _const_spec(sink_col.shape),
            _const_spec(hm.shape),
        ],
        out_specs=[tok(NA_WIDTH), tok(SW_WIDTH)],
        out_shape=[jax.ShapeDtypeStruct((B, S, NA_WIDTH), jnp.bfloat16),
                   jax.ShapeDtypeStruct((B, S, SW_WIDTH), jnp.bfloat16)],
        compiler_params=pltpu.CompilerParams(
            dimension_semantics=("arbitrary", "arbitrary"), vmem_limit_bytes=VMEM_LIMIT),
        name="attn",
    )(bounded, qa, sza, qb, szb, flat(ka), flat(va), flat(kb), flat(vb), pb, swm, sink_col, hm)


def _merge_kernel(ya_ref, yb_ref, sga_ref, sgb_ref, x_ref, woa_ref, wob_ref, wo_ref, y_ref):
    f32 = jnp.float32
    pa = jnp.dot(ya_ref[0], woa_ref[...], preferred_element_type=f32)
    pbm = jnp.dot(yb_ref[0], wob_ref[...], preferred_element_type=f32)
    merged = (sga_ref[0].astype(f32) * pa + sgb_ref[0].astype(f32) * pbm).astype(jnp.bfloat16)
    y_ref[0] = x_ref[0] + jnp.dot(merged, wo_ref[...], preferred_element_type=f32)


def _merge(x, ya, yb, sga, sgb, woa, wob, wo):
    B, S, _ = x.shape
    tm = MERGE_TILE
    tok = lambda w: pl.BlockSpec((1, tm, w), lambda b, i: (b, i, 0))
    return pl.pallas_call(
        _merge_kernel,
        grid=(B, S // tm),
        in_specs=[tok(NA_WIDTH), tok(SW_WIDTH), tok(D_MODEL), tok(D_MODEL), tok(D_MODEL),
                  _const_spec(woa.shape), _const_spec(wob.shape), _const_spec(wo.shape)],
        out_specs=tok(D_MODEL),
        out_shape=jax.ShapeDtypeStruct((B, S, D_MODEL), jnp.float32),
        compiler_params=pltpu.CompilerParams(
            dimension_semantics=("arbitrary", "arbitrary"), vmem_limit_bytes=VMEM_LIMIT),
        name="merge",
    )(ya, yb, sga, sgb, x, woa, wob, wo)


def _na_bias_table(rpb):
    c = np.arange(GRID_W)
    cs = np.clip(c - NA_KW // 2, 0, GRID_W - NA_KW)
    col_in = (c[None, :] >= cs[:, None]) & (c[None, :] < cs[:, None] + NA_KW)
    pad = GRID_W - NA_KW
    rp = jnp.pad(rpb.astype(jnp.float32) * LOG2_E, ((0, 0), (0, 0), (pad, pad)))
    b = jnp.stack([rp[:, :, GRID_W - 1 - q:2 * GRID_W - 1 - q] for q in range(GRID_W)], axis=2)
    b = jnp.where(col_in[None, None], b, NEG_BIG)
    nd = 2 * NA_KH - 2
    pair = jnp.concatenate([b[:, :nd], b[:, 1:nd + 1]], axis=-1)
    pair = pair.reshape(NA_HEADS // HEADS_PER_MXU, HEADS_PER_MXU, nd, GRID_W, LANES)
    return pair.transpose(2, 0, 1, 3, 4).reshape(nd, NA_HEADS // HEADS_PER_MXU,
                                                 HEADS_PER_MXU * GRID_W, LANES)


def _sw_mask_table():
    qi = np.arange(SW_BLOCK)[None, :, None]
    kj = np.arange(SW_KWIN)[None, None, :]
    d = np.arange(SW_SHIFTS)[:, None, None]
    band = np.abs(kj - d * SW_BLOCK - qi) <= WINDOW
    return jnp.asarray(np.where(band, 0.0, NEG_BIG), dtype=jnp.float32)


def _layer(x, p):
    S = x.shape[1]
    outs = _projection(x, p["g"], p["w_in"], p["bd"], p["qna"], p["kna"], p["qnb"], p["knb"],
                       p["cos"][:S], p["sin"][:S])
    qa, ka, va, sza, qb, kb, vb, szb, sga, sgb = outs
    ya, yb = _attention(p["bounded"], qa, ka, va, sza, qb, kb, vb, szb, p["pb"], p["swm"], p["sink"],
                        p["hm"])
    return _merge(x, ya, yb, sga, sgb, p["woa"], p["wob"], p["wo"])


def _prepare(S, norm_g, w_in, qn_a, kn_a, rpb_a, qn_b, kn_b, sink_b, w_out_a, w_out_b, w_o):
    f32, bf16 = jnp.float32, jnp.bfloat16
    seg = np.arange(MXU_DIM) // HEAD_DIM
    bd = jnp.asarray(np.where(seg[:, None] == seg[None, :], 1.0 / HEAD_DIM, 0.0), dtype=bf16)
    hm = seg[None, None, :] == np.arange(HEADS_PER_MXU)[:, None, None]
    hm = jnp.asarray(np.broadcast_to(hm, (HEADS_PER_MXU, max(SW_BLOCK, GRID_W), MXU_DIM)), dtype=bf16)
    half = HEAD_DIM // 2
    inv = ROPE_THETA ** (-jnp.arange(half, dtype=f32) / half)
    ang = jnp.arange(S, dtype=f32)[:, None] * inv[None, :]
    cos, sin = jnp.cos(ang), jnp.sin(ang)
    cos = jnp.tile(jnp.concatenate([cos, cos], axis=1), (1, LANES // HEAD_DIM))
    sin = jnp.tile(jnp.concatenate([-sin, sin], axis=1), (1, LANES // HEAD_DIM))
    sink_col = jnp.repeat((sink_b.astype(f32) * LOG2_E).reshape(SW_KV_HEADS, SW_GROUP), SW_BLOCK, axis=1)
    gains = jnp.stack([qn_a, kn_a, qn_b, kn_b]).astype(f32)
    gmax = jnp.max(jnp.abs(gains), axis=1)
    amax = lambda t: jnp.max(jnp.abs(t.astype(f32)))
    qk_bound = HEAD_DIM ** 0.5 * LOG2_E * (1.0 + 2.0 ** -6)
    bound_a = qk_bound * gmax[0] * gmax[1] + LOG2_E * amax(rpb_a)
    bound_b = jnp.maximum(qk_bound * gmax[2] * gmax[3], LOG2_E * amax(sink_b))
    bounded = (jnp.maximum(bound_a, bound_b) <= MAX_UNSHIFTED_LOGIT).astype(jnp.int32).reshape(1)
    gains = jnp.tile(gains, (1, NA_HEADS))
    return {
        "bounded": bounded,
        "g": norm_g.astype(f32).reshape(1, D_MODEL),
        "w_in": w_in.astype(bf16),
        "bd": bd,
        "qna": gains[0:1], "kna": gains[1:2], "qnb": gains[2:3], "knb": gains[3:4, :SW_KV_WIDTH],
        "cos": cos, "sin": sin,
        "pb": _na_bias_table(rpb_a),
        "swm": _sw_mask_table(),
        "sink": sink_col.reshape(SW_KV_HEADS, SW_GROUP * SW_BLOCK, 1),
        "hm": hm,
        "woa": w_out_a.astype(bf16), "wob": w_out_b.astype(bf16), "wo": w_o.astype(bf16),
    }


def kernel(x_prompt, x_sample, norm_g, w_in, qn_a, kn_a, rpb_a, qn_b, kn_b, sink_b, w_out_a, w_out_b, w_o):
    depth = norm_g.shape[0]
    y_prompt, y_sample = x_prompt, x_sample
    S = max(x_prompt.shape[1], x_sample.shape[1])
    for l in range(depth):
        p = _prepare(S, norm_g[l], w_in[l], qn_a[l], kn_a[l], rpb_a[l], qn_b[l], kn_b[l], sink_b[l],
                     w_out_a[l], w_out_b[l], w_o[l])
        y_prompt = _layer(y_prompt, p)
        y_sample = _layer(y_sample, p)
    return (y_prompt, y_sample)
```

```python
import functools

import jax
import jax.numpy as jnp
import numpy as np
from jax import lax
from jax.experimental import pallas as pl
from jax.experimental.pallas import tpu as pltpu

D_MODEL = 1024
HEAD_DIM = 64
GRID_W = 64
NA_HEADS = 8
NA_KH = 8
NA_KW = 16
NA_WIDTH = NA_HEADS * HEAD_DIM
SW_HEADS = 8
SW_KV_HEADS = 2
SW_GROUP = SW_HEADS // SW_KV_HEADS
SW_WIDTH = SW_HEADS * HEAD_DIM
SW_KV_WIDTH = SW_KV_HEADS * HEAD_DIM
WINDOW = 128
ROPE_THETA = 10000.0
NORM_EPS = 1e-6

C_QA = 0
C_KA = C_QA + NA_WIDTH
C_VA = C_KA + NA_WIDTH
C_ZA = C_VA + NA_WIDTH
C_QB = C_ZA + NA_WIDTH
C_KB = C_QB + SW_WIDTH
C_VB = C_KB + SW_KV_WIDTH
C_ZB = C_VB + SW_KV_WIDTH
C_GA = C_ZB + SW_WIDTH
C_GB = C_GA + D_MODEL
IN_WIDTH = C_GB + D_MODEL

LANES = 128
MXU_DIM = 256
HEADS_PER_MXU = MXU_DIM // HEAD_DIM
NEG_BIG = -1e30
LOG2_E = 1.4426950408889634
MAX_UNSHIFTED_LOGIT = 32.0
VMEM_LIMIT = 56 * 1024 * 1024

PROJ_TILE = 512
ATT_TILE = 1024
ATT_ROWS = ATT_TILE // GRID_W
MERGE_TILE = 1024
NA_HALO = 256
NA_EXT = ATT_TILE + 2 * NA_HALO
SW_BLOCK = 64
SW_KWIN = 384
SW_EXT = ATT_TILE + SW_KWIN - SW_BLOCK
SW_SHIFTS = (SW_KWIN - SW_BLOCK) // SW_BLOCK + 1


def _segment_mean_sq(t, bd):
    w = t.shape[1]
    sq = (t * t).astype(jnp.bfloat16)
    outs = []
    for c in range(0, w, MXU_DIM):
        cw = min(MXU_DIM, w - c)
        outs.append(jnp.dot(sq[:, c:c + cw], bd[:cw, :cw], preferred_element_type=jnp.float32))
    return outs[0] if len(outs) == 1 else jnp.concatenate(outs, axis=1)


def _swap_halves(t):
    w = t.shape[1]
    lane = lax.broadcasted_iota(jnp.int32, (t.shape[0], LANES), 1)
    low = (lane % HEAD_DIM) < (HEAD_DIM // 2)
    outs = []
    for c in range(0, w, LANES):
        blk = t[:, c:c + LANES]
        fwd = pltpu.roll(blk, HEAD_DIM // 2, 1)
        bwd = pltpu.roll(blk, LANES - HEAD_DIM // 2, 1)
        outs.append(jnp.where(low, bwd, fwd))
    return outs[0] if len(outs) == 1 else jnp.concatenate(outs, axis=1)


def _tile_lanes(t, reps):
    return t if reps == 1 else jnp.concatenate([t] * reps, axis=1)


def _proj_kernel(x_ref, g_ref, w_ref, bd_ref, qna_ref, kna_ref, qnb_ref, knb_ref, cos_ref, sin_ref,
                 qa_o, ka_o, va_o, sza_o, qb_o, kb_o, vb_o, szb_o, sga_o, sgb_o):
    f32, bf16 = jnp.float32, jnp.bfloat16
    x = x_ref[0]
    ms = jnp.mean(x * x, axis=-1, keepdims=True)
    h = (x * lax.rsqrt(ms + NORM_EPS) * g_ref[...]).astype(bf16)
    bd = bd_ref[...]
    scale = HEAD_DIM ** -0.5 * LOG2_E

    def proj(c0, c1):
        return jnp.dot(h, w_ref[:, c0:c1], preferred_element_type=f32)

    def head_norm(t, gain):
        return t * lax.rsqrt(_segment_mean_sq(t, bd) + NORM_EPS) * gain

    def rotary(t):
        reps = t.shape[1] // LANES
        cos = _tile_lanes(cos_ref[...], reps)
        sin = _tile_lanes(sin_ref[...], reps)
        return t * cos + _swap_halves(t) * sin

    qa_o[0] = (head_norm(proj(C_QA, C_KA), qna_ref[...]) * scale).astype(bf16)
    ka_o[0] = head_norm(proj(C_KA, C_VA), kna_ref[...]).astype(bf16)
    va_o[0] = proj(C_VA, C_ZA).astype(bf16)
    sza_o[0] = jax.nn.silu(proj(C_ZA, C_QB)).astype(bf16)
    qb_o[0] = (rotary(head_norm(proj(C_QB, C_KB), qnb_ref[...])) * scale).astype(bf16)

    kvb = proj(C_KB, C_ZB)
    kb = rotary(head_norm(kvb[:, :SW_KV_WIDTH], knb_ref[...]))
    vb = kvb[:, SW_KV_WIDTH:]
    lane = lax.broadcasted_iota(jnp.int32, kb.shape, 1)
    first = lane < HEAD_DIM

    def replicate(t):
        r = pltpu.roll(t, HEAD_DIM, 1)
        h0 = jnp.where(first, t, r)
        h1 = jnp.where(first, r, t)
        return jnp.concatenate([h0, h0, h1, h1], axis=1)

    kb_o[0] = replicate(kb).astype(bf16)
    vb_o[0] = replicate(vb).astype(bf16)
    szb_o[0] = jax.nn.silu(proj(C_ZB, C_GA)).astype(bf16)
    sga_o[0] = jax.nn.sigmoid(proj(C_GA, C_GB)).astype(bf16)
    sgb_o[0] = jax.nn.sigmoid(proj(C_GB, IN_WIDTH)).astype(bf16)


def _const_spec(shape):
    nd = len(shape)
    return pl.BlockSpec(shape, lambda b, i: (0,) * nd)


def _projection(x, g, w_bf, bd, qna, kna, qnb, knb, cos, sin):
    B, S, _ = x.shape
    tm = PROJ_TILE
    grid = (B, S // tm)
    tok = lambda w: pl.BlockSpec((1, tm, w), lambda b, i: (b, i, 0))
    out_widths = [NA_WIDTH, NA_WIDTH, NA_WIDTH, NA_WIDTH, SW_WIDTH, SW_WIDTH, SW_WIDTH, SW_WIDTH,
                  D_MODEL, D_MODEL]
    return pl.pallas_call(
        _proj_kernel,
        grid=grid,
        in_specs=[
            tok(D_MODEL),
            _const_spec((1, D_MODEL)),
            _const_spec((D_MODEL, IN_WIDTH)),
            _const_spec((MXU_DIM, MXU_DIM)),
            _const_spec((1, NA_WIDTH)),
            _const_spec((1, NA_WIDTH)),
            _const_spec((1, SW_WIDTH)),
            _const_spec((1, SW_KV_WIDTH)),
            pl.BlockSpec((tm, LANES), lambda b, i: (i, 0)),
            pl.BlockSpec((tm, LANES), lambda b, i: (i, 0)),
        ],
        out_specs=[tok(w) for w in out_widths],
        out_shape=[jax.ShapeDtypeStruct((B, S, w), jnp.bfloat16) for w in out_widths],
        compiler_params=pltpu.CompilerParams(
            dimension_semantics=("arbitrary", "arbitrary"), vmem_limit_bytes=VMEM_LIMIT),
        name="proj",
    )(x, g, w_bf, bd, qna, kna, qnb, knb, cos, sin)


def _stack_heads(q, hm_ref):
    rows = q.shape[0]
    return jnp.concatenate([q * hm_ref[hh, :rows, :] for hh in range(HEADS_PER_MXU)], axis=0)


def _take_diag(r, l, rows):
    lane = lax.broadcasted_iota(jnp.int32, (rows, MXU_DIM), 1) // HEAD_DIM
    last = HEADS_PER_MXU - 1
    num = r[last * rows:]
    den = jnp.broadcast_to(l[last * rows:], (rows, MXU_DIM))
    for hh in range(last - 1, -1, -1):
        num = jnp.where(lane == hh, r[hh * rows:(hh + 1) * rows], num)
        den = jnp.where(lane == hh, l[hh * rows:(hh + 1) * rows], den)
    return num / den


def _softmax_rows(s, add, floor, shift):
    n = s.shape[1] // LANES
    sb = [s[:, m * LANES:(m + 1) * LANES] + add(m) for m in range(n)]
    if shift:
        mx = jnp.max(functools.reduce(jnp.maximum, sb), axis=-1, keepdims=True)
        if floor is not None:
            mx = jnp.maximum(mx, floor)
            floor = floor - mx
        sb = [t - mx for t in sb]
    p = [jnp.exp2(t) for t in sb]
    l = jnp.sum(functools.reduce(jnp.add, p), axis=-1, keepdims=True)
    if floor is not None:
        l = l + jnp.exp2(floor)
    return jnp.concatenate(p, axis=1).astype(jnp.bfloat16), l


def _attn_kernel(bounded_ref, qa_ref, sza_ref, qb_ref, szb_ref, ka_ref, va_ref, kb_ref, vb_ref,
                 pb_ref, swm_ref, sink_ref, hm_ref, ya_ref, yb_ref, *, seq_len):
    f32, bf16 = jnp.float32, jnp.bfloat16
    i = pl.program_id(1)
    nt = (((1,), (1,)), ((), ()))
    rows_total = seq_len // GRID_W
    kh = min(NA_KH, rows_total)
    win = kh * GRID_W
    na_start = _window_start(i, NA_HALO, NA_EXT, seq_len)
    sw_start = _window_start(i, WINDOW, SW_EXT, seq_len)

    def na_row(rl, shift):
        r = i * ATT_ROWS + rl
        krow0 = jnp.clip(r - kh // 2, 0, rows_total - kh)
        off = pl.multiple_of(krow0 * GRID_W - na_start, GRID_W)
        d0 = krow0 - r + (NA_KH - 1)
        qrows = pl.ds(pl.multiple_of(rl * GRID_W, GRID_W), GRID_W)
        for g in range(NA_HEADS // HEADS_PER_MXU):
            cols = slice(g * MXU_DIM, (g + 1) * MXU_DIM)
            qs = _stack_heads(qa_ref[0, qrows, cols], hm_ref)
            s = lax.dot_general(qs, ka_ref[pl.ds(off, win), cols], nt,
                                preferred_element_type=f32)
            p, l = _softmax_rows(
                s,
                lambda m: jnp.concatenate([pb_ref[g * HEADS_PER_MXU + hh, d0 + 2 * m]
                                           for hh in range(HEADS_PER_MXU)], axis=0),
                None, shift)
            rr = jnp.dot(p, va_ref[pl.ds(off, win), cols], preferred_element_type=f32)
            o = _take_diag(rr, l, GRID_W)
            ya_ref[0, qrows, cols] = (o * sza_ref[0, qrows, cols].astype(f32)).astype(bf16)

    def sw_block(nl, shift):
        q0 = i * ATT_TILE + nl * SW_BLOCK
        w0 = jnp.clip(q0 - WINDOW, 0, seq_len - SW_KWIN)
        var = (q0 - w0) // SW_BLOCK
        qrows = pl.ds(pl.multiple_of(nl * SW_BLOCK, SW_BLOCK), SW_BLOCK)
        krows = pl.ds(pl.multiple_of(w0 - sw_start, SW_BLOCK), SW_KWIN)
        for j in range(SW_KV_HEADS):
            cols = slice(j * MXU_DIM, (j + 1) * MXU_DIM)
            qs = _stack_heads(qb_ref[0, qrows, cols], hm_ref)
            s = lax.dot_general(qs, kb_ref[krows, cols], nt, preferred_element_type=f32)
            p, l = _softmax_rows(
                s,
                lambda m: jnp.concatenate([swm_ref[var, :, m * LANES:(m + 1) * LANES]] * SW_GROUP, axis=0),
                sink_ref[j], shift)
            rr = jnp.dot(p, vb_ref[krows, cols], preferred_element_type=f32)
            o = _take_diag(rr, l, SW_BLOCK)
            yb_ref[0, qrows, cols] = (o * szb_ref[0, qrows, cols].astype(f32)).astype(bf16)

    @pl.when(bounded_ref[0] != 0)
    def _():
        for rl in range(ATT_TILE // GRID_W):
            na_row(rl, False)
        for nl in range(ATT_TILE // SW_BLOCK):
            sw_block(nl, False)

    @pl.when(bounded_ref[0] == 0)
    def _():
        def na_body(rl, carry):
            na_row(rl, True)
            return carry

        def sw_body(nl, carry):
            sw_block(nl, True)
            return carry

        lax.fori_loop(0, ATT_TILE // GRID_W, na_body, 0)
        lax.fori_loop(0, ATT_TILE // SW_BLOCK, sw_body, 0)


def _window_start(i, halo, ext, seq_len):
    return jnp.clip(i * ATT_TILE - halo, 0, seq_len - ext)


def _attention(bounded, qa, ka, va, sza, qb, kb, vb, szb, pb, swm, sink_col, hm):
    B, S, _ = qa.shape
    tq = ATT_TILE
    grid = (B, S // tq)

    tok = lambda w: pl.BlockSpec((1, tq, w), lambda b, i: (b, i, 0))

    def window(width, halo, ext):
        return pl.BlockSpec(
            (pl.Element(ext), pl.Element(width)),
            lambda b, i: (pl.multiple_of(b * S + _window_start(i, halo, ext, S), SW_BLOCK), 0))

    flat = lambda t: t.reshape(B * S, t.shape[-1])
    kern = functools.partial(_attn_kernel, seq_len=S)
    return pl.pallas_call(
        kern,
        grid=grid,
        in_specs=[
            pl.BlockSpec(memory_space=pltpu.SMEM),
            tok(NA_WIDTH), tok(NA_WIDTH), tok(SW_WIDTH), tok(SW_WIDTH),
            window(NA_WIDTH, NA_HALO, NA_EXT), window(NA_WIDTH, NA_HALO, NA_EXT),
            window(SW_WIDTH, WINDOW, SW_EXT), window(SW_WIDTH, WINDOW, SW_EXT),
            _const_spec(pb.shape), _const_spec(swm.shape), _const_spec(sink_col.shape),
            _const_spec(hm.shape),
        ],
        out_specs=[tok(NA_WIDTH), tok(SW_WIDTH)],
        out_shape=[jax.ShapeDtypeStruct((B, S, NA_WIDTH), jnp.bfloat16),
                   jax.ShapeDtypeStruct((B, S, SW_WIDTH), jnp.bfloat16)],
        compiler_params=pltpu.CompilerParams(
            dimension_semantics=("arbitrary", "arbitrary"), vmem_limit_bytes=VMEM_LIMIT),
        name="attn",
    )(bounded, qa, sza, qb, szb, flat(ka), flat(va), flat(kb), flat(vb), pb, swm, sink_col, hm)


def _merge_kernel(ya_ref, yb_ref, sga_ref, sgb_ref, x_ref, woa_ref, wob_ref, wo_ref, y_ref):
    f32 = jnp.float32
    pa = jnp.dot(ya_ref[0], woa_ref[...], preferred_element_type=f32)
    pbm = jnp.dot(yb_ref[0], wob_ref[...], preferred_element_type=f32)
    merged = (sga_ref[0].astype(f32) * pa + sgb_ref[0].astype(f32) * pbm).astype(jnp.bfloat16)
    y_ref[0] = x_ref[0] + jnp.dot(merged, wo_ref[...], preferred_element_type=f32)


def _merge(x, ya, yb, sga, sgb, woa, wob, wo):
    B, S, _ = x.shape
    tm = MERGE_TILE
    tok = lambda w: pl.BlockSpec((1, tm, w), lambda b, i: (b, i, 0))
    return pl.pallas_call(
        _merge_kernel,
        grid=(B, S // tm),
        in_specs=[tok(NA_WIDTH), tok(SW_WIDTH), tok(D_MODEL), tok(D_MODEL), tok(D_MODEL),
                  _const_spec(woa.shape), _const_spec(wob.shape), _const_spec(wo.shape)],
        out_specs=tok(D_MODEL),
        out_shape=jax.ShapeDtypeStruct((B, S, D_MODEL), jnp.float32),
        compiler_params=pltpu.CompilerParams(
            dimension_semantics=("arbitrary", "arbitrary"), vmem_limit_bytes=VMEM_LIMIT),
        name="merge",
    )(ya, yb, sga, sgb, x, woa, wob, wo)


def _na_bias_table(rpb):
    c = np.arange(GRID_W)
    cs = np.clip(c - NA_KW // 2, 0, GRID_W - NA_KW)
    col_in = (c[None, :] >= cs[:, None]) & (c[None, :] < cs[:, None] + NA_KW)
    pad = GRID_W - NA_KW
    rp = jnp.pad(rpb.astype(jnp.float32) * LOG2_E, ((0, 0), (0, 0), (pad, pad)))
    b = jnp.stack([rp[:, :, GRID_W - 1 - q:2 * GRID_W - 1 - q] for q in range(GRID_W)], axis=2)
    b = jnp.where(col_in[None, None], b, NEG_BIG)
    nd = 2 * NA_KH - 2
    return jnp.concatenate([b[:, :nd], b[:, 1:nd + 1]], axis=-1)


def _sw_mask_table():
    qi = np.arange(SW_BLOCK)[None, :, None]
    kj = np.arange(SW_KWIN)[None, None, :]
    d = np.arange(SW_SHIFTS)[:, None, None]
    band = np.abs(kj - d * SW_BLOCK - qi) <= WINDOW
    return jnp.asarray(np.where(band, 0.0, NEG_BIG), dtype=jnp.float32)


def _layer(x, p):
    S = x.shape[1]
    outs = _projection(x, p["g"], p["w_in"], p["bd"], p["qna"], p["kna"], p["qnb"], p["knb"],
                       p["cos"][:S], p["sin"][:S])
    qa, ka, va, sza, qb, kb, vb, szb, sga, sgb = outs
    ya, yb = _attention(p["bounded"], qa, ka, va, sza, qb, kb, vb, szb, p["pb"], p["swm"], p["sink"],
                        p["hm"])
    return _merge(x, ya, yb, sga, sgb, p["woa"], p["wob"], p["wo"])


def _prepare(S, norm_g, w_in, qn_a, kn_a, rpb_a, qn_b, kn_b, sink_b, w_out_a, w_out_b, w_o):
    f32, bf16 = jnp.float32, jnp.bfloat16
    seg = np.arange(MXU_DIM) // HEAD_DIM
    bd = jnp.asarray(np.where(seg[:, None] == seg[None, :], 1.0 / HEAD_DIM, 0.0), dtype=bf16)
    hm = seg[None, None, :] == np.arange(HEADS_PER_MXU)[:, None, None]
    hm = jnp.asarray(np.broadcast_to(hm, (HEADS_PER_MXU, max(SW_BLOCK, GRID_W), MXU_DIM)), dtype=bf16)
    half = HEAD_DIM // 2
    inv = (np.float32(ROPE_THETA) ** (-np.arange(half, dtype=np.float32) / np.float32(half))).astype(np.float32)
    ang = np.arange(S, dtype=np.float32)[:, None] * inv[None, :]
    cos, sin = np.cos(ang), np.sin(ang)
    cos = jnp.asarray(np.tile(np.concatenate([cos, cos], axis=1), (1, LANES // HEAD_DIM)), dtype=f32)
    sin = jnp.asarray(np.tile(np.concatenate([-sin, sin], axis=1), (1, LANES // HEAD_DIM)), dtype=f32)
    sink_col = jnp.repeat((sink_b.astype(f32) * LOG2_E).reshape(SW_KV_HEADS, SW_GROUP), SW_BLOCK, axis=1)
    gains = jnp.stack([qn_a, kn_a, qn_b, kn_b]).astype(f32)
    gmax = jnp.max(jnp.abs(gains), axis=1)
    amax = lambda t: jnp.max(jnp.abs(t.astype(f32)))
    qk_bound = HEAD_DIM ** 0.5 * LOG2_E * (1.0 + 2.0 ** -6)
    bound_a = qk_bound * gmax[0] * gmax[1] + LOG2_E * amax(rpb_a)
    bound_b = jnp.maximum(qk_bound * gmax[2] * gmax[3], LOG2_E * amax(sink_b))
    bounded = (jnp.maximum(bound_a, bound_b) <= MAX_UNSHIFTED_LOGIT).astype(jnp.int32).reshape(1)
    gains = jnp.tile(gains, (1, NA_HEADS))
    return {
        "bounded": bounded,
        "g": norm_g.astype(f32).reshape(1, D_MODEL),
        "w_in": w_in.astype(bf16),
        "bd": bd,
        "qna": gains[0:1], "kna": gains[1:2], "qnb": gains[2:3], "knb": gains[3:4, :SW_KV_WIDTH],
        "cos": cos, "sin": sin,
        "pb": _na_bias_table(rpb_a),
        "swm": _sw_mask_table(),
        "sink": sink_col.reshape(SW_KV_HEADS, SW_GROUP * SW_BLOCK, 1),
        "hm": hm,
        "woa": w_out_a.astype(bf16), "wob": w_out_b.astype(bf16), "wo": w_o.astype(bf16),
    }


def kernel(x_prompt, x_sample, norm_g, w_in, qn_a, kn_a, rpb_a, qn_b, kn_b, sink_b, w_out_a, w_out_b, w_o):
    depth = norm_g.shape[0]
    y_prompt, y_sample = x_prompt, x_sample
    S = max(x_prompt.shape[1], x_sample.shape[1])
    for l in range(depth):
        p = _prepare(S, norm_g[l], w_in[l], qn_a[l], kn_a[l], rpb_a[l], qn_b[l], kn_b[l], sink_b[l],
                     w_out_a[l], w_out_b[l], w_o[l])
        y_prompt = _layer(y_prompt, p)
        y_sample = _layer(y_sample, p)
    return (y_prompt, y_sample)
```

```python
import functools

import jax
import jax.numpy as jnp
import numpy as np
from jax import lax
from jax.experimental import pallas as pl
from jax.experimental.pallas import tpu as pltpu

D_MODEL = 1024
HEAD_DIM = 64
GRID_W = 64
NA_HEADS = 8
NA_KH = 8
NA_KW = 16
NA_WIDTH = NA_HEADS * HEAD_DIM
SW_HEADS = 8
SW_KV_HEADS = 2
SW_GROUP = SW_HEADS // SW_KV_HEADS
SW_WIDTH = SW_HEADS * HEAD_DIM
SW_KV_WIDTH = SW_KV_HEADS * HEAD_DIM
WINDOW = 128
ROPE_THETA = 10000.0
NORM_EPS = 1e-6

C_QA = 0
C_KA = C_QA + NA_WIDTH
C_VA = C_KA + NA_WIDTH
C_ZA = C_VA + NA_WIDTH
C_QB = C_ZA + NA_WIDTH
C_KB = C_QB + SW_WIDTH
C_VB = C_KB + SW_KV_WIDTH
C_ZB = C_VB + SW_KV_WIDTH
C_GA = C_ZB + SW_WIDTH
C_GB = C_GA + D_MODEL
IN_WIDTH = C_GB + D_MODEL

LANES = 128
MXU_DIM = 256
HEADS_PER_MXU = MXU_DIM // HEAD_DIM
NEG_BIG = -1e30
LOG2_E = 1.4426950408889634
MAX_UNSHIFTED_LOGIT = 32.0
VMEM_LIMIT = 56 * 1024 * 1024

PROJ_TILE = 512
ATT_TILE = 1024
ATT_ROWS = ATT_TILE // GRID_W
MERGE_TILE = 1024
NA_HALO = 256
NA_EXT = ATT_TILE + 2 * NA_HALO
SW_BLOCK = 64
SW_KWIN = 384
SW_EXT = ATT_TILE + SW_KWIN - SW_BLOCK
SW_SHIFTS = (SW_KWIN - SW_BLOCK) // SW_BLOCK + 1


def _segment_mean_sq(t, bd):
    w = t.shape[1]
    sq = (t * t).astype(jnp.bfloat16)
    outs = []
    for c in range(0, w, MXU_DIM):
        cw = min(MXU_DIM, w - c)
        outs.append(jnp.dot(sq[:, c:c + cw], bd[:cw, :cw], preferred_element_type=jnp.float32))
    return outs[0] if len(outs) == 1 else jnp.concatenate(outs, axis=1)


def _swap_halves(t):
    w = t.shape[1]
    lane = lax.broadcasted_iota(jnp.int32, (t.shape[0], LANES), 1)
    low = (lane % HEAD_DIM) < (HEAD_DIM // 2)
    outs = []
    for c in range(0, w, LANES):
        blk = t[:, c:c + LANES]
        fwd = pltpu.roll(blk, HEAD_DIM // 2, 1)
        bwd = pltpu.roll(blk, LANES - HEAD_DIM // 2, 1)
        outs.append(jnp.where(low, bwd, fwd))
    return outs[0] if len(outs) == 1 else jnp.concatenate(outs, axis=1)


def _tile_lanes(t, reps):
    return t if reps == 1 else jnp.concatenate([t] * reps, axis=1)


def _proj_kernel(x_ref, g_ref, w_ref, bd_ref, qna_ref, kna_ref, qnb_ref, knb_ref, cos_ref, sin_ref,
                 qa_o, ka_o, va_o, sza_o, qb_o, kb_o, vb_o, szb_o, sga_o, sgb_o):
    f32, bf16 = jnp.float32, jnp.bfloat16
    x = x_ref[0]
    ms = jnp.mean(x * x, axis=-1, keepdims=True)
    h = (x * lax.rsqrt(ms + NORM_EPS) * g_ref[...]).astype(bf16)
    bd = bd_ref[...]
    scale = HEAD_DIM ** -0.5 * LOG2_E

    def proj(c0, c1):
        return jnp.dot(h, w_ref[:, c0:c1], preferred_element_type=f32)

    def head_norm(t, gain):
        return t * lax.rsqrt(_segment_mean_sq(t, bd) + NORM_EPS) * gain

    def rotary(t):
        reps = t.shape[1] // LANES
        cos = _tile_lanes(cos_ref[...], reps)
        sin = _tile_lanes(sin_ref[...], reps)
        return t * cos + _swap_halves(t) * sin

    qa_o[0] = (head_norm(proj(C_QA, C_KA), qna_ref[...]) * scale).astype(bf16)
    ka_o[0] = head_norm(proj(C_KA, C_VA), kna_ref[...]).astype(bf16)
    va_o[0] = proj(C_VA, C_ZA).astype(bf16)
    sza_o[0] = jax.nn.silu(proj(C_ZA, C_QB)).astype(bf16)
    qb_o[0] = (rotary(head_norm(proj(C_QB, C_KB), qnb_ref[...])) * scale).astype(bf16)

    kvb = proj(C_KB, C_ZB)
    kb = rotary(head_norm(kvb[:, :SW_KV_WIDTH], knb_ref[...]))
    vb = kvb[:, SW_KV_WIDTH:]
    lane = lax.broadcasted_iota(jnp.int32, kb.shape, 1)
    first = lane < HEAD_DIM

    def replicate(t):
        r = pltpu.roll(t, HEAD_DIM, 1)
        h0 = jnp.where(first, t, r)
        h1 = jnp.where(first, r, t)
        return jnp.concatenate([h0, h0, h1, h1], axis=1)

    kb_o[0] = replicate(kb).astype(bf16)
    vb_o[0] = replicate(vb).astype(bf16)
    szb_o[0] = jax.nn.silu(proj(C_ZB, C_GA)).astype(bf16)
    sga_o[0] = jax.nn.sigmoid(proj(C_GA, C_GB)).astype(bf16)
    sgb_o[0] = jax.nn.sigmoid(proj(C_GB, IN_WIDTH)).astype(bf16)


def _const_spec(shape):
    nd = len(shape)
    return pl.BlockSpec(shape, lambda b, i: (0,) * nd)


def _projection(x, g, w_bf, bd, qna, kna, qnb, knb, cos, sin):
    B, S, _ = x.shape
    tm = PROJ_TILE
    grid = (B, S // tm)
    tok = lambda w: pl.BlockSpec((1, tm, w), lambda b, i: (b, i, 0))
    out_widths = [NA_WIDTH, NA_WIDTH, NA_WIDTH, NA_WIDTH, SW_WIDTH, SW_WIDTH, SW_WIDTH, SW_WIDTH,
                  D_MODEL, D_MODEL]
    return pl.pallas_call(
        _proj_kernel,
        grid=grid,
        in_specs=[
            tok(D_MODEL),
            _const_spec((1, D_MODEL)),
            _const_spec((D_MODEL, IN_WIDTH)),
            _const_spec((MXU_DIM, MXU_DIM)),
            _const_spec((1, NA_WIDTH)),
            _const_spec((1, NA_WIDTH)),
            _const_spec((1, SW_WIDTH)),
            _const_spec((1, SW_KV_WIDTH)),
            pl.BlockSpec((tm, LANES), lambda b, i: (i, 0)),
            pl.BlockSpec((tm, LANES), lambda b, i: (i, 0)),
        ],
        out_specs=[tok(w) for w in out_widths],
        out_shape=[jax.ShapeDtypeStruct((B, S, w), jnp.bfloat16) for w in out_widths],
        compiler_params=pltpu.CompilerParams(
            dimension_semantics=("arbitrary", "arbitrary"), vmem_limit_bytes=VMEM_LIMIT),
        name="proj",
    )(x, g, w_bf, bd, qna, kna, qnb, knb, cos, sin)


def _stack_heads(q, hm_ref):
    rows = q.shape[0]
    return jnp.concatenate([q * hm_ref[hh, :rows, :] for hh in range(HEADS_PER_MXU)], axis=0)


def _take_diag(r, l, rows):
    lane = lax.broadcasted_iota(jnp.int32, (rows, MXU_DIM), 1) // HEAD_DIM
    last = HEADS_PER_MXU - 1
    num = r[last * rows:]
    den = jnp.broadcast_to(l[last * rows:], (rows, MXU_DIM))
    for hh in range(last - 1, -1, -1):
        num = jnp.where(lane == hh, r[hh * rows:(hh + 1) * rows], num)
        den = jnp.where(lane == hh, l[hh * rows:(hh + 1) * rows], den)
    return num / den


def _softmax_rows(s, add, floor, shift):
    n = s.shape[1] // LANES
    sb = [s[:, m * LANES:(m + 1) * LANES] + add(m) for m in range(n)]
    if shift:
        mx = jnp.max(functools.reduce(jnp.maximum, sb), axis=-1, keepdims=True)
        if floor is not None:
            mx = jnp.maximum(mx, floor)
            floor = floor - mx
        sb = [t - mx for t in sb]
    p = [jnp.exp2(t) for t in sb]
    l = jnp.sum(functools.reduce(jnp.add, p), axis=-1, keepdims=True)
    if floor is not None:
        l = l + jnp.exp2(floor)
    return jnp.concatenate(p, axis=1).astype(jnp.bfloat16), l


def _attn_kernel(bounded_ref, qa_ref, sza_ref, qb_ref, szb_ref, ka_ref, va_ref, kb_ref, vb_ref,
                 pb_ref, swm_ref, sink_ref, hm_ref, ya_ref, yb_ref, *, seq_len):
    f32, bf16 = jnp.float32, jnp.bfloat16
    i = pl.program_id(1)
    nt = (((1,), (1,)), ((), ()))
    rows_total = seq_len // GRID_W
    kh = min(NA_KH, rows_total)
    win = kh * GRID_W
    na_start = _window_start(i, NA_HALO, NA_EXT, seq_len)
    sw_start = _window_start(i, WINDOW, SW_EXT, seq_len)

    def na_row(rl, shift):
        r = i * ATT_ROWS + rl
        krow0 = jnp.clip(r - kh // 2, 0, rows_total - kh)
        off = pl.multiple_of(krow0 * GRID_W - na_start, GRID_W)
        d0 = krow0 - r + (NA_KH - 1)
        qrows = pl.ds(pl.multiple_of(rl * GRID_W, GRID_W), GRID_W)
        for g in range(NA_HEADS // HEADS_PER_MXU):
            cols = slice(g * MXU_DIM, (g + 1) * MXU_DIM)
            qs = _stack_heads(qa_ref[0, qrows, cols], hm_ref)
            s = lax.dot_general(qs, ka_ref[pl.ds(off, win), cols], nt,
                                preferred_element_type=f32)
            p, l = _softmax_rows(
                s,
                lambda m: jnp.concatenate([pb_ref[g * HEADS_PER_MXU + hh, d0 + 2 * m]
                                           for hh in range(HEADS_PER_MXU)], axis=0),
                None, shift)
            rr = jnp.dot(p, va_ref[pl.ds(off, win), cols], preferred_element_type=f32)
            o = _take_diag(rr, l, GRID_W)
            ya_ref[0, qrows, cols] = (o * sza_ref[0, qrows, cols].astype(f32)).astype(bf16)

    def sw_block(nl, shift):
        q0 = i * ATT_TILE + nl * SW_BLOCK
        w0 = jnp.clip(q0 - WINDOW, 0, seq_len - SW_KWIN)
        var = (q0 - w0) // SW_BLOCK
        qrows = pl.ds(pl.multiple_of(nl * SW_BLOCK, SW_BLOCK), SW_BLOCK)
        krows = pl.ds(pl.multiple_of(w0 - sw_start, SW_BLOCK), SW_KWIN)
        for j in range(SW_KV_HEADS):
            cols = slice(j * MXU_DIM, (j + 1) * MXU_DIM)
            qs = _stack_heads(qb_ref[0, qrows, cols], hm_ref)
            s = lax.dot_general(qs, kb_ref[krows, cols], nt, preferred_element_type=f32)
            p, l = _softmax_rows(
                s,
                lambda m: jnp.concatenate([swm_ref[var, :, m * LANES:(m + 1) * LANES]] * SW_GROUP, axis=0),
                sink_ref[j], shift)
            rr = jnp.dot(p, vb_ref[krows, cols], preferred_element_type=f32)
            o = _take_diag(rr, l, SW_BLOCK)
            yb_ref[0, qrows, cols] = (o * szb_ref[0, qrows, cols].astype(f32)).astype(bf16)

    @pl.when(bounded_ref[0] != 0)
    def _():
        for rl in range(ATT_TILE // GRID_W):
            na_row(rl, False)
        for nl in range(ATT_TILE // SW_BLOCK):
            sw_block(nl, False)

    @pl.when(bounded_ref[0] == 0)
    def _():
        def na_body(rl, carry):
            na_row(rl, True)
            return carry

        def sw_body(nl, carry):
            sw_block(nl, True)
            return carry

        lax.fori_loop(0, ATT_TILE // GRID_W, na_body, 0)
        lax.fori_loop(0, ATT_TILE // SW_BLOCK, sw_body, 0)


def _window_start(i, halo, ext, seq_len):
    return jnp.clip(i * ATT_TILE - halo, 0, seq_len - ext)


def _attention(bounded, qa, ka, va, sza, qb, kb, vb, szb, pb, swm, sink_col, hm):
    B, S, _ = qa.shape
    tq = ATT_TILE
    grid = (B, S // tq)

    tok = lambda w: pl.BlockSpec((1, tq, w), lambda b, i: (b, i, 0))

    def window(width, halo, ext):
        return pl.BlockSpec(
            (pl.Element(ext), pl.Element(width)),
            lambda b, i: (pl.multiple_of(b * S + _window_start(i, halo, ext, S), SW_BLOCK), 0))

    flat = lambda t: t.reshape(B * S, t.shape[-1])
    kern = functools.partial(_attn_kernel, seq_len=S)
    return pl.pallas_call(
        kern,
        grid=grid,
        in_specs=[
            pl.BlockSpec(memory_space=pltpu.SMEM),
            tok(NA_WIDTH), tok(NA_WIDTH), tok(SW_WIDTH), tok(SW_WIDTH),
            window(NA_WIDTH, NA_HALO, NA_EXT), window(NA_WIDTH, NA_HALO, NA_EXT),
            window(SW_WIDTH, WINDOW, SW_EXT), window(SW_WIDTH, WINDOW, SW_EXT),
            _const_spec(pb.shape), _const_spec(swm.shape), _const_spec(sink_col.shape),
            _const_spec(hm.shape),
        ],
        out_specs=[tok(NA_WIDTH), tok(SW_WIDTH)],
        out_shape=[jax.ShapeDtypeStruct((B, S, NA_WIDTH), jnp.bfloat16),
                   jax.ShapeDtypeStruct((B, S, SW_WIDTH), jnp.bfloat16)],
        compiler_params=pltpu.CompilerParams(
            dimension_semantics=("arbitrary", "arbitrary"), vmem_limit_bytes=VMEM_LIMIT),
        name="attn",
    )(bounded, qa, sza, qb, szb, flat(ka), flat(va), flat(kb), flat(vb), pb, swm, sink_col, hm)


def _merge_kernel(ya_ref, yb_ref, sga_ref, sgb_ref, x_ref, woa_ref, wob_ref, wo_ref, y_ref):
    f32 = jnp.float32
    pa = jnp.dot(ya_ref[0], woa_ref[...], preferred_element_type=f32)
    pbm = jnp.dot(yb_ref[0], wob_ref[...], preferred_element_type=f32)
    merged = (sga_ref[0].astype(f32) * pa + sgb_ref[0].astype(f32) * pbm).astype(jnp.bfloat16)
    y_ref[0] = x_ref[0] + jnp.dot(merged, wo_ref[...], preferred_element_type=f32)


def _merge(x, ya, yb, sga, sgb, woa, wob, wo):
    B, S, _ = x.shape
    tm = MERGE_TILE
    tok = lambda w: pl.BlockSpec((1, tm, w), lambda b, i: (b, i, 0))
    return pl.pallas_call(
        _merge_kernel,
        grid=(B, S // tm),
        in_specs=[tok(NA_WIDTH), tok(SW_WIDTH), tok(D_MODEL), tok(D_MODEL), tok(D_MODEL),
                  _const_spec(woa.shape), _const_spec(wob.shape), _const_spec(wo.shape)],
        out_specs=tok(D_MODEL),
        out_shape=jax.ShapeDtypeStruct((B, S, D_MODEL), jnp.float32),
        compiler_params=pltpu.CompilerParams(
            dimension_semantics=("arbitrary", "arbitrary"), vmem_limit_bytes=VMEM_LIMIT),
        name="merge",
    )(ya, yb, sga, sgb, x, woa, wob, wo)


def _na_bias_table(rpb):
    c = np.arange(GRID_W)
    cs = np.clip(c - NA_KW // 2, 0, GRID_W - NA_KW)
    col_in = (c[None, :] >= cs[:, None]) & (c[None, :] < cs[:, None] + NA_KW)
    nt = 2 * NA_KW - 1
    t_idx = c[None, :] - c[:, None] + (NA_KW - 1)
    onehot = (t_idx[None] == np.arange(nt)[:, None, None]) & col_in[None]
    onehot = jnp.asarray(onehot.reshape(nt, GRID_W * GRID_W), dtype=jnp.float32)
    flat = (rpb.astype(jnp.float32) * LOG2_E).reshape(NA_HEADS * (2 * NA_KH - 1), nt)
    b = jnp.dot(flat, onehot, precision=lax.Precision.HIGHEST)
    b = b.reshape(NA_HEADS, 2 * NA_KH - 1, GRID_W, GRID_W)
    b = jnp.where(col_in[None, None], b, NEG_BIG)
    nd = 2 * NA_KH - 2
    return jnp.concatenate([b[:, :nd], b[:, 1:nd + 1]], axis=-1)


def _sw_mask_table():
    qi = np.arange(SW_BLOCK)[None, :, None]
    kj = np.arange(SW_KWIN)[None, None, :]
    d = np.arange(SW_SHIFTS)[:, None, None]
    band = np.abs(kj - d * SW_BLOCK - qi) <= WINDOW
    return jnp.asarray(np.where(band, 0.0, NEG_BIG), dtype=jnp.float32)


def _layer(x, p):
    S = x.shape[1]
    outs = _projection(x, p["g"], p["w_in"], p["bd"], p["qna"], p["kna"], p["qnb"], p["knb"],
                       p["cos"][:S], p["sin"][:S])
    qa, ka, va, sza, qb, kb, vb, szb, sga, sgb = outs
    ya, yb = _attention(p["bounded"], qa, ka, va, sza, qb, kb, vb, szb, p["pb"], p["swm"], p["sink"],
                        p["hm"])
    return _merge(x, ya, yb, sga, sgb, p["woa"], p["wob"], p["wo"])


def _prepare(S, norm_g, w_in, qn_a, kn_a, rpb_a, qn_b, kn_b, sink_b, w_out_a, w_out_b, w_o):
    f32, bf16 = jnp.float32, jnp.bfloat16
    seg = np.arange(MXU_DIM) // HEAD_DIM
    bd = jnp.asarray(np.where(seg[:, None] == seg[None, :], 1.0 / HEAD_DIM, 0.0), dtype=bf16)
    hm = seg[None, None, :] == np.arange(HEADS_PER_MXU)[:, None, None]
    hm = jnp.asarray(np.broadcast_to(hm, (HEADS_PER_MXU, max(SW_BLOCK, GRID_W), MXU_DIM)), dtype=bf16)
    half = HEAD_DIM // 2
    inv = (np.float32(ROPE_THETA) ** (-np.arange(half, dtype=np.float32) / np.float32(half))).astype(np.float32)
    ang = np.arange(S, dtype=np.float32)[:, None] * inv[None, :]
    cos, sin = np.cos(ang), np.sin(ang)
    cos = jnp.asarray(np.tile(np.concatenate([cos, cos], axis=1), (1, LANES // HEAD_DIM)), dtype=f32)
    sin = jnp.asarray(np.tile(np.concatenate([-sin, sin], axis=1), (1, LANES // HEAD_DIM)), dtype=f32)
    sink_col = jnp.repeat((sink_b.astype(f32) * LOG2_E).reshape(SW_KV_HEADS, SW_GROUP), SW_BLOCK, axis=1)
    gains = jnp.stack([qn_a, kn_a, qn_b, kn_b]).astype(f32)
    gmax = jnp.max(jnp.abs(gains), axis=1)
    amax = lambda t: jnp.max(jnp.abs(t.astype(f32)))
    qk_bound = HEAD_DIM ** 0.5 * LOG2_E * (1.0 + 2.0 ** -6)
    bound_a = qk_bound * gmax[0] * gmax[1] + LOG2_E * amax(rpb_a)
    bound_b = jnp.maximum(qk_bound * gmax[2] * gmax[3], LOG2_E * amax(sink_b))
    bounded = (jnp.maximum(bound_a, bound_b) <= MAX_UNSHIFTED_LOGIT).astype(jnp.int32).reshape(1)
    gains = jnp.tile(gains, (1, NA_HEADS))
    return {
        "bounded": bounded,
        "g": norm_g.astype(f32).reshape(1, D_MODEL),
        "w_in": w_in.astype(bf16),
        "bd": bd,
        "qna": gains[0:1], "kna": gains[1:2], "qnb": gains[2:3], "knb": gains[3:4, :SW_KV_WIDTH],
        "cos": cos, "sin": sin,
        "pb": _na_bias_table(rpb_a),
        "swm": _sw_mask_table(),
        "sink": sink_col.reshape(SW_KV_HEADS, SW_GROUP * SW_BLOCK, 1),
        "hm": hm,
        "woa": w_out_a.astype(bf16), "wob": w_out_b.astype(bf16), "wo": w_o.astype(bf16),
    }


def kernel(x_prompt, x_sample, norm_g, w_in, qn_a, kn_a, rpb_a, qn_b, kn_b, sink_b, w_out_a, w_out_b, w_o):
    depth = norm_g.shape[0]
    y_prompt, y_sample = x_prompt, x_sample
    S = max(x_prompt.shape[1], x_sample.shape[1])
    for l in range(depth):
        p = _prepare(S, norm_g[l], w_in[l], qn_a[l], kn_a[l], rpb_a[l], qn_b[l], kn_b[l], sink_b[l],
                     w_out_a[l], w_out_b[l], w_o[l])
        y_prompt = _layer(y_prompt, p)
        y_sample = _layer(y_sample, p)
    return (y_prompt, y_sample)
```

```python
import functools

import jax
import jax.numpy as jnp
import numpy as np
from jax import lax
from jax.experimental import pallas as pl
from jax.experimental.pallas import tpu as pltpu

D_MODEL = 1024
HEAD_DIM = 64
GRID_W = 64
NA_HEADS = 8
NA_KH = 8
NA_KW = 16
NA_WIDTH = NA_HEADS * HEAD_DIM
SW_HEADS = 8
SW_KV_HEADS = 2
SW_GROUP = SW_HEADS // SW_KV_HEADS
SW_WIDTH = SW_HEADS * HEAD_DIM
SW_KV_WIDTH = SW_KV_HEADS * HEAD_DIM
WINDOW = 128
ROPE_THETA = 10000.0
NORM_EPS = 1e-6

C_QA = 0
C_KA = C_QA + NA_WIDTH
C_VA = C_KA + NA_WIDTH
C_ZA = C_VA + NA_WIDTH
C_QB = C_ZA + NA_WIDTH
C_KB = C_QB + SW_WIDTH
C_VB = C_KB + SW_KV_WIDTH
C_ZB = C_VB + SW_KV_WIDTH
C_GA = C_ZB + SW_WIDTH
C_GB = C_GA + D_MODEL
IN_WIDTH = C_GB + D_MODEL

LANES = 128
MXU_DIM = 256
HEADS_PER_MXU = MXU_DIM // HEAD_DIM
NEG_BIG = -1e30
LOG2_E = 1.4426950408889634
MAX_UNSHIFTED_LOGIT = 32.0
VMEM_LIMIT = 56 * 1024 * 1024

PROJ_TILE = 1024
ATT_TILE = 1024
ATT_ROWS = ATT_TILE // GRID_W
MERGE_TILE = 1024
NA_HALO = 256
NA_EXT = ATT_TILE + 2 * NA_HALO
SW_BLOCK = 64
SW_KWIN = 384
SW_EXT = ATT_TILE + SW_KWIN - SW_BLOCK
SW_SHIFTS = (SW_KWIN - SW_BLOCK) // SW_BLOCK + 1


def _segment_mean_sq(t, bd):
    w = t.shape[1]
    sq = (t * t).astype(jnp.bfloat16)
    outs = []
    for c in range(0, w, MXU_DIM):
        cw = min(MXU_DIM, w - c)
        outs.append(jnp.dot(sq[:, c:c + cw], bd[:cw, :cw], preferred_element_type=jnp.float32))
    return outs[0] if len(outs) == 1 else jnp.concatenate(outs, axis=1)


def _swap_halves(t):
    w = t.shape[1]
    lane = lax.broadcasted_iota(jnp.int32, (t.shape[0], LANES), 1)
    low = (lane % HEAD_DIM) < (HEAD_DIM // 2)
    outs = []
    for c in range(0, w, LANES):
        blk = t[:, c:c + LANES]
        fwd = pltpu.roll(blk, HEAD_DIM // 2, 1)
        bwd = pltpu.roll(blk, LANES - HEAD_DIM // 2, 1)
        outs.append(jnp.where(low, bwd, fwd))
    return outs[0] if len(outs) == 1 else jnp.concatenate(outs, axis=1)


def _tile_lanes(t, reps):
    return t if reps == 1 else jnp.concatenate([t] * reps, axis=1)


def _proj_kernel(x_ref, g_ref, w_ref, bd_ref, qna_ref, kna_ref, qnb_ref, knb_ref, cos_ref, sin_ref,
                 qa_o, ka_o, va_o, sza_o, qb_o, kb_o, vb_o, szb_o, sga_o, sgb_o):
    f32, bf16 = jnp.float32, jnp.bfloat16
    x = x_ref[0]
    ms = jnp.mean(x * x, axis=-1, keepdims=True)
    h = (x * lax.rsqrt(ms + NORM_EPS) * g_ref[...]).astype(bf16)
    bd = bd_ref[...]
    scale = HEAD_DIM ** -0.5 * LOG2_E

    def proj(c0, c1):
        return jnp.dot(h, w_ref[:, c0:c1], preferred_element_type=f32)

    def head_norm(t, gain):
        return t * lax.rsqrt(_segment_mean_sq(t, bd) + NORM_EPS) * gain

    def rotary(t):
        reps = t.shape[1] // LANES
        cos = _tile_lanes(cos_ref[...], reps)
        sin = _tile_lanes(sin_ref[...], reps)
        return t * cos + _swap_halves(t) * sin

    qa_o[0] = (head_norm(proj(C_QA, C_KA), qna_ref[...]) * scale).astype(bf16)
    ka_o[0] = head_norm(proj(C_KA, C_VA), kna_ref[...]).astype(bf16)
    va_o[0] = proj(C_VA, C_ZA).astype(bf16)
    sza_o[0] = jax.nn.silu(proj(C_ZA, C_QB)).astype(bf16)
    qb_o[0] = (rotary(head_norm(proj(C_QB, C_KB), qnb_ref[...])) * scale).astype(bf16)

    kvb = proj(C_KB, C_ZB)
    kb = rotary(head_norm(kvb[:, :SW_KV_WIDTH], knb_ref[...]))
    vb = kvb[:, SW_KV_WIDTH:]
    lane = lax.broadcasted_iota(jnp.int32, kb.shape, 1)
    first = lane < HEAD_DIM

    def replicate(t):
        r = pltpu.roll(t, HEAD_DIM, 1)
        h0 = jnp.where(first, t, r)
        h1 = jnp.where(first, r, t)
        return jnp.concatenate([h0, h0, h1, h1], axis=1)

    kb_o[0] = replicate(kb).astype(bf16)
    vb_o[0] = replicate(vb).astype(bf16)
    szb_o[0] = jax.nn.silu(proj(C_ZB, C_GA)).astype(bf16)
    sga_o[0] = jax.nn.sigmoid(proj(C_GA, C_GB)).astype(bf16)
    sgb_o[0] = jax.nn.sigmoid(proj(C_GB, IN_WIDTH)).astype(bf16)


def _const_spec(shape, single=False):
    nd = len(shape)
    mode = {"pipeline_mode": pl.Buffered(1)} if single else {}
    return pl.BlockSpec(shape, lambda b, i: (0,) * nd, **mode)


def _projection(x, g, w_bf, bd, qna, kna, qnb, knb, cos, sin):
    B, S, _ = x.shape
    tm = PROJ_TILE
    grid = (B, S // tm)
    tok = lambda w: pl.BlockSpec((1, tm, w), lambda b, i: (b, i, 0))
    out_widths = [NA_WIDTH, NA_WIDTH, NA_WIDTH, NA_WIDTH, SW_WIDTH, SW_WIDTH, SW_WIDTH, SW_WIDTH,
                  D_MODEL, D_MODEL]
    return pl.pallas_call(
        _proj_kernel,
        grid=grid,
        in_specs=[
            tok(D_MODEL),
            _const_spec((1, D_MODEL)),
            _const_spec((D_MODEL, IN_WIDTH), single=True),
            _const_spec((MXU_DIM, MXU_DIM)),
            _const_spec((1, NA_WIDTH)),
            _const_spec((1, NA_WIDTH)),
            _const_spec((1, SW_WIDTH)),
            _const_spec((1, SW_KV_WIDTH)),
            pl.BlockSpec((tm, LANES), lambda b, i: (i, 0)),
            pl.BlockSpec((tm, LANES), lambda b, i: (i, 0)),
        ],
        out_specs=[tok(w) for w in out_widths],
        out_shape=[jax.ShapeDtypeStruct((B, S, w), jnp.bfloat16) for w in out_widths],
        compiler_params=pltpu.CompilerParams(
            dimension_semantics=("arbitrary", "arbitrary"), vmem_limit_bytes=VMEM_LIMIT),
        name="proj",
    )(x, g, w_bf, bd, qna, kna, qnb, knb, cos, sin)


def _stack_heads(q, hm_ref):
    rows = q.shape[0]
    return jnp.concatenate([q * hm_ref[hh, :rows, :] for hh in range(HEADS_PER_MXU)], axis=0)


def _take_diag(r, l, rows):
    lane = lax.broadcasted_iota(jnp.int32, (rows, MXU_DIM), 1) // HEAD_DIM
    last = HEADS_PER_MXU - 1
    num = r[last * rows:]
    den = jnp.broadcast_to(l[last * rows:], (rows, MXU_DIM))
    for hh in range(last - 1, -1, -1):
        num = jnp.where(lane == hh, r[hh * rows:(hh + 1) * rows], num)
        den = jnp.where(lane == hh, l[hh * rows:(hh + 1) * rows], den)
    return num / den


def _softmax_rows(s, add, floor, shift):
    n = s.shape[1] // LANES
    sb = [s[:, m * LANES:(m + 1) * LANES] + add(m) for m in range(n)]
    if shift:
        mx = jnp.max(functools.reduce(jnp.maximum, sb), axis=-1, keepdims=True)
        if floor is not None:
            mx = jnp.maximum(mx, floor)
            floor = floor - mx
        sb = [t - mx for t in sb]
    p = [jnp.exp2(t) for t in sb]
    l = jnp.sum(functools.reduce(jnp.add, p), axis=-1, keepdims=True)
    if floor is not None:
        l = l + jnp.exp2(floor)
    return jnp.concatenate(p, axis=1).astype(jnp.bfloat16), l


def _attn_kernel(bounded_ref, qa_ref, sza_ref, qb_ref, szb_ref, ka_ref, va_ref, kb_ref, vb_ref,
                 pb_ref, swm_ref, sink_ref, hm_ref, ya_ref, yb_ref, *, seq_len):
    f32, bf16 = jnp.float32, jnp.bfloat16
    i = pl.program_id(1)
    nt = (((1,), (1,)), ((), ()))
    rows_total = seq_len // GRID_W
    kh = min(NA_KH, rows_total)
    win = kh * GRID_W
    na_start = _window_start(i, NA_HALO, NA_EXT, seq_len)
    sw_start = _window_start(i, WINDOW, SW_EXT, seq_len)

    def na_row(rl, shift):
        r = i * ATT_ROWS + rl
        krow0 = jnp.clip(r - kh // 2, 0, rows_total - kh)
        off = pl.multiple_of(krow0 * GRID_W - na_start, GRID_W)
        d0 = krow0 - r + (NA_KH - 1)
        qrows = pl.ds(pl.multiple_of(rl * GRID_W, GRID_W), GRID_W)
        for g in range(NA_HEADS // HEADS_PER_MXU):
            cols = slice(g * MXU_DIM, (g + 1) * MXU_DIM)
            qs = _stack_heads(qa_ref[0, qrows, cols], hm_ref)
            s = lax.dot_general(qs, ka_ref[pl.ds(off, win), cols], nt,
                                preferred_element_type=f32)
            p, l = _softmax_rows(
                s,
                lambda m: jnp.concatenate([pb_ref[g * HEADS_PER_MXU + hh, d0 + 2 * m]
                                           for hh in range(HEADS_PER_MXU)], axis=0),
                None, shift)
            rr = jnp.dot(p, va_ref[pl.ds(off, win), cols], preferred_element_type=f32)
            o = _take_diag(rr, l, GRID_W)
            ya_ref[0, qrows, cols] = (o * sza_ref[0, qrows, cols].astype(f32)).astype(bf16)

    def sw_block(nl, shift):
        q0 = i * ATT_TILE + nl * SW_BLOCK
        w0 = jnp.clip(q0 - WINDOW, 0, seq_len - SW_KWIN)
        var = (q0 - w0) // SW_BLOCK
        qrows = pl.ds(pl.multiple_of(nl * SW_BLOCK, SW_BLOCK), SW_BLOCK)
        krows = pl.ds(pl.multiple_of(w0 - sw_start, SW_BLOCK), SW_KWIN)
        for j in range(SW_KV_HEADS):
            cols = slice(j * MXU_DIM, (j + 1) * MXU_DIM)
            qs = _stack_heads(qb_ref[0, qrows, cols], hm_ref)
            s = lax.dot_general(qs, kb_ref[krows, cols], nt, preferred_element_type=f32)
            p, l = _softmax_rows(
                s,
                lambda m: jnp.concatenate([swm_ref[var, :, m * LANES:(m + 1) * LANES]] * SW_GROUP, axis=0),
                sink_ref[j], shift)
            rr = jnp.dot(p, vb_ref[krows, cols], preferred_element_type=f32)
            o = _take_diag(rr, l, SW_BLOCK)
            yb_ref[0, qrows, cols] = (o * szb_ref[0, qrows, cols].astype(f32)).astype(bf16)

    @pl.when(bounded_ref[0] != 0)
    def _():
        for rl in range(ATT_TILE // GRID_W):
            na_row(rl, False)
        for nl in range(ATT_TILE // SW_BLOCK):
            sw_block(nl, False)

    @pl.when(bounded_ref[0] == 0)
    def _():
        def na_body(rl, carry):
            na_row(rl, True)
            return carry

        def sw_body(nl, carry):
            sw_block(nl, True)
            return carry

        lax.fori_loop(0, ATT_TILE // GRID_W, na_body, 0)
        lax.fori_loop(0, ATT_TILE // SW_BLOCK, sw_body, 0)


def _window_start(i, halo, ext, seq_len):
    return jnp.clip(i * ATT_TILE - halo, 0, seq_len - ext)


def _attention(bounded, qa, ka, va, sza, qb, kb, vb, szb, pb, swm, sink_col, hm):
    B, S, _ = qa.shape
    tq = ATT_TILE
    grid = (B, S // tq)

    tok = lambda w: pl.BlockSpec((1, tq, w), lambda b, i: (b, i, 0))

    def window(width, halo, ext):
        return pl.BlockSpec(
            (pl.Element(ext), pl.Element(width)),
            lambda b, i: (pl.multiple_of(b * S + _window_start(i, halo, ext, S), SW_BLOCK), 0))

    flat = lambda t: t.reshape(B * S, t.shape[-1])
    kern = functools.partial(_attn_kernel, seq_len=S)
    return pl.pallas_call(
        kern,
        grid=grid,
        in_specs=[
            pl.BlockSpec(memory_space=pltpu.SMEM),
            tok(NA_WIDTH), tok(NA_WIDTH), tok(SW_WIDTH), tok(SW_WIDTH),
            window(NA_WIDTH, NA_HALO, NA_EXT), window(NA_WIDTH, NA_HALO, NA_EXT),
            window(SW_WIDTH, WINDOW, SW_EXT), window(SW_WIDTH, WINDOW, SW_EXT),
            _const_spec(pb.shape), _const_spec(swm.shape), _const_spec(sink_col.shape),
            _const_spec(hm.shape),
        ],
        out_specs=[tok(NA_WIDTH), tok(SW_WIDTH)],
        out_shape=[jax.ShapeDtypeStruct((B, S, NA_WIDTH), jnp.bfloat16),
                   jax.ShapeDtypeStruct((B, S, SW_WIDTH), jnp.bfloat16)],
        compiler_params=pltpu.CompilerParams(
            dimension_semantics=("arbitrary", "arbitrary"), vmem_limit_bytes=VMEM_LIMIT),
        name="attn",
    )(bounded, qa, sza, qb, szb, flat(ka), flat(va), flat(kb), flat(vb), pb, swm, sink_col, hm)


def _merge_kernel(ya_ref, yb_ref, sga_ref, sgb_ref, x_ref, woa_ref, wob_ref, wo_ref, y_ref):
    f32 = jnp.float32
    pa = jnp.dot(ya_ref[0], woa_ref[...], preferred_element_type=f32)
    pbm = jnp.dot(yb_ref[0], wob_ref[...], preferred_element_type=f32)
    merged = (sga_ref[0].astype(f32) * pa + sgb_ref[0].astype(f32) * pbm).astype(jnp.bfloat16)
    y_ref[0] = x_ref[0] + jnp.dot(merged, wo_ref[...], preferred_element_type=f32)


def _merge(x, ya, yb, sga, sgb, woa, wob, wo):
    B, S, _ = x.shape
    tm = MERGE_TILE
    tok = lambda w: pl.BlockSpec((1, tm, w), lambda b, i: (b, i, 0))
    return pl.pallas_call(
        _merge_kernel,
        grid=(B, S // tm),
        in_specs=[tok(NA_WIDTH), tok(SW_WIDTH), tok(D_MODEL), tok(D_MODEL), tok(D_MODEL),
                  _const_spec(woa.shape), _const_spec(wob.shape), _const_spec(wo.shape)],
        out_specs=tok(D_MODEL),
        out_shape=jax.ShapeDtypeStruct((B, S, D_MODEL), jnp.float32),
        compiler_params=pltpu.CompilerParams(
            dimension_semantics=("arbitrary", "arbitrary"), vmem_limit_bytes=VMEM_LIMIT),
        name="merge",
    )(ya, yb, sga, sgb, x, woa, wob, wo)


def _na_bias_table(rpb):
    c = np.arange(GRID_W)
    cs = np.clip(c - NA_KW // 2, 0, GRID_W - NA_KW)
    col_in = (c[None, :] >= cs[:, None]) & (c[None, :] < cs[:, None] + NA_KW)
    nt = 2 * NA_KW - 1
    t_idx = c[None, :] - c[:, None] + (NA_KW - 1)
    onehot = (t_idx[None] == np.arange(nt)[:, None, None]) & col_in[None]
    onehot = jnp.asarray(onehot.reshape(nt, GRID_W * GRID_W), dtype=jnp.float32)
    flat = (rpb.astype(jnp.float32) * LOG2_E).reshape(NA_HEADS * (2 * NA_KH - 1), nt)
    b = jnp.dot(flat, onehot, precision=lax.Precision.HIGHEST)
    b = b.reshape(NA_HEADS, 2 * NA_KH - 1, GRID_W, GRID_W)
    b = jnp.where(col_in[None, None], b, NEG_BIG)
    nd = 2 * NA_KH - 2
    return jnp.concatenate([b[:, :nd], b[:, 1:nd + 1]], axis=-1)


def _sw_mask_table():
    qi = np.arange(SW_BLOCK)[None, :, None]
    kj = np.arange(SW_KWIN)[None, None, :]
    d = np.arange(SW_SHIFTS)[:, None, None]
    band = np.abs(kj - d * SW_BLOCK - qi) <= WINDOW
    return jnp.asarray(np.where(band, 0.0, NEG_BIG), dtype=jnp.float32)


def _layer(x, p):
    S = x.shape[1]
    outs = _projection(x, p["g"], p["w_in"], p["bd"], p["qna"], p["kna"], p["qnb"], p["knb"],
                       p["cos"][:S], p["sin"][:S])
    qa, ka, va, sza, qb, kb, vb, szb, sga, sgb = outs
    ya, yb = _attention(p["bounded"], qa, ka, va, sza, qb, kb, vb, szb, p["pb"], p["swm"], p["sink"],
                        p["hm"])
    return _merge(x, ya, yb, sga, sgb, p["woa"], p["wob"], p["wo"])


def _prepare(S, norm_g, w_in, qn_a, kn_a, rpb_a, qn_b, kn_b, sink_b, w_out_a, w_out_b, w_o):
    f32, bf16 = jnp.float32, jnp.bfloat16
    seg = np.arange(MXU_DIM) // HEAD_DIM
    bd = jnp.asarray(np.where(seg[:, None] == seg[None, :], 1.0 / HEAD_DIM, 0.0), dtype=bf16)
    hm = seg[None, None, :] == np.arange(HEADS_PER_MXU)[:, None, None]
    hm = jnp.asarray(np.broadcast_to(hm, (HEADS_PER_MXU, max(SW_BLOCK, GRID_W), MXU_DIM)), dtype=bf16)
    half = HEAD_DIM // 2
    inv = (np.float32(ROPE_THETA) ** (-np.arange(half, dtype=np.float32) / np.float32(half))).astype(np.float32)
    ang = np.arange(S, dtype=np.float32)[:, None] * inv[None, :]
    cos, sin = np.cos(ang), np.sin(ang)
    cos = jnp.asarray(np.tile(np.concatenate([cos, cos], axis=1), (1, LANES // HEAD_DIM)), dtype=f32)
    sin = jnp.asarray(np.tile(np.concatenate([-sin, sin], axis=1), (1, LANES // HEAD_DIM)), dtype=f32)
    sink_col = jnp.repeat((sink_b.astype(f32) * LOG2_E).reshape(SW_KV_HEADS, SW_GROUP), SW_BLOCK, axis=1)
    gains = jnp.stack([qn_a, kn_a, qn_b, kn_b]).astype(f32)
    gmax = jnp.max(jnp.abs(gains), axis=1)
    amax = lambda t: jnp.max(jnp.abs(t.astype(f32)))
    qk_bound = HEAD_DIM ** 0.5 * LOG2_E * (1.0 + 2.0 ** -6)
    bound_a = qk_bound * gmax[0] * gmax[1] + LOG2_E * amax(rpb_a)
    bound_b = jnp.maximum(qk_bound * gmax[2] * gmax[3], LOG2_E * amax(sink_b))
    bounded = (jnp.maximum(bound_a, bound_b) <= MAX_UNSHIFTED_LOGIT).astype(jnp.int32).reshape(1)
    gains = jnp.tile(gains, (1, NA_HEADS))
    return {
        "bounded": bounded,
        "g": norm_g.astype(f32).reshape(1, D_MODEL),
        "w_in": w_in.astype(bf16),
        "bd": bd,
        "qna": gains[0:1], "kna": gains[1:2], "qnb": gains[2:3], "knb": gains[3:4, :SW_KV_WIDTH],
        "cos": cos, "sin": sin,
        "pb": _na_bias_table(rpb_a),
        "swm": _sw_mask_table(),
        "sink": sink_col.reshape(SW_KV_HEADS, SW_GROUP * SW_BLOCK, 1),
        "hm": hm,
        "woa": w_out_a.astype(bf16), "wob": w_out_b.astype(bf16), "wo": w_o.astype(bf16),
    }


def kernel(x_prompt, x_sample, norm_g, w_in, qn_a, kn_a, rpb_a, qn_b, kn_b, sink_b, w_out_a, w_out_b, w_o):
    depth = norm_g.shape[0]
    y_prompt, y_sample = x_prompt, x_sample
    S = max(x_prompt.shape[1], x_sample.shape[1])
    for l in range(depth):
        p = _prepare(S, norm_g[l], w_in[l], qn_a[l], kn_a[l], rpb_a[l], qn_b[l], kn_b[l], sink_b[l],
                     w_out_a[l], w_out_b[l], w_o[l])
        y_prompt = _layer(y_prompt, p)
        y_sample = _layer(y_sample, p)
    return (y_prompt, y_sample)
```

```python
import functools

import jax
import jax.numpy as jnp
import numpy as np
from jax import lax
from jax.experimental import pallas as pl
from jax.experimental.pallas import tpu as pltpu

D_MODEL = 1024
HEAD_DIM = 64
GRID_W = 64
NA_HEADS = 8
NA_KH = 8
NA_KW = 16
NA_WIDTH = NA_HEADS * HEAD_DIM
SW_HEADS = 8
SW_KV_HEADS = 2
SW_GROUP = SW_HEADS // SW_KV_HEADS
SW_WIDTH = SW_HEADS * HEAD_DIM
SW_KV_WIDTH = SW_KV_HEADS * HEAD_DIM
WINDOW = 128
ROPE_THETA = 10000.0
NORM_EPS = 1e-6

C_QA = 0
C_KA = C_QA + NA_WIDTH
C_VA = C_KA + NA_WIDTH
C_ZA = C_VA + NA_WIDTH
C_QB = C_ZA + NA_WIDTH
C_KB = C_QB + SW_WIDTH
C_VB = C_KB + SW_KV_WIDTH
C_ZB = C_VB + SW_KV_WIDTH
C_GA = C_ZB + SW_WIDTH
C_GB = C_GA + D_MODEL
IN_WIDTH = C_GB + D_MODEL

LANES = 128
MXU_DIM = 256
HEADS_PER_MXU = MXU_DIM // HEAD_DIM
NEG_BIG = -1e30
LOG2_E = 1.4426950408889634
MAX_UNSHIFTED_LOGIT = 32.0
VMEM_LIMIT = 56 * 1024 * 1024

PROJ_TILE = 512
ATT_TILE = 1024
ATT_ROWS = ATT_TILE // GRID_W
MERGE_TILE = 1024
NA_HALO = 256
NA_EXT = ATT_TILE + 2 * NA_HALO
SW_BLOCK = 64
SW_KWIN = 384
SW_EXT = ATT_TILE + SW_KWIN - SW_BLOCK
SW_SHIFTS = (SW_KWIN - SW_BLOCK) // SW_BLOCK + 1


def _segment_mean_sq(t, bd):
    w = t.shape[1]
    sq = (t * t).astype(jnp.bfloat16)
    outs = []
    for c in range(0, w, MXU_DIM):
        cw = min(MXU_DIM, w - c)
        outs.append(jnp.dot(sq[:, c:c + cw], bd[:cw, :cw], preferred_element_type=jnp.float32))
    return outs[0] if len(outs) == 1 else jnp.concatenate(outs, axis=1)


def _swap_halves(t):
    w = t.shape[1]
    lane = lax.broadcasted_iota(jnp.int32, (t.shape[0], LANES), 1)
    low = (lane % HEAD_DIM) < (HEAD_DIM // 2)
    outs = []
    for c in range(0, w, LANES):
        blk = t[:, c:c + LANES]
        fwd = pltpu.roll(blk, HEAD_DIM // 2, 1)
        bwd = pltpu.roll(blk, LANES - HEAD_DIM // 2, 1)
        outs.append(jnp.where(low, bwd, fwd))
    return outs[0] if len(outs) == 1 else jnp.concatenate(outs, axis=1)


def _tile_lanes(t, reps):
    return t if reps == 1 else jnp.concatenate([t] * reps, axis=1)


def _proj_kernel(x_ref, g_ref, w_ref, bd_ref, qna_ref, kna_ref, qnb_ref, knb_ref, cos_ref, sin_ref,
                 qa_o, ka_o, va_o, sza_o, qb_o, kb_o, vb_o, szb_o, sga_o, sgb_o):
    f32, bf16 = jnp.float32, jnp.bfloat16
    x = x_ref[0]
    ms = jnp.mean(x * x, axis=-1, keepdims=True)
    h = (x * lax.rsqrt(ms + NORM_EPS) * g_ref[...]).astype(bf16)
    bd = bd_ref[...]
    scale = HEAD_DIM ** -0.5 * LOG2_E

    def proj(c0, c1):
        return jnp.dot(h, w_ref[:, c0:c1].astype(bf16), preferred_element_type=f32)

    def head_norm(t, gain):
        return t * lax.rsqrt(_segment_mean_sq(t, bd) + NORM_EPS) * gain

    def rotary(t):
        reps = t.shape[1] // LANES
        cos = _tile_lanes(cos_ref[...], reps)
        sin = _tile_lanes(sin_ref[...], reps)
        return t * cos + _swap_halves(t) * sin

    qa_o[0] = (head_norm(proj(C_QA, C_KA), qna_ref[...]) * scale).astype(bf16)
    ka_o[0] = head_norm(proj(C_KA, C_VA), kna_ref[...]).astype(bf16)
    va_o[0] = proj(C_VA, C_ZA).astype(bf16)
    sza_o[0] = jax.nn.silu(proj(C_ZA, C_QB)).astype(bf16)
    qb_o[0] = (rotary(head_norm(proj(C_QB, C_KB), qnb_ref[...])) * scale).astype(bf16)

    kvb = proj(C_KB, C_ZB)
    kb = rotary(head_norm(kvb[:, :SW_KV_WIDTH], knb_ref[...]))
    vb = kvb[:, SW_KV_WIDTH:]
    lane = lax.broadcasted_iota(jnp.int32, kb.shape, 1)
    first = lane < HEAD_DIM

    def replicate(t):
        r = pltpu.roll(t, HEAD_DIM, 1)
        h0 = jnp.where(first, t, r)
        h1 = jnp.where(first, r, t)
        return jnp.concatenate([h0, h0, h1, h1], axis=1)

    kb_o[0] = replicate(kb).astype(bf16)
    vb_o[0] = replicate(vb).astype(bf16)
    szb_o[0] = jax.nn.silu(proj(C_ZB, C_GA)).astype(bf16)
    sga_o[0] = jax.nn.sigmoid(proj(C_GA, C_GB)).astype(bf16)
    sgb_o[0] = jax.nn.sigmoid(proj(C_GB, IN_WIDTH)).astype(bf16)


def _const_spec(shape):
    nd = len(shape)
    return pl.BlockSpec(shape, lambda b, i: (0,) * nd)


def _projection(x, g, w_bf, bd, qna, kna, qnb, knb, cos, sin):
    B, S, _ = x.shape
    tm = PROJ_TILE
    grid = (B, S // tm)
    tok = lambda w: pl.BlockSpec((1, tm, w), lambda b, i: (b, i, 0))
    out_widths = [NA_WIDTH, NA_WIDTH, NA_WIDTH, NA_WIDTH, SW_WIDTH, SW_WIDTH, SW_WIDTH, SW_WIDTH,
                  D_MODEL, D_MODEL]
    return pl.pallas_call(
        _proj_kernel,
        grid=grid,
        in_specs=[
            tok(D_MODEL),
            _const_spec((1, D_MODEL)),
            pl.BlockSpec((D_MODEL, IN_WIDTH), lambda b, i: (0, 0), pipeline_mode=pl.Buffered(1)),
            _const_spec((MXU_DIM, MXU_DIM)),
            _const_spec((1, NA_WIDTH)),
            _const_spec((1, NA_WIDTH)),
            _const_spec((1, SW_WIDTH)),
            _const_spec((1, SW_KV_WIDTH)),
            pl.BlockSpec((tm, LANES), lambda b, i: (i, 0)),
            pl.BlockSpec((tm, LANES), lambda b, i: (i, 0)),
        ],
        out_specs=[tok(w) for w in out_widths],
        out_shape=[jax.ShapeDtypeStruct((B, S, w), jnp.bfloat16) for w in out_widths],
        compiler_params=pltpu.CompilerParams(
            dimension_semantics=("arbitrary", "arbitrary"), vmem_limit_bytes=VMEM_LIMIT),
        name="proj",
    )(x, g, w_bf, bd, qna, kna, qnb, knb, cos, sin)


def _stack_heads(q, hm_ref):
    rows = q.shape[0]
    return jnp.concatenate([q * hm_ref[hh, :rows, :] for hh in range(HEADS_PER_MXU)], axis=0)


def _take_diag(r, l, rows):
    lane = lax.broadcasted_iota(jnp.int32, (rows, MXU_DIM), 1) // HEAD_DIM
    last = HEADS_PER_MXU - 1
    num = r[last * rows:]
    den = jnp.broadcast_to(l[last * rows:], (rows, MXU_DIM))
    for hh in range(last - 1, -1, -1):
        num = jnp.where(lane == hh, r[hh * rows:(hh + 1) * rows], num)
        den = jnp.where(lane == hh, l[hh * rows:(hh + 1) * rows], den)
    return num / den


def _softmax_rows(s, add, floor, shift):
    n = s.shape[1] // LANES
    sb = [s[:, m * LANES:(m + 1) * LANES] + add(m) for m in range(n)]
    if shift:
        mx = jnp.max(functools.reduce(jnp.maximum, sb), axis=-1, keepdims=True)
        if floor is not None:
            mx = jnp.maximum(mx, floor)
            floor = floor - mx
        sb = [t - mx for t in sb]
    p = [jnp.exp2(t) for t in sb]
    l = jnp.sum(functools.reduce(jnp.add, p), axis=-1, keepdims=True)
    if floor is not None:
        l = l + jnp.exp2(floor)
    return jnp.concatenate(p, axis=1).astype(jnp.bfloat16), l


def _attn_kernel(bounded_ref, qa_ref, sza_ref, qb_ref, szb_ref, ka_ref, va_ref, kb_ref, vb_ref,
                 pb_ref, swm_ref, sink_ref, hm_ref, ya_ref, yb_ref, *, seq_len):
    f32, bf16 = jnp.float32, jnp.bfloat16
    i = pl.program_id(1)
    nt = (((1,), (1,)), ((), ()))
    rows_total = seq_len // GRID_W
    kh = min(NA_KH, rows_total)
    win = kh * GRID_W
    na_start = _window_start(i, NA_HALO, NA_EXT, seq_len)
    sw_start = _window_start(i, WINDOW, SW_EXT, seq_len)

    def na_row(rl, shift):
        r = i * ATT_ROWS + rl
        krow0 = jnp.clip(r - kh // 2, 0, rows_total - kh)
        off = pl.multiple_of(krow0 * GRID_W - na_start, GRID_W)
        d0 = krow0 - r + (NA_KH - 1)
        qrows = pl.ds(pl.multiple_of(rl * GRID_W, GRID_W), GRID_W)
        for g in range(NA_HEADS // HEADS_PER_MXU):
            cols = slice(g * MXU_DIM, (g + 1) * MXU_DIM)
            qs = _stack_heads(qa_ref[0, qrows, cols], hm_ref)
            s = lax.dot_general(qs, ka_ref[pl.ds(off, win), cols], nt,
                                preferred_element_type=f32)
            p, l = _softmax_rows(
                s,
                lambda m: jnp.concatenate([pb_ref[g * HEADS_PER_MXU + hh, d0 + 2 * m]
                                           for hh in range(HEADS_PER_MXU)], axis=0),
                None, shift)
            rr = jnp.dot(p, va_ref[pl.ds(off, win), cols], preferred_element_type=f32)
            o = _take_diag(rr, l, GRID_W)
            ya_ref[0, qrows, cols] = (o * sza_ref[0, qrows, cols].astype(f32)).astype(bf16)

    def sw_block(nl, shift):
        q0 = i * ATT_TILE + nl * SW_BLOCK
        w0 = jnp.clip(q0 - WINDOW, 0, seq_len - SW_KWIN)
        var = (q0 - w0) // SW_BLOCK
        qrows = pl.ds(pl.multiple_of(nl * SW_BLOCK, SW_BLOCK), SW_BLOCK)
        krows = pl.ds(pl.multiple_of(w0 - sw_start, SW_BLOCK), SW_KWIN)
        for j in range(SW_KV_HEADS):
            cols = slice(j * MXU_DIM, (j + 1) * MXU_DIM)
            qs = _stack_heads(qb_ref[0, qrows, cols], hm_ref)
            s = lax.dot_general(qs, kb_ref[krows, cols], nt, preferred_element_type=f32)
            p, l = _softmax_rows(
                s,
                lambda m: jnp.concatenate([swm_ref[var, :, m * LANES:(m + 1) * LANES]] * SW_GROUP, axis=0),
                sink_ref[j], shift)
            rr = jnp.dot(p, vb_ref[krows, cols], preferred_element_type=f32)
            o = _take_diag(rr, l, SW_BLOCK)
            yb_ref[0, qrows, cols] = (o * szb_ref[0, qrows, cols].astype(f32)).astype(bf16)

    @pl.when(bounded_ref[0] != 0)
    def _():
        for rl in range(ATT_TILE // GRID_W):
            na_row(rl, False)
        for nl in range(ATT_TILE // SW_BLOCK):
            sw_block(nl, False)

    @pl.when(bounded_ref[0] == 0)
    def _():
        def na_body(rl, carry):
            na_row(rl, True)
            return carry

        def sw_body(nl, carry):
            sw_block(nl, True)
            return carry

        lax.fori_loop(0, ATT_TILE // GRID_W, na_body, 0)
        lax.fori_loop(0, ATT_TILE // SW_BLOCK, sw_body, 0)


def _window_start(i, halo, ext, seq_len):
    return jnp.clip(i * ATT_TILE - halo, 0, seq_len - ext)


def _attention(bounded, qa, ka, va, sza, qb, kb, vb, szb, pb, swm, sink_col, hm):
    B, S, _ = qa.shape
    tq = ATT_TILE
    grid = (B, S // tq)

    tok = lambda w: pl.BlockSpec((1, tq, w), lambda b, i: (b, i, 0))

    def window(width, halo, ext):
        return pl.BlockSpec(
            (pl.Element(ext), pl.Element(width)),
            lambda b, i: (pl.multiple_of(b * S + _window_start(i, halo, ext, S), SW_BLOCK), 0))

    flat = lambda t: t.reshape(B * S, t.shape[-1])
    kern = functools.partial(_attn_kernel, seq_len=S)
    return pl.pallas_call(
        kern,
        grid=grid,
        in_specs=[
            pl.BlockSpec(memory_space=pltpu.SMEM),
            tok(NA_WIDTH), tok(NA_WIDTH), tok(SW_WIDTH), tok(SW_WIDTH),
            window(NA_WIDTH, NA_HALO, NA_EXT), window(NA_WIDTH, NA_HALO, NA_EXT),
            window(SW_WIDTH, WINDOW, SW_EXT), window(SW_WIDTH, WINDOW, SW_EXT),
            _const_spec(pb.shape), _const_spec(swm.shape), _const_spec(sink_col.shape),
            _const_spec(hm.shape),
        ],
        out_specs=[tok(NA_WIDTH), tok(SW_WIDTH)],
        out_shape=[jax.ShapeDtypeStruct((B, S, NA_WIDTH), jnp.bfloat16),
                   jax.ShapeDtypeStruct((B, S, SW_WIDTH), jnp.bfloat16)],
        compiler_params=pltpu.CompilerParams(
            dimension_semantics=("arbitrary", "arbitrary"), vmem_limit_bytes=VMEM_LIMIT),
        name="attn",
    )(bounded, qa, sza, qb, szb, flat(ka), flat(va), flat(kb), flat(vb), pb, swm, sink_col, hm)


def _merge_kernel(ya_ref, yb_ref, sga_ref, sgb_ref, x_ref, woa_ref, wob_ref, wo_ref, y_ref):
    f32 = jnp.float32
    pa = jnp.dot(ya_ref[0], woa_ref[...], preferred_element_type=f32)
    pbm = jnp.dot(yb_ref[0], wob_ref[...], preferred_element_type=f32)
    merged = (sga_ref[0].astype(f32) * pa + sgb_ref[0].astype(f32) * pbm).astype(jnp.bfloat16)
    y_ref[0] = x_ref[0] + jnp.dot(merged, wo_ref[...], preferred_element_type=f32)


def _merge(x, ya, yb, sga, sgb, woa, wob, wo):
    B, S, _ = x.shape
    tm = MERGE_TILE
    tok = lambda w: pl.BlockSpec((1, tm, w), lambda b, i: (b, i, 0))
    return pl.pallas_call(
        _merge_kernel,
        grid=(B, S // tm),
        in_specs=[tok(NA_WIDTH), tok(SW_WIDTH), tok(D_MODEL), tok(D_MODEL), tok(D_MODEL),
                  _const_spec(woa.shape), _const_spec(wob.shape), _const_spec(wo.shape)],
        out_specs=tok(D_MODEL),
        out_shape=jax.ShapeDtypeStruct((B, S, D_MODEL), jnp.float32),
        compiler_params=pltpu.CompilerParams(
            dimension_semantics=("arbitrary", "arbitrary"), vmem_limit_bytes=VMEM_LIMIT),
        name="merge",
    )(ya, yb, sga, sgb, x, woa, wob, wo)


def _na_bias_table(rpb):
    c = np.arange(GRID_W)
    cs = np.clip(c - NA_KW // 2, 0, GRID_W - NA_KW)
    col_in = (c[None, :] >= cs[:, None]) & (c[None, :] < cs[:, None] + NA_KW)
    nt = 2 * NA_KW - 1
    t_idx = c[None, :] - c[:, None] + (NA_KW - 1)
    onehot = (t_idx[None] == np.arange(nt)[:, None, None]) & col_in[None]
    onehot = jnp.asarray(onehot.reshape(nt, GRID_W * GRID_W), dtype=jnp.float32)
    flat = (rpb.astype(jnp.float32) * LOG2_E).reshape(NA_HEADS * (2 * NA_KH - 1), nt)
    b = jnp.dot(flat, onehot, precision=lax.Precision.HIGHEST)
    b = b.reshape(NA_HEADS, 2 * NA_KH - 1, GRID_W, GRID_W)
    b = jnp.where(col_in[None, None], b, NEG_BIG)
    nd = 2 * NA_KH - 2
    return jnp.concatenate([b[:, :nd], b[:, 1:nd + 1]], axis=-1)


def _sw_mask_table():
    qi = np.arange(SW_BLOCK)[None, :, None]
    kj = np.arange(SW_KWIN)[None, None, :]
    d = np.arange(SW_SHIFTS)[:, None, None]
    band = np.abs(kj - d * SW_BLOCK - qi) <= WINDOW
    return jnp.asarray(np.where(band, 0.0, NEG_BIG), dtype=jnp.float32)


def _layer(x, p):
    S = x.shape[1]
    outs = _projection(x, p["g"], p["w_in"], p["bd"], p["qna"], p["kna"], p["qnb"], p["knb"],
                       p["cos"][:S], p["sin"][:S])
    qa, ka, va, sza, qb, kb, vb, szb, sga, sgb = outs
    ya, yb = _attention(p["bounded"], qa, ka, va, sza, qb, kb, vb, szb, p["pb"], p["swm"], p["sink"],
                        p["hm"])
    return _merge(x, ya, yb, sga, sgb, p["woa"], p["wob"], p["wo"])


def _prepare(S, norm_g, w_in, qn_a, kn_a, rpb_a, qn_b, kn_b, sink_b, w_out_a, w_out_b, w_o):
    f32, bf16 = jnp.float32, jnp.bfloat16
    seg = np.arange(MXU_DIM) // HEAD_DIM
    bd = jnp.asarray(np.where(seg[:, None] == seg[None, :], 1.0 / HEAD_DIM, 0.0), dtype=bf16)
    hm = seg[None, None, :] == np.arange(HEADS_PER_MXU)[:, None, None]
    hm = jnp.asarray(np.broadcast_to(hm, (HEADS_PER_MXU, max(SW_BLOCK, GRID_W), MXU_DIM)), dtype=bf16)
    half = HEAD_DIM // 2
    inv = (np.float32(ROPE_THETA) ** (-np.arange(half, dtype=np.float32) / np.float32(half))).astype(np.float32)
    ang = np.arange(S, dtype=np.float32)[:, None] * inv[None, :]
    cos, sin = np.cos(ang), np.sin(ang)
    cos = jnp.asarray(np.tile(np.concatenate([cos, cos], axis=1), (1, LANES // HEAD_DIM)), dtype=f32)
    sin = jnp.asarray(np.tile(np.concatenate([-sin, sin], axis=1), (1, LANES // HEAD_DIM)), dtype=f32)
    sink_col = jnp.repeat((sink_b.astype(f32) * LOG2_E).reshape(SW_KV_HEADS, SW_GROUP), SW_BLOCK, axis=1)
    gains = jnp.stack([qn_a, kn_a, qn_b, kn_b]).astype(f32)
    gmax = jnp.max(jnp.abs(gains), axis=1)
    amax = lambda t: jnp.max(jnp.abs(t.astype(f32)))
    qk_bound = HEAD_DIM ** 0.5 * LOG2_E * (1.0 + 2.0 ** -6)
    bound_a = qk_bound * gmax[0] * gmax[1] + LOG2_E * amax(rpb_a)
    bound_b = jnp.maximum(qk_bound * gmax[2] * gmax[3], LOG2_E * amax(sink_b))
    bounded = (jnp.maximum(bound_a, bound_b) <= MAX_UNSHIFTED_LOGIT).astype(jnp.int32).reshape(1)
    gains = jnp.tile(gains, (1, NA_HEADS))
    return {
        "bounded": bounded,
        "g": norm_g.astype(f32).reshape(1, D_MODEL),
        "w_in": w_in.astype(f32),
        "bd": bd,
        "qna": gains[0:1], "kna": gains[1:2], "qnb": gains[2:3], "knb": gains[3:4, :SW_KV_WIDTH],
        "cos": cos, "sin": sin,
        "pb": _na_bias_table(rpb_a),
        "swm": _sw_mask_table(),
        "sink": sink_col.reshape(SW_KV_HEADS, SW_GROUP * SW_BLOCK, 1),
        "hm": hm,
        "woa": w_out_a.astype(bf16), "wob": w_out_b.astype(bf16), "wo": w_o.astype(bf16),
    }


def kernel(x_prompt, x_sample, norm_g, w_in, qn_a, kn_a, rpb_a, qn_b, kn_b, sink_b, w_out_a, w_out_b, w_o):
    depth = norm_g.shape[0]
    y_prompt, y_sample = x_prompt, x_sample
    S = max(x_prompt.shape[1], x_sample.shape[1])
    for l in range(depth):
        p = _prepare(S, norm_g[l], w_in[l], qn_a[l], kn_a[l], rpb_a[l], qn_b[l], kn_b[l], sink_b[l],
                     w_out_a[l], w_out_b[l], w_o[l])
        y_prompt = _layer(y_prompt, p)
        y_sample = _layer(y_sample, p)
    return (y_prompt, y_sample)
```

```python
import functools

import jax
import jax.numpy as jnp
import numpy as np
from jax import lax
from jax.experimental import pallas as pl
from jax.experimental.pallas import tpu as pltpu

D_MODEL = 1024
HEAD_DIM = 64
GRID_W = 64
NA_HEADS = 8
NA_KH = 8
NA_KW = 16
NA_WIDTH = NA_HEADS * HEAD_DIM
SW_HEADS = 8
SW_KV_HEADS = 2
SW_GROUP = SW_HEADS // SW_KV_HEADS
SW_WIDTH = SW_HEADS * HEAD_DIM
SW_KV_WIDTH = SW_KV_HEADS * HEAD_DIM
WINDOW = 128
ROPE_THETA = 10000.0
NORM_EPS = 1e-6

C_QA = 0
C_KA = C_QA + NA_WIDTH
C_VA = C_KA + NA_WIDTH
C_ZA = C_VA + NA_WIDTH
C_QB = C_ZA + NA_WIDTH
C_KB = C_QB + SW_WIDTH
C_VB = C_KB + SW_KV_WIDTH
C_ZB = C_VB + SW_KV_WIDTH
C_GA = C_ZB + SW_WIDTH
C_GB = C_GA + D_MODEL
IN_WIDTH = C_GB + D_MODEL

LANES = 128
MXU_DIM = 256
HEADS_PER_MXU = MXU_DIM // HEAD_DIM
NEG_BIG = -1e30
LOG2_E = 1.4426950408889634
MAX_UNSHIFTED_LOGIT = 32.0
VMEM_LIMIT = 56 * 1024 * 1024

PROJ_TILE = 512
ATT_TILE = 1024
ATT_ROWS = ATT_TILE // GRID_W
MERGE_TILE = 1024
NA_HALO = 256
NA_EXT = ATT_TILE + 2 * NA_HALO
SW_BLOCK = 64
SW_KWIN = 384
SW_EXT = ATT_TILE + SW_KWIN - SW_BLOCK
SW_SHIFTS = (SW_KWIN - SW_BLOCK) // SW_BLOCK + 1


def _segment_mean_sq(t, bd):
    w = t.shape[1]
    sq = (t * t).astype(jnp.bfloat16)
    outs = []
    for c in range(0, w, MXU_DIM):
        cw = min(MXU_DIM, w - c)
        outs.append(jnp.dot(sq[:, c:c + cw], bd[:cw, :cw], preferred_element_type=jnp.float32))
    return outs[0] if len(outs) == 1 else jnp.concatenate(outs, axis=1)


def _swap_halves(t):
    w = t.shape[1]
    lane = lax.broadcasted_iota(jnp.int32, (t.shape[0], LANES), 1)
    low = (lane % HEAD_DIM) < (HEAD_DIM // 2)
    outs = []
    for c in range(0, w, LANES):
        blk = t[:, c:c + LANES]
        fwd = pltpu.roll(blk, HEAD_DIM // 2, 1)
        bwd = pltpu.roll(blk, LANES - HEAD_DIM // 2, 1)
        outs.append(jnp.where(low, bwd, fwd))
    return outs[0] if len(outs) == 1 else jnp.concatenate(outs, axis=1)


def _tile_lanes(t, reps):
    return t if reps == 1 else jnp.concatenate([t] * reps, axis=1)


def _proj_kernel(x_ref, g_ref, w_ref, bd_ref, qna_ref, kna_ref, qnb_ref, knb_ref, cos_ref, sin_ref,
                 qa_o, ka_o, va_o, sza_o, qb_o, kb_o, vb_o, szb_o, sga_o, sgb_o):
    f32, bf16 = jnp.float32, jnp.bfloat16
    x = x_ref[0]
    ms = jnp.mean(x * x, axis=-1, keepdims=True)
    h = (x * lax.rsqrt(ms + NORM_EPS) * g_ref[...]).astype(bf16)
    bd = bd_ref[...]
    scale = HEAD_DIM ** -0.5 * LOG2_E

    def proj(c0, c1):
        return jnp.dot(h, w_ref[:, c0:c1], preferred_element_type=f32)

    def head_norm(t, gain):
        return t * lax.rsqrt(_segment_mean_sq(t, bd) + NORM_EPS) * gain

    def rotary(t):
        reps = t.shape[1] // LANES
        cos = _tile_lanes(cos_ref[...], reps)
        sin = _tile_lanes(sin_ref[...], reps)
        return t * cos + _swap_halves(t) * sin

    qa_o[0] = (head_norm(proj(C_QA, C_KA), qna_ref[...]) * scale).astype(bf16)
    ka_o[0] = head_norm(proj(C_KA, C_VA), kna_ref[...]).astype(bf16)
    va_o[0] = proj(C_VA, C_ZA).astype(bf16)
    sza_o[0] = jax.nn.silu(proj(C_ZA, C_QB)).astype(bf16)
    qb_o[0] = (rotary(head_norm(proj(C_QB, C_KB), qnb_ref[...])) * scale).astype(bf16)

    kvb = proj(C_KB, C_ZB)
    kb = rotary(head_norm(kvb[:, :SW_KV_WIDTH], knb_ref[...]))
    vb = kvb[:, SW_KV_WIDTH:]
    lane = lax.broadcasted_iota(jnp.int32, kb.shape, 1)
    first = lane < HEAD_DIM

    def replicate(t):
        r = pltpu.roll(t, HEAD_DIM, 1)
        h0 = jnp.where(first, t, r)
        h1 = jnp.where(first, r, t)
        return jnp.concatenate([h0, h0, h1, h1], axis=1)

    kb_o[0] = replicate(kb).astype(bf16)
    vb_o[0] = replicate(vb).astype(bf16)
    szb_o[0] = jax.nn.silu(proj(C_ZB, C_GA)).astype(bf16)
    sga_o[0] = jax.nn.sigmoid(proj(C_GA, C_GB)).astype(bf16)
    sgb_o[0] = jax.nn.sigmoid(proj(C_GB, IN_WIDTH)).astype(bf16)


def _const_spec(shape):
    nd = len(shape)
    return pl.BlockSpec(shape, lambda b, i: (0,) * nd)


def _projection(x, g, w_bf, bd, qna, kna, qnb, knb, cos, sin):
    B, S, _ = x.shape
    tm = PROJ_TILE
    grid = (B, S // tm)
    tok = lambda w: pl.BlockSpec((1, tm, w), lambda b, i: (b, i, 0))
    out_widths = [NA_WIDTH, NA_WIDTH, NA_WIDTH, NA_WIDTH, SW_WIDTH, SW_WIDTH, SW_WIDTH, SW_WIDTH,
                  D_MODEL, D_MODEL]
    return pl.pallas_call(
        _proj_kernel,
        grid=grid,
        in_specs=[
            tok(D_MODEL),
            _const_spec((1, D_MODEL)),
            _const_spec((D_MODEL, IN_WIDTH)),
            _const_spec((MXU_DIM, MXU_DIM)),
            _const_spec((1, NA_WIDTH)),
            _const_spec((1, NA_WIDTH)),
            _const_spec((1, SW_WIDTH)),
            _const_spec((1, SW_KV_WIDTH)),
            pl.BlockSpec((tm, LANES), lambda b, i: (i, 0)),
            pl.BlockSpec((tm, LANES), lambda b, i: (i, 0)),
        ],
        out_specs=[tok(w) for w in out_widths],
        out_shape=[jax.ShapeDtypeStruct((B, S, w), jnp.bfloat16) for w in out_widths],
        compiler_params=pltpu.CompilerParams(
            dimension_semantics=("arbitrary", "arbitrary"), vmem_limit_bytes=VMEM_LIMIT),
        name="proj",
    )(x, g, w_bf, bd, qna, kna, qnb, knb, cos, sin)


def _stack_heads(q, hm_ref):
    rows = q.shape[0]
    return jnp.concatenate([q * hm_ref[hh, :rows, :] for hh in range(HEADS_PER_MXU)], axis=0)


def _take_diag(r, l, rows):
    lane = lax.broadcasted_iota(jnp.int32, (rows, MXU_DIM), 1) // HEAD_DIM
    last = HEADS_PER_MXU - 1
    num = r[last * rows:]
    den = jnp.broadcast_to(l[last * rows:], (rows, MXU_DIM))
    for hh in range(last - 1, -1, -1):
        num = jnp.where(lane == hh, r[hh * rows:(hh + 1) * rows], num)
        den = jnp.where(lane == hh, l[hh * rows:(hh + 1) * rows], den)
    return num / den


def _softmax_rows(s, add, floor, shift):
    n = s.shape[1] // LANES
    sb = [s[:, m * LANES:(m + 1) * LANES] + add(m) for m in range(n)]
    if shift:
        mx = jnp.max(functools.reduce(jnp.maximum, sb), axis=-1, keepdims=True)
        if floor is not None:
            mx = jnp.maximum(mx, floor)
            floor = floor - mx
        sb = [t - mx for t in sb]
    p = [jnp.exp2(t) for t in sb]
    l = jnp.sum(functools.reduce(jnp.add, p), axis=-1, keepdims=True)
    if floor is not None:
        l = l + jnp.exp2(floor)
    return jnp.concatenate(p, axis=1).astype(jnp.bfloat16), l


def _attn_kernel(bounded_ref, qa_ref, sza_ref, qb_ref, szb_ref, ka_ref, va_ref, kb_ref, vb_ref,
                 pb_ref, swm_ref, sink_ref, hm_ref, ya_ref, yb_ref, *, seq_len):
    f32, bf16 = jnp.float32, jnp.bfloat16
    i = pl.program_id(1)
    nt = (((1,), (1,)), ((), ()))
    rows_total = seq_len // GRID_W
    kh = min(NA_KH, rows_total)
    win = kh * GRID_W
    na_start = _window_start(i, NA_HALO, NA_EXT, seq_len)
    sw_start = _window_start(i, WINDOW, SW_EXT, seq_len)

    def na_row(rl, shift):
        r = i * ATT_ROWS + rl
        krow0 = jnp.clip(r - kh // 2, 0, rows_total - kh)
        off = pl.multiple_of(krow0 * GRID_W - na_start, GRID_W)
        d0 = krow0 - r + (NA_KH - 1)
        qrows = pl.ds(pl.multiple_of(rl * GRID_W, GRID_W), GRID_W)
        for g in range(NA_HEADS // HEADS_PER_MXU):
            cols = slice(g * MXU_DIM, (g + 1) * MXU_DIM)
            qs = _stack_heads(qa_ref[0, qrows, cols], hm_ref)
            s = lax.dot_general(qs, ka_ref[pl.ds(off, win), cols], nt,
                                preferred_element_type=f32)
            p, l = _softmax_rows(
                s,
                lambda m: jnp.concatenate([pb_ref[g * HEADS_PER_MXU + hh, d0 + 2 * m]
                                           for hh in range(HEADS_PER_MXU)], axis=0),
                None, shift)
            rr = jnp.dot(p, va_ref[pl.ds(off, win), cols], preferred_element_type=f32)
            o = _take_diag(rr, l, GRID_W)
            ya_ref[0, qrows, cols] = (o * sza_ref[0, qrows, cols].astype(f32)).astype(bf16)

    def sw_block(nl, shift):
        q0 = i * ATT_TILE + nl * SW_BLOCK
        w0 = jnp.clip(q0 - WINDOW, 0, seq_len - SW_KWIN)
        var = (q0 - w0) // SW_BLOCK
        qrows = pl.ds(pl.multiple_of(nl * SW_BLOCK, SW_BLOCK), SW_BLOCK)
        krows = pl.ds(pl.multiple_of(w0 - sw_start, SW_BLOCK), SW_KWIN)
        for j in range(SW_KV_HEADS):
            cols = slice(j * MXU_DIM, (j + 1) * MXU_DIM)
            qs = _stack_heads(qb_ref[0, qrows, cols], hm_ref)
            s = lax.dot_general(qs, kb_ref[krows, cols], nt, preferred_element_type=f32)
            p, l = _softmax_rows(
                s,
                lambda m: jnp.concatenate([swm_ref[var, :, m * LANES:(m + 1) * LANES]] * SW_GROUP, axis=0),
                sink_ref[j], shift)
            rr = jnp.dot(p, vb_ref[krows, cols], preferred_element_type=f32)
            o = _take_diag(rr, l, SW_BLOCK)
            yb_ref[0, qrows, cols] = (o * szb_ref[0, qrows, cols].astype(f32)).astype(bf16)

    @pl.when(bounded_ref[0] != 0)
    def _():
        for rl in range(ATT_TILE // GRID_W):
            na_row(rl, False)
        for nl in range(ATT_TILE // SW_BLOCK):
            sw_block(nl, False)

    @pl.when(bounded_ref[0] == 0)
    def _():
        def na_body(rl, carry):
            na_row(rl, True)
            return carry

        def sw_body(nl, carry):
            sw_block(nl, True)
            return carry

        lax.fori_loop(0, ATT_TILE // GRID_W, na_body, 0)
        lax.fori_loop(0, ATT_TILE // SW_BLOCK, sw_body, 0)


def _window_start(i, halo, ext, seq_len):
    return jnp.clip(i * ATT_TILE - halo, 0, seq_len - ext)


def _attention(bounded, qa, ka, va, sza, qb, kb, vb, szb, pb, swm, sink_col, hm):
    B, S, _ = qa.shape
    tq = ATT_TILE
    grid = (B, S // tq)

    tok = lambda w: pl.BlockSpec((1, tq, w), lambda b, i: (b, i, 0))

    def window(width, halo, ext):
        return pl.BlockSpec(
            (pl.Element(ext), pl.Element(width)),
            lambda b, i: (pl.multiple_of(b * S + _window_start(i, halo, ext, S), SW_BLOCK), 0))

    flat = lambda t: t.reshape(B * S, t.shape[-1])
    kern = functools.partial(_attn_kernel, seq_len=S)
    return pl.pallas_call(
        kern,
        grid=grid,
        in_specs=[
            pl.BlockSpec(memory_space=pltpu.SMEM),
            tok(NA_WIDTH), tok(NA_WIDTH), tok(SW_WIDTH), tok(SW_WIDTH),
            window(NA_WIDTH, NA_HALO, NA_EXT), window(NA_WIDTH, NA_HALO, NA_EXT),
            window(SW_WIDTH, WINDOW, SW_EXT), window(SW_WIDTH, WINDOW, SW_EXT),
            _const_spec(pb.shape), _const_spec(swm.shape), _const_spec(sink_col.shape),
            _const_spec(hm.shape),
        ],
        out_specs=[tok(NA_WIDTH), tok(SW_WIDTH)],
        out_shape=[jax.ShapeDtypeStruct((B, S, NA_WIDTH), jnp.bfloat16),
                   jax.ShapeDtypeStruct((B, S, SW_WIDTH), jnp.bfloat16)],
        compiler_params=pltpu.CompilerParams(
            dimension_semantics=("arbitrary", "arbitrary"), vmem_limit_bytes=VMEM_LIMIT),
        name="attn",
    )(bounded, qa, sza, qb, szb, flat(ka), flat(va), flat(kb), flat(vb), pb, swm, sink_col, hm)


def _merge_kernel(ya_ref, yb_ref, sga_ref, sgb_ref, x_ref, woa_ref, wob_ref, wo_ref, y_ref):
    f32 = jnp.float32
    pa = jnp.dot(ya_ref[0], woa_ref[...], preferred_element_type=f32)
    pbm = jnp.dot(yb_ref[0], wob_ref[...], preferred_element_type=f32)
    merged = (sga_ref[0].astype(f32) * pa + sgb_ref[0].astype(f32) * pbm).astype(jnp.bfloat16)
    y_ref[0] = x_ref[0] + jnp.dot(merged, wo_ref[...], preferred_element_type=f32)


def _merge(x, ya, yb, sga, sgb, woa, wob, wo):
    B, S, _ = x.shape
    tm = MERGE_TILE
    tok = lambda w: pl.BlockSpec((1, tm, w), lambda b, i: (b, i, 0))
    return pl.pallas_call(
        _merge_kernel,
        grid=(B, S // tm),
        in_specs=[tok(NA_WIDTH), tok(SW_WIDTH), tok(D_MODEL), tok(D_MODEL), tok(D_MODEL),
                  _const_spec(woa.shape), _const_spec(wob.shape), _const_spec(wo.shape)],
        out_specs=tok(D_MODEL),
        out_shape=jax.ShapeDtypeStruct((B, S, D_MODEL), jnp.float32),
        compiler_params=pltpu.CompilerParams(
            dimension_semantics=("arbitrary", "arbitrary"), vmem_limit_bytes=VMEM_LIMIT),
        name="merge",
    )(ya, yb, sga, sgb, x, woa, wob, wo)


def _na_bias_table(rpb):
    c = np.arange(GRID_W)
    cs = np.clip(c - NA_KW // 2, 0, GRID_W - NA_KW)
    col_in = (c[None, :] >= cs[:, None]) & (c[None, :] < cs[:, None] + NA_KW)
    nt = 2 * NA_KW - 1
    t_idx = c[None, :] - c[:, None] + (NA_KW - 1)
    onehot = (t_idx[None] == np.arange(nt)[:, None, None]) & col_in[None]
    onehot = jnp.asarray(onehot.reshape(nt, GRID_W * GRID_W), dtype=jnp.float32)
    flat = (rpb.astype(jnp.float32) * LOG2_E).reshape(NA_HEADS * (2 * NA_KH - 1), nt)
    b = jnp.dot(flat, onehot, precision=lax.Precision.HIGHEST)
    b = b.reshape(NA_HEADS, 2 * NA_KH - 1, GRID_W, GRID_W)
    b = jnp.where(col_in[None, None], b, NEG_BIG)
    nd = 2 * NA_KH - 2
    return jnp.concatenate([b[:, :nd], b[:, 1:nd + 1]], axis=-1)


def _sw_mask_table():
    qi = np.arange(SW_BLOCK)[None, :, None]
    kj = np.arange(SW_KWIN)[None, None, :]
    d = np.arange(SW_SHIFTS)[:, None, None]
    band = np.abs(kj - d * SW_BLOCK - qi) <= WINDOW
    return jnp.asarray(np.where(band, 0.0, NEG_BIG), dtype=jnp.float32)


def _layer(xs, p):
    projected = [_projection(x, p["g"], p["w_in"], p["bd"], p["qna"], p["kna"], p["qnb"], p["knb"],
                             p["cos"][:x.shape[1]], p["sin"][:x.shape[1]]) for x in xs]
    attended = [_attention(p["bounded"], qa, ka, va, sza, qb, kb, vb, szb, p["pb"], p["swm"], p["sink"],
                           p["hm"])
                for qa, ka, va, sza, qb, kb, vb, szb, _, _ in projected]
    return [_merge(x, ya, yb, outs[8], outs[9], p["woa"], p["wob"], p["wo"])
            for x, (ya, yb), outs in zip(xs, attended, projected)]


def _prepare(S, norm_g, w_in, qn_a, kn_a, rpb_a, qn_b, kn_b, sink_b, w_out_a, w_out_b, w_o):
    f32, bf16 = jnp.float32, jnp.bfloat16
    seg = np.arange(MXU_DIM) // HEAD_DIM
    bd = jnp.asarray(np.where(seg[:, None] == seg[None, :], 1.0 / HEAD_DIM, 0.0), dtype=bf16)
    hm = seg[None, None, :] == np.arange(HEADS_PER_MXU)[:, None, None]
    hm = jnp.asarray(np.broadcast_to(hm, (HEADS_PER_MXU, max(SW_BLOCK, GRID_W), MXU_DIM)), dtype=bf16)
    half = HEAD_DIM // 2
    inv = (np.float32(ROPE_THETA) ** (-np.arange(half, dtype=np.float32) / np.float32(half))).astype(np.float32)
    ang = np.arange(S, dtype=np.float32)[:, None] * inv[None, :]
    cos, sin = np.cos(ang), np.sin(ang)
    cos = jnp.asarray(np.tile(np.concatenate([cos, cos], axis=1), (1, LANES // HEAD_DIM)), dtype=f32)
    sin = jnp.asarray(np.tile(np.concatenate([-sin, sin], axis=1), (1, LANES // HEAD_DIM)), dtype=f32)
    sink_col = jnp.repeat((sink_b.astype(f32) * LOG2_E).reshape(SW_KV_HEADS, SW_GROUP), SW_BLOCK, axis=1)
    gains = jnp.stack([qn_a, kn_a, qn_b, kn_b]).astype(f32)
    gmax = jnp.max(jnp.abs(gains), axis=1)
    amax = lambda t: jnp.max(jnp.abs(t.astype(f32)))
    qk_bound = HEAD_DIM ** 0.5 * LOG2_E * (1.0 + 2.0 ** -6)
    bound_a = qk_bound * gmax[0] * gmax[1] + LOG2_E * amax(rpb_a)
    bound_b = jnp.maximum(qk_bound * gmax[2] * gmax[3], LOG2_E * amax(sink_b))
    bounded = (jnp.maximum(bound_a, bound_b) <= MAX_UNSHIFTED_LOGIT).astype(jnp.int32).reshape(1)
    gains = jnp.tile(gains, (1, NA_HEADS))
    return {
        "bounded": bounded,
        "g": norm_g.astype(f32).reshape(1, D_MODEL),
        "w_in": w_in.astype(bf16),
        "bd": bd,
        "qna": gains[0:1], "kna": gains[1:2], "qnb": gains[2:3], "knb": gains[3:4, :SW_KV_WIDTH],
        "cos": cos, "sin": sin,
        "pb": _na_bias_table(rpb_a),
        "swm": _sw_mask_table(),
        "sink": sink_col.reshape(SW_KV_HEADS, SW_GROUP * SW_BLOCK, 1),
        "hm": hm,
        "woa": w_out_a.astype(bf16), "wob": w_out_b.astype(bf16), "wo": w_o.astype(bf16),
    }


def kernel(x_prompt, x_sample, norm_g, w_in, qn_a, kn_a, rpb_a, qn_b, kn_b, sink_b, w_out_a, w_out_b, w_o):
    depth = norm_g.shape[0]
    y_prompt, y_sample = x_prompt, x_sample
    S = max(x_prompt.shape[1], x_sample.shape[1])
    for l in range(depth):
        p = _prepare(S, norm_g[l], w_in[l], qn_a[l], kn_a[l], rpb_a[l], qn_b[l], kn_b[l], sink_b[l],
                     w_out_a[l], w_out_b[l], w_o[l])
        y_prompt, y_sample = _layer([y_prompt, y_sample], p)
    return (y_prompt, y_sample)
```

```python
import functools

import jax
import jax.numpy as jnp
import numpy as np
from jax import lax
from jax.experimental import pallas as pl
from jax.experimental.pallas import tpu as pltpu

D_MODEL = 1024
HEAD_DIM = 64
GRID_W = 64
NA_HEADS = 8
NA_KH = 8
NA_KW = 16
NA_WIDTH = NA_HEADS * HEAD_DIM
SW_HEADS = 8
SW_KV_HEADS = 2
SW_GROUP = SW_HEADS // SW_KV_HEADS
SW_WIDTH = SW_HEADS * HEAD_DIM
SW_KV_WIDTH = SW_KV_HEADS * HEAD_DIM
WINDOW = 128
ROPE_THETA = 10000.0
NORM_EPS = 1e-6

C_QA = 0
C_KA = C_QA + NA_WIDTH
C_VA = C_KA + NA_WIDTH
C_ZA = C_VA + NA_WIDTH
C_QB = C_ZA + NA_WIDTH
C_KB = C_QB + SW_WIDTH
C_VB = C_KB + SW_KV_WIDTH
C_ZB = C_VB + SW_KV_WIDTH
C_GA = C_ZB + SW_WIDTH
C_GB = C_GA + D_MODEL
IN_WIDTH = C_GB + D_MODEL

LANES = 128
MXU_DIM = 256
HEADS_PER_MXU = MXU_DIM // HEAD_DIM
NEG_BIG = -1e30
LOG2_E = 1.4426950408889634
MAX_UNSHIFTED_LOGIT = 32.0
VMEM_LIMIT = 56 * 1024 * 1024

PROJ_TILE = 512
ATT_TILE = 1024
ATT_ROWS = ATT_TILE // GRID_W
MERGE_TILE = 1024
NA_HALO = 256
NA_EXT = ATT_TILE + 2 * NA_HALO
SW_BLOCK = 64
SW_KWIN = 384
SW_EXT = ATT_TILE + SW_KWIN - SW_BLOCK
SW_SHIFTS = (SW_KWIN - SW_BLOCK) // SW_BLOCK + 1


def _segment_mean_sq(t, bd):
    w = t.shape[1]
    sq = (t * t).astype(jnp.bfloat16)
    outs = []
    for c in range(0, w, MXU_DIM):
        cw = min(MXU_DIM, w - c)
        outs.append(jnp.dot(sq[:, c:c + cw], bd[:cw, :cw], preferred_element_type=jnp.float32))
    return outs[0] if len(outs) == 1 else jnp.concatenate(outs, axis=1)


def _swap_halves(t):
    w = t.shape[1]
    lane = lax.broadcasted_iota(jnp.int32, (t.shape[0], LANES), 1)
    low = (lane % HEAD_DIM) < (HEAD_DIM // 2)
    outs = []
    for c in range(0, w, LANES):
        blk = t[:, c:c + LANES]
        fwd = pltpu.roll(blk, HEAD_DIM // 2, 1)
        bwd = pltpu.roll(blk, LANES - HEAD_DIM // 2, 1)
        outs.append(jnp.where(low, bwd, fwd))
    return outs[0] if len(outs) == 1 else jnp.concatenate(outs, axis=1)


def _tile_lanes(t, reps):
    return t if reps == 1 else jnp.concatenate([t] * reps, axis=1)


def _proj_kernel(x_ref, g_ref, w_ref, bd_ref, qna_ref, kna_ref, qnb_ref, knb_ref, cos_ref, sin_ref,
                 qa_o, ka_o, va_o, sza_o, qb_o, kb_o, vb_o, szb_o):
    f32, bf16 = jnp.float32, jnp.bfloat16
    x = x_ref[0]
    ms = jnp.mean(x * x, axis=-1, keepdims=True)
    h = (x * lax.rsqrt(ms + NORM_EPS) * g_ref[...]).astype(bf16)
    bd = bd_ref[...]
    scale = HEAD_DIM ** -0.5 * LOG2_E

    def proj(c0, c1):
        return jnp.dot(h, w_ref[:, c0:c1], preferred_element_type=f32)

    def head_norm(t, gain):
        return t * lax.rsqrt(_segment_mean_sq(t, bd) + NORM_EPS) * gain

    def rotary(t):
        reps = t.shape[1] // LANES
        cos = _tile_lanes(cos_ref[...], reps)
        sin = _tile_lanes(sin_ref[...], reps)
        return t * cos + _swap_halves(t) * sin

    qa_o[0] = (head_norm(proj(C_QA, C_KA), qna_ref[...]) * scale).astype(bf16)
    ka_o[0] = head_norm(proj(C_KA, C_VA), kna_ref[...]).astype(bf16)
    va_o[0] = proj(C_VA, C_ZA).astype(bf16)
    sza_o[0] = jax.nn.silu(proj(C_ZA, C_QB)).astype(bf16)
    qb_o[0] = (rotary(head_norm(proj(C_QB, C_KB), qnb_ref[...])) * scale).astype(bf16)

    kvb = proj(C_KB, C_ZB)
    kb = rotary(head_norm(kvb[:, :SW_KV_WIDTH], knb_ref[...]))
    vb = kvb[:, SW_KV_WIDTH:]
    lane = lax.broadcasted_iota(jnp.int32, kb.shape, 1)
    first = lane < HEAD_DIM

    def replicate(t):
        r = pltpu.roll(t, HEAD_DIM, 1)
        h0 = jnp.where(first, t, r)
        h1 = jnp.where(first, r, t)
        return jnp.concatenate([h0, h0, h1, h1], axis=1)

    kb_o[0] = replicate(kb).astype(bf16)
    vb_o[0] = replicate(vb).astype(bf16)
    szb_o[0] = jax.nn.silu(proj(C_ZB, C_GA)).astype(bf16)


def _const_spec(shape):
    nd = len(shape)
    return pl.BlockSpec(shape, lambda b, i: (0,) * nd)


def _projection(x, g, w_bf, bd, qna, kna, qnb, knb, cos, sin):
    B, S, _ = x.shape
    tm = PROJ_TILE
    grid = (B, S // tm)
    tok = lambda w: pl.BlockSpec((1, tm, w), lambda b, i: (b, i, 0))
    out_widths = [NA_WIDTH, NA_WIDTH, NA_WIDTH, NA_WIDTH, SW_WIDTH, SW_WIDTH, SW_WIDTH, SW_WIDTH]
    return pl.pallas_call(
        _proj_kernel,
        grid=grid,
        in_specs=[
            tok(D_MODEL),
            _const_spec((1, D_MODEL)),
            _const_spec((D_MODEL, C_GA)),
            _const_spec((MXU_DIM, MXU_DIM)),
            _const_spec((1, NA_WIDTH)),
            _const_spec((1, NA_WIDTH)),
            _const_spec((1, SW_WIDTH)),
            _const_spec((1, SW_KV_WIDTH)),
            pl.BlockSpec((tm, LANES), lambda b, i: (i, 0)),
            pl.BlockSpec((tm, LANES), lambda b, i: (i, 0)),
        ],
        out_specs=[tok(w) for w in out_widths],
        out_shape=[jax.ShapeDtypeStruct((B, S, w), jnp.bfloat16) for w in out_widths],
        compiler_params=pltpu.CompilerParams(
            dimension_semantics=("arbitrary", "arbitrary"), vmem_limit_bytes=VMEM_LIMIT),
        name="proj",
    )(x, g, w_bf, bd, qna, kna, qnb, knb, cos, sin)


def _stack_heads(q, hm_ref):
    rows = q.shape[0]
    return jnp.concatenate([q * hm_ref[hh, :rows, :] for hh in range(HEADS_PER_MXU)], axis=0)


def _take_diag(r, l, rows):
    lane = lax.broadcasted_iota(jnp.int32, (rows, MXU_DIM), 1) // HEAD_DIM
    last = HEADS_PER_MXU - 1
    num = r[last * rows:]
    den = jnp.broadcast_to(l[last * rows:], (rows, MXU_DIM))
    for hh in range(last - 1, -1, -1):
        num = jnp.where(lane == hh, r[hh * rows:(hh + 1) * rows], num)
        den = jnp.where(lane == hh, l[hh * rows:(hh + 1) * rows], den)
    return num / den


def _softmax_rows(s, add, floor, shift):
    n = s.shape[1] // LANES
    sb = [s[:, m * LANES:(m + 1) * LANES] + add(m) for m in range(n)]
    if shift:
        mx = jnp.max(functools.reduce(jnp.maximum, sb), axis=-1, keepdims=True)
        if floor is not None:
            mx = jnp.maximum(mx, floor)
            floor = floor - mx
        sb = [t - mx for t in sb]
    p = [jnp.exp2(t) for t in sb]
    l = jnp.sum(functools.reduce(jnp.add, p), axis=-1, keepdims=True)
    if floor is not None:
        l = l + jnp.exp2(floor)
    return jnp.concatenate(p, axis=1).astype(jnp.bfloat16), l


def _attn_kernel(bounded_ref, qa_ref, sza_ref, qb_ref, szb_ref, ka_ref, va_ref, kb_ref, vb_ref,
                 pb_ref, swm_ref, sink_ref, hm_ref, ya_ref, yb_ref, *, seq_len):
    f32, bf16 = jnp.float32, jnp.bfloat16
    i = pl.program_id(1)
    nt = (((1,), (1,)), ((), ()))
    rows_total = seq_len // GRID_W
    kh = min(NA_KH, rows_total)
    win = kh * GRID_W
    na_start = _window_start(i, NA_HALO, NA_EXT, seq_len)
    sw_start = _window_start(i, WINDOW, SW_EXT, seq_len)

    def na_row(rl, shift):
        r = i * ATT_ROWS + rl
        krow0 = jnp.clip(r - kh // 2, 0, rows_total - kh)
        off = pl.multiple_of(krow0 * GRID_W - na_start, GRID_W)
        d0 = krow0 - r + (NA_KH - 1)
        qrows = pl.ds(pl.multiple_of(rl * GRID_W, GRID_W), GRID_W)
        for g in range(NA_HEADS // HEADS_PER_MXU):
            cols = slice(g * MXU_DIM, (g + 1) * MXU_DIM)
            qs = _stack_heads(qa_ref[0, qrows, cols], hm_ref)
            s = lax.dot_general(qs, ka_ref[pl.ds(off, win), cols], nt,
                                preferred_element_type=f32)
            p, l = _softmax_rows(
                s,
                lambda m: jnp.concatenate([pb_ref[g * HEADS_PER_MXU + hh, d0 + 2 * m]
                                           for hh in range(HEADS_PER_MXU)], axis=0),
                None, shift)
            rr = jnp.dot(p, va_ref[pl.ds(off, win), cols], preferred_element_type=f32)
            o = _take_diag(rr, l, GRID_W)
            ya_ref[0, qrows, cols] = (o * sza_ref[0, qrows, cols].astype(f32)).astype(bf16)

    def sw_block(nl, shift):
        q0 = i * ATT_TILE + nl * SW_BLOCK
        w0 = jnp.clip(q0 - WINDOW, 0, seq_len - SW_KWIN)
        var = (q0 - w0) // SW_BLOCK
        qrows = pl.ds(pl.multiple_of(nl * SW_BLOCK, SW_BLOCK), SW_BLOCK)
        krows = pl.ds(pl.multiple_of(w0 - sw_start, SW_BLOCK), SW_KWIN)
        for j in range(SW_KV_HEADS):
            cols = slice(j * MXU_DIM, (j + 1) * MXU_DIM)
            qs = _stack_heads(qb_ref[0, qrows, cols], hm_ref)
            s = lax.dot_general(qs, kb_ref[krows, cols], nt, preferred_element_type=f32)
            p, l = _softmax_rows(
                s,
                lambda m: jnp.concatenate([swm_ref[var, :, m * LANES:(m + 1) * LANES]] * SW_GROUP, axis=0),
                sink_ref[j], shift)
            rr = jnp.dot(p, vb_ref[krows, cols], preferred_element_type=f32)
            o = _take_diag(rr, l, SW_BLOCK)
            yb_ref[0, qrows, cols] = (o * szb_ref[0, qrows, cols].astype(f32)).astype(bf16)

    @pl.when(bounded_ref[0] != 0)
    def _():
        for rl in range(ATT_TILE // GRID_W):
            na_row(rl, False)
        for nl in range(ATT_TILE // SW_BLOCK):
            sw_block(nl, False)

    @pl.when(bounded_ref[0] == 0)
    def _():
        def na_body(rl, carry):
            na_row(rl, True)
            return carry

        def sw_body(nl, carry):
            sw_block(nl, True)
            return carry

        lax.fori_loop(0, ATT_TILE // GRID_W, na_body, 0)
        lax.fori_loop(0, ATT_TILE // SW_BLOCK, sw_body, 0)


def _window_start(i, halo, ext, seq_len):
    return jnp.clip(i * ATT_TILE - halo, 0, seq_len - ext)


def _attention(bounded, qa, ka, va, sza, qb, kb, vb, szb, pb, swm, sink_col, hm):
    B, S, _ = qa.shape
    tq = ATT_TILE
    grid = (B, S // tq)

    tok = lambda w: pl.BlockSpec((1, tq, w), lambda b, i: (b, i, 0))

    def window(width, halo, ext):
        return pl.BlockSpec(
            (pl.Element(ext), pl.Element(width)),
            lambda b, i: (pl.multiple_of(b * S + _window_start(i, halo, ext, S), SW_BLOCK), 0))

    flat = lambda t: t.reshape(B * S, t.shape[-1])
    kern = functools.partial(_attn_kernel, seq_len=S)
    return pl.pallas_call(
        kern,
        grid=grid,
        in_specs=[
            pl.BlockSpec(memory_space=pltpu.SMEM),
            tok(NA_WIDTH), tok(NA_WIDTH), tok(SW_WIDTH), tok(SW_WIDTH),
            window(NA_WIDTH, NA_HALO, NA_EXT), window(NA_WIDTH, NA_HALO, NA_EXT),
            window(SW_WIDTH, WINDOW, SW_EXT), window(SW_WIDTH, WINDOW, SW_EXT),
            _const_spec(pb.shape), _const_spec(swm.shape), _const_spec(sink_col.shape),
            _const_spec(hm.shape),
        ],
        out_specs=[tok(NA_WIDTH), tok(SW_WIDTH)],
        out_shape=[jax.ShapeDtypeStruct((B, S, NA_WIDTH), jnp.bfloat16),
                   jax.ShapeDtypeStruct((B, S, SW_WIDTH), jnp.bfloat16)],
        compiler_params=pltpu.CompilerParams(
            dimension_semantics=("arbitrary", "arbitrary"), vmem_limit_bytes=VMEM_LIMIT),
        name="attn",
    )(bounded, qa, sza, qb, szb, flat(ka), flat(va), flat(kb), flat(vb), pb, swm, sink_col, hm)


def _merge_kernel(ya_ref, yb_ref, x_ref, g_ref, wg_ref, woa_ref, wob_ref, wo_ref, y_ref):
    f32, bf16 = jnp.float32, jnp.bfloat16
    x = x_ref[0]
    ms = jnp.mean(x * x, axis=-1, keepdims=True)
    h = (x * lax.rsqrt(ms + NORM_EPS) * g_ref[...]).astype(bf16)
    ga = jax.nn.sigmoid(jnp.dot(h, wg_ref[:, :D_MODEL], preferred_element_type=f32))
    gb = jax.nn.sigmoid(jnp.dot(h, wg_ref[:, D_MODEL:], preferred_element_type=f32))
    pa = jnp.dot(ya_ref[0], woa_ref[...], preferred_element_type=f32)
    pbm = jnp.dot(yb_ref[0], wob_ref[...], preferred_element_type=f32)
    merged = (ga * pa + gb * pbm).astype(bf16)
    y_ref[0] = x + jnp.dot(merged, wo_ref[...], preferred_element_type=f32)


def _merge(x, ya, yb, g, wg, woa, wob, wo):
    B, S, _ = x.shape
    tm = MERGE_TILE
    tok = lambda w: pl.BlockSpec((1, tm, w), lambda b, i: (b, i, 0))
    one = lambda shape: pl.BlockSpec(shape, lambda b, i: (0,) * len(shape), pipeline_mode=pl.Buffered(1))
    return pl.pallas_call(
        _merge_kernel,
        grid=(B, S // tm),
        in_specs=[tok(NA_WIDTH), tok(SW_WIDTH), tok(D_MODEL), _const_spec(g.shape),
                  one(wg.shape), one(woa.shape), one(wob.shape), one(wo.shape)],
        out_specs=tok(D_MODEL),
        out_shape=jax.ShapeDtypeStruct((B, S, D_MODEL), jnp.float32),
        compiler_params=pltpu.CompilerParams(
            dimension_semantics=("arbitrary", "arbitrary"), vmem_limit_bytes=VMEM_LIMIT),
        name="merge",
    )(ya, yb, x, g, wg, woa, wob, wo)


def _na_bias_table(rpb):
    c = np.arange(GRID_W)
    cs = np.clip(c - NA_KW // 2, 0, GRID_W - NA_KW)
    col_in = (c[None, :] >= cs[:, None]) & (c[None, :] < cs[:, None] + NA_KW)
    nt = 2 * NA_KW - 1
    t_idx = c[None, :] - c[:, None] + (NA_KW - 1)
    onehot = (t_idx[None] == np.arange(nt)[:, None, None]) & col_in[None]
    onehot = jnp.asarray(onehot.reshape(nt, GRID_W * GRID_W), dtype=jnp.float32)
    flat = (rpb.astype(jnp.float32) * LOG2_E).reshape(NA_HEADS * (2 * NA_KH - 1), nt)
    b = jnp.dot(flat, onehot, precision=lax.Precision.HIGHEST)
    b = b.reshape(NA_HEADS, 2 * NA_KH - 1, GRID_W, GRID_W)
    b = jnp.where(col_in[None, None], b, NEG_BIG)
    nd = 2 * NA_KH - 2
    return jnp.concatenate([b[:, :nd], b[:, 1:nd + 1]], axis=-1)


def _sw_mask_table():
    qi = np.arange(SW_BLOCK)[None, :, None]
    kj = np.arange(SW_KWIN)[None, None, :]
    d = np.arange(SW_SHIFTS)[:, None, None]
    band = np.abs(kj - d * SW_BLOCK - qi) <= WINDOW
    return jnp.asarray(np.where(band, 0.0, NEG_BIG), dtype=jnp.float32)


def _layer(x, p):
    S = x.shape[1]
    outs = _projection(x, p["g"], p["w_in"], p["bd"], p["qna"], p["kna"], p["qnb"], p["knb"],
                       p["cos"][:S], p["sin"][:S])
    qa, ka, va, sza, qb, kb, vb, szb = outs
    ya, yb = _attention(p["bounded"], qa, ka, va, sza, qb, kb, vb, szb, p["pb"], p["swm"], p["sink"],
                        p["hm"])
    return _merge(x, ya, yb, p["g"], p["w_gate"], p["woa"], p["wob"], p["wo"])


def _prepare(S, norm_g, w_in, qn_a, kn_a, rpb_a, qn_b, kn_b, sink_b, w_out_a, w_out_b, w_o):
    f32, bf16 = jnp.float32, jnp.bfloat16
    seg = np.arange(MXU_DIM) // HEAD_DIM
    bd = jnp.asarray(np.where(seg[:, None] == seg[None, :], 1.0 / HEAD_DIM, 0.0), dtype=bf16)
    hm = seg[None, None, :] == np.arange(HEADS_PER_MXU)[:, None, None]
    hm = jnp.asarray(np.broadcast_to(hm, (HEADS_PER_MXU, max(SW_BLOCK, GRID_W), MXU_DIM)), dtype=bf16)
    half = HEAD_DIM // 2
    inv = (np.float32(ROPE_THETA) ** (-np.arange(half, dtype=np.float32) / np.float32(half))).astype(np.float32)
    ang = np.arange(S, dtype=np.float32)[:, None] * inv[None, :]
    cos, sin = np.cos(ang), np.sin(ang)
    cos = jnp.asarray(np.tile(np.concatenate([cos, cos], axis=1), (1, LANES // HEAD_DIM)), dtype=f32)
    sin = jnp.asarray(np.tile(np.concatenate([-sin, sin], axis=1), (1, LANES // HEAD_DIM)), dtype=f32)
    sink_col = jnp.repeat((sink_b.astype(f32) * LOG2_E).reshape(SW_KV_HEADS, SW_GROUP), SW_BLOCK, axis=1)
    gains = jnp.stack([qn_a, kn_a, qn_b, kn_b]).astype(f32)
    gmax = jnp.max(jnp.abs(gains), axis=1)
    amax = lambda t: jnp.max(jnp.abs(t.astype(f32)))
    qk_bound = HEAD_DIM ** 0.5 * LOG2_E * (1.0 + 2.0 ** -6)
    bound_a = qk_bound * gmax[0] * gmax[1] + LOG2_E * amax(rpb_a)
    bound_b = jnp.maximum(qk_bound * gmax[2] * gmax[3], LOG2_E * amax(sink_b))
    bounded = (jnp.maximum(bound_a, bound_b) <= MAX_UNSHIFTED_LOGIT).astype(jnp.int32).reshape(1)
    gains = jnp.tile(gains, (1, NA_HEADS))
    w_bf = w_in.astype(bf16)
    return {
        "bounded": bounded,
        "g": norm_g.astype(f32).reshape(1, D_MODEL),
        "w_in": w_bf[:, :C_GA],
        "w_gate": w_bf[:, C_GA:],
        "bd": bd,
        "qna": gains[0:1], "kna": gains[1:2], "qnb": gains[2:3], "knb": gains[3:4, :SW_KV_WIDTH],
        "cos": cos, "sin": sin,
        "pb": _na_bias_table(rpb_a),
        "swm": _sw_mask_table(),
        "sink": sink_col.reshape(SW_KV_HEADS, SW_GROUP * SW_BLOCK, 1),
        "hm": hm,
        "woa": w_out_a.astype(bf16), "wob": w_out_b.astype(bf16), "wo": w_o.astype(bf16),
    }


def kernel(x_prompt, x_sample, norm_g, w_in, qn_a, kn_a, rpb_a, qn_b, kn_b, sink_b, w_out_a, w_out_b, w_o):
    depth = norm_g.shape[0]
    y_prompt, y_sample = x_prompt, x_sample
    S = max(x_prompt.shape[1], x_sample.shape[1])
    for l in range(depth):
        p = _prepare(S, norm_g[l], w_in[l], qn_a[l], kn_a[l], rpb_a[l], qn_b[l], kn_b[l], sink_b[l],
                     w_out_a[l], w_out_b[l], w_o[l])
        y_prompt = _layer(y_prompt, p)
        y_sample = _layer(y_sample, p)
    return (y_prompt, y_sample)
```

```python
import functools

import jax
import jax.numpy as jnp
import numpy as np
from jax import lax
from jax.experimental import pallas as pl
from jax.experimental.pallas import tpu as pltpu

D_MODEL = 1024
HEAD_DIM = 64
GRID_W = 64
NA_HEADS = 8
NA_KH = 8
NA_KW = 16
NA_WIDTH = NA_HEADS * HEAD_DIM
SW_HEADS = 8
SW_KV_HEADS = 2
SW_GROUP = SW_HEADS // SW_KV_HEADS
SW_WIDTH = SW_HEADS * HEAD_DIM
SW_KV_WIDTH = SW_KV_HEADS * HEAD_DIM
WINDOW = 128
ROPE_THETA = 10000.0
NORM_EPS = 1e-6

C_QA = 0
C_KA = C_QA + NA_WIDTH
C_VA = C_KA + NA_WIDTH
C_ZA = C_VA + NA_WIDTH
C_QB = C_ZA + NA_WIDTH
C_KB = C_QB + SW_WIDTH
C_VB = C_KB + SW_KV_WIDTH
C_ZB = C_VB + SW_KV_WIDTH
C_GA = C_ZB + SW_WIDTH
C_GB = C_GA + D_MODEL
IN_WIDTH = C_GB + D_MODEL

LANES = 128
MXU_DIM = 256
HEADS_PER_MXU = MXU_DIM // HEAD_DIM
NEG_BIG = -1e30
LOG2_E = 1.4426950408889634
MAX_UNSHIFTED_LOGIT = 32.0
VMEM_LIMIT = 56 * 1024 * 1024

PROJ_TILE = 512
ATT_TILE = 1024
ATT_ROWS = ATT_TILE // GRID_W
MERGE_TILE = 1024
NA_HALO = 256
NA_EXT = ATT_TILE + 2 * NA_HALO
SW_BLOCK = 64
SW_KWIN = 384
SW_EXT = ATT_TILE + SW_KWIN - SW_BLOCK
SW_SHIFTS = (SW_KWIN - SW_BLOCK) // SW_BLOCK + 1


def _segment_mean_sq(t, bd):
    w = t.shape[1]
    sq = (t * t).astype(jnp.bfloat16)
    outs = []
    for c in range(0, w, MXU_DIM):
        cw = min(MXU_DIM, w - c)
        outs.append(jnp.dot(sq[:, c:c + cw], bd[:cw, :cw], preferred_element_type=jnp.float32))
    return outs[0] if len(outs) == 1 else jnp.concatenate(outs, axis=1)


def _swap_halves(t):
    w = t.shape[1]
    lane = lax.broadcasted_iota(jnp.int32, (t.shape[0], LANES), 1)
    low = (lane % HEAD_DIM) < (HEAD_DIM // 2)
    outs = []
    for c in range(0, w, LANES):
        blk = t[:, c:c + LANES]
        fwd = pltpu.roll(blk, HEAD_DIM // 2, 1)
        bwd = pltpu.roll(blk, LANES - HEAD_DIM // 2, 1)
        outs.append(jnp.where(low, bwd, fwd))
    return outs[0] if len(outs) == 1 else jnp.concatenate(outs, axis=1)


def _tile_lanes(t, reps):
    return t if reps == 1 else jnp.concatenate([t] * reps, axis=1)


def _proj_kernel(x_ref, g_ref, w_ref, bd_ref, qna_ref, kna_ref, qnb_ref, knb_ref, cos_ref, sin_ref,
                 qa_o, ka_o, va_o, sza_o, qb_o, kb_o, vb_o, szb_o):
    f32, bf16 = jnp.float32, jnp.bfloat16
    x = x_ref[0]
    ms = jnp.mean(x * x, axis=-1, keepdims=True)
    h = (x * lax.rsqrt(ms + NORM_EPS) * g_ref[...]).astype(bf16)
    bd = bd_ref[...]
    scale = HEAD_DIM ** -0.5 * LOG2_E

    def proj(c0, c1):
        return jnp.dot(h, w_ref[:, c0:c1], preferred_element_type=f32)

    def head_norm(t, gain):
        return t * lax.rsqrt(_segment_mean_sq(t, bd) + NORM_EPS) * gain

    def rotary(t):
        reps = t.shape[1] // LANES
        cos = _tile_lanes(cos_ref[...], reps)
        sin = _tile_lanes(sin_ref[...], reps)
        return t * cos + _swap_halves(t) * sin

    qa_o[0] = (head_norm(proj(C_QA, C_KA), qna_ref[...]) * scale).astype(bf16)
    ka_o[0] = head_norm(proj(C_KA, C_VA), kna_ref[...]).astype(bf16)
    va_o[0] = proj(C_VA, C_ZA).astype(bf16)
    sza_o[0] = jax.nn.silu(proj(C_ZA, C_QB)).astype(bf16)
    qb_o[0] = (rotary(head_norm(proj(C_QB, C_KB), qnb_ref[...])) * scale).astype(bf16)

    kvb = proj(C_KB, C_ZB)
    kb = rotary(head_norm(kvb[:, :SW_KV_WIDTH], knb_ref[...]))
    vb = kvb[:, SW_KV_WIDTH:]
    lane = lax.broadcasted_iota(jnp.int32, kb.shape, 1)
    first = lane < HEAD_DIM

    def replicate(t):
        r = pltpu.roll(t, HEAD_DIM, 1)
        h0 = jnp.where(first, t, r)
        h1 = jnp.where(first, r, t)
        return jnp.concatenate([h0, h0, h1, h1], axis=1)

    kb_o[0] = replicate(kb).astype(bf16)
    vb_o[0] = replicate(vb).astype(bf16)
    szb_o[0] = jax.nn.silu(proj(C_ZB, C_GA)).astype(bf16)


def _const_spec(shape):
    nd = len(shape)
    return pl.BlockSpec(shape, lambda b, i: (0,) * nd)


def _projection(x, g, w_bf, bd, qna, kna, qnb, knb, cos, sin):
    B, S, _ = x.shape
    tm = PROJ_TILE
    grid = (B, S // tm)
    tok = lambda w: pl.BlockSpec((1, tm, w), lambda b, i: (b, i, 0))
    out_widths = [NA_WIDTH, NA_WIDTH, NA_WIDTH, NA_WIDTH, SW_WIDTH, SW_WIDTH, SW_WIDTH, SW_WIDTH]
    return pl.pallas_call(
        _proj_kernel,
        grid=grid,
        in_specs=[
            tok(D_MODEL),
            _const_spec((1, D_MODEL)),
            pl.BlockSpec((pl.Element(D_MODEL), pl.Element(C_GA)), lambda b, i: (0, 0)),
            _const_spec((MXU_DIM, MXU_DIM)),
            _const_spec((1, NA_WIDTH)),
            _const_spec((1, NA_WIDTH)),
            _const_spec((1, SW_WIDTH)),
            _const_spec((1, SW_KV_WIDTH)),
            pl.BlockSpec((tm, LANES), lambda b, i: (i, 0)),
            pl.BlockSpec((tm, LANES), lambda b, i: (i, 0)),
        ],
        out_specs=[tok(w) for w in out_widths],
        out_shape=[jax.ShapeDtypeStruct((B, S, w), jnp.bfloat16) for w in out_widths],
        compiler_params=pltpu.CompilerParams(
            dimension_semantics=("arbitrary", "arbitrary"), vmem_limit_bytes=VMEM_LIMIT),
        name="proj",
    )(x, g, w_bf, bd, qna, kna, qnb, knb, cos, sin)


def _stack_heads(q, hm_ref):
    rows = q.shape[0]
    return jnp.concatenate([q * hm_ref[hh, :rows, :] for hh in range(HEADS_PER_MXU)], axis=0)


def _take_diag(r, l, rows):
    lane = lax.broadcasted_iota(jnp.int32, (rows, MXU_DIM), 1) // HEAD_DIM
    last = HEADS_PER_MXU - 1
    num = r[last * rows:]
    den = jnp.broadcast_to(l[last * rows:], (rows, MXU_DIM))
    for hh in range(last - 1, -1, -1):
        num = jnp.where(lane == hh, r[hh * rows:(hh + 1) * rows], num)
        den = jnp.where(lane == hh, l[hh * rows:(hh + 1) * rows], den)
    return num / den


def _softmax_rows(s, add, floor, shift):
    n = s.shape[1] // LANES
    sb = [s[:, m * LANES:(m + 1) * LANES] + add(m) for m in range(n)]
    if shift:
        mx = jnp.max(functools.reduce(jnp.maximum, sb), axis=-1, keepdims=True)
        if floor is not None:
            mx = jnp.maximum(mx, floor)
            floor = floor - mx
        sb = [t - mx for t in sb]
    p = [jnp.exp2(t) for t in sb]
    l = jnp.sum(functools.reduce(jnp.add, p), axis=-1, keepdims=True)
    if floor is not None:
        l = l + jnp.exp2(floor)
    return jnp.concatenate(p, axis=1).astype(jnp.bfloat16), l


def _attn_kernel(bounded_ref, qa_ref, sza_ref, qb_ref, szb_ref, ka_ref, va_ref, kb_ref, vb_ref,
                 pb_ref, swm_ref, sink_ref, hm_ref, ya_ref, yb_ref, *, seq_len):
    f32, bf16 = jnp.float32, jnp.bfloat16
    i = pl.program_id(1)
    nt = (((1,), (1,)), ((), ()))
    rows_total = seq_len // GRID_W
    kh = min(NA_KH, rows_total)
    win = kh * GRID_W
    na_start = _window_start(i, NA_HALO, NA_EXT, seq_len)
    sw_start = _window_start(i, WINDOW, SW_EXT, seq_len)

    def na_row(rl, shift):
        r = i * ATT_ROWS + rl
        krow0 = jnp.clip(r - kh // 2, 0, rows_total - kh)
        off = pl.multiple_of(krow0 * GRID_W - na_start, GRID_W)
        d0 = krow0 - r + (NA_KH - 1)
        qrows = pl.ds(pl.multiple_of(rl * GRID_W, GRID_W), GRID_W)
        for g in range(NA_HEADS // HEADS_PER_MXU):
            cols = slice(g * MXU_DIM, (g + 1) * MXU_DIM)
            qs = _stack_heads(qa_ref[0, qrows, cols], hm_ref)
            s = lax.dot_general(qs, ka_ref[pl.ds(off, win), cols], nt,
                                preferred_element_type=f32)
            p, l = _softmax_rows(
                s,
                lambda m: jnp.concatenate([pb_ref[g * HEADS_PER_MXU + hh, d0 + 2 * m]
                                           for hh in range(HEADS_PER_MXU)], axis=0),
                None, shift)
            rr = jnp.dot(p, va_ref[pl.ds(off, win), cols], preferred_element_type=f32)
            o = _take_diag(rr, l, GRID_W)
            ya_ref[0, qrows, cols] = (o * sza_ref[0, qrows, cols].astype(f32)).astype(bf16)

    def sw_block(nl, shift):
        q0 = i * ATT_TILE + nl * SW_BLOCK
        w0 = jnp.clip(q0 - WINDOW, 0, seq_len - SW_KWIN)
        var = (q0 - w0) // SW_BLOCK
        qrows = pl.ds(pl.multiple_of(nl * SW_BLOCK, SW_BLOCK), SW_BLOCK)
        krows = pl.ds(pl.multiple_of(w0 - sw_start, SW_BLOCK), SW_KWIN)
        for j in range(SW_KV_HEADS):
            cols = slice(j * MXU_DIM, (j + 1) * MXU_DIM)
            qs = _stack_heads(qb_ref[0, qrows, cols], hm_ref)
            s = lax.dot_general(qs, kb_ref[krows, cols], nt, preferred_element_type=f32)
            p, l = _softmax_rows(
                s,
                lambda m: jnp.concatenate([swm_ref[var, :, m * LANES:(m + 1) * LANES]] * SW_GROUP, axis=0),
                sink_ref[j], shift)
            rr = jnp.dot(p, vb_ref[krows, cols], preferred_element_type=f32)
            o = _take_diag(rr, l, SW_BLOCK)
            yb_ref[0, qrows, cols] = (o * szb_ref[0, qrows, cols].astype(f32)).astype(bf16)

    @pl.when(bounded_ref[0] != 0)
    def _():
        for rl in range(ATT_TILE // GRID_W):
            na_row(rl, False)
        for nl in range(ATT_TILE // SW_BLOCK):
            sw_block(nl, False)

    @pl.when(bounded_ref[0] == 0)
    def _():
        def na_body(rl, carry):
            na_row(rl, True)
            return carry

        def sw_body(nl, carry):
            sw_block(nl, True)
            return carry

        lax.fori_loop(0, ATT_TILE // GRID_W, na_body, 0)
        lax.fori_loop(0, ATT_TILE // SW_BLOCK, sw_body, 0)


def _window_start(i, halo, ext, seq_len):
    return jnp.clip(i * ATT_TILE - halo, 0, seq_len - ext)


def _attention(bounded, qa, ka, va, sza, qb, kb, vb, szb, pb, swm, sink_col, hm):
    B, S, _ = qa.shape
    tq = ATT_TILE
    grid = (B, S // tq)

    tok = lambda w: pl.BlockSpec((1, tq, w), lambda b, i: (b, i, 0))

    def window(width, halo, ext):
        return pl.BlockSpec(
            (pl.Element(ext), pl.Element(width)),
            lambda b, i: (pl.multiple_of(b * S + _window_start(i, halo, ext, S), SW_BLOCK), 0))

    flat = lambda t: t.reshape(B * S, t.shape[-1])
    kern = functools.partial(_attn_kernel, seq_len=S)
    return pl.pallas_call(
        kern,
        grid=grid,
        in_specs=[
            pl.BlockSpec(memory_space=pltpu.SMEM),
            tok(NA_WIDTH), tok(NA_WIDTH), tok(SW_WIDTH), tok(SW_WIDTH),
            window(NA_WIDTH, NA_HALO, NA_EXT), window(NA_WIDTH, NA_HALO, NA_EXT),
            window(SW_WIDTH, WINDOW, SW_EXT), window(SW_WIDTH, WINDOW, SW_EXT),
            _const_spec(pb.shape), _const_spec(swm.shape), _const_spec(sink_col.shape),
            _const_spec(hm.shape),
        ],
        out_specs=[tok(NA_WIDTH), tok(SW_WIDTH)],
        out_shape=[jax.ShapeDtypeStruct((B, S, NA_WIDTH), jnp.bfloat16),
                   jax.ShapeDtypeStruct((B, S, SW_WIDTH), jnp.bfloat16)],
        compiler_params=pltpu.CompilerParams(
            dimension_semantics=("arbitrary", "arbitrary"), vmem_limit_bytes=VMEM_LIMIT),
        name="attn",
    )(bounded, qa, sza, qb, szb, flat(ka), flat(va), flat(kb), flat(vb), pb, swm, sink_col, hm)


def _merge_kernel(ya_ref, yb_ref, x_ref, g_ref, wg_ref, woa_ref, wob_ref, wo_ref, y_ref):
    f32, bf16 = jnp.float32, jnp.bfloat16
    x = x_ref[0]
    ms = jnp.mean(x * x, axis=-1, keepdims=True)
    h = (x * lax.rsqrt(ms + NORM_EPS) * g_ref[...]).astype(bf16)
    ga = jax.nn.sigmoid(jnp.dot(h, wg_ref[:, :D_MODEL], preferred_element_type=f32))
    gb = jax.nn.sigmoid(jnp.dot(h, wg_ref[:, D_MODEL:], preferred_element_type=f32))
    pa = jnp.dot(ya_ref[0], woa_ref[...], preferred_element_type=f32)
    pbm = jnp.dot(yb_ref[0], wob_ref[...], preferred_element_type=f32)
    merged = (ga * pa + gb * pbm).astype(bf16)
    y_ref[0] = x + jnp.dot(merged, wo_ref[...], preferred_element_type=f32)


def _merge(x, ya, yb, g, wg, woa, wob, wo):
    B, S, _ = x.shape
    tm = MERGE_TILE
    tok = lambda w: pl.BlockSpec((1, tm, w), lambda b, i: (b, i, 0))
    one = lambda shape: pl.BlockSpec(shape, lambda b, i: (0,) * len(shape), pipeline_mode=pl.Buffered(1))
    return pl.pallas_call(
        _merge_kernel,
        grid=(B, S // tm),
        in_specs=[tok(NA_WIDTH), tok(SW_WIDTH), tok(D_MODEL), _const_spec(g.shape),
                  pl.BlockSpec((pl.Element(D_MODEL), pl.Element(2 * D_MODEL)), lambda b, i: (0, C_GA),
                               pipeline_mode=pl.Buffered(1)),
                  one(woa.shape), one(wob.shape), one(wo.shape)],
        out_specs=tok(D_MODEL),
        out_shape=jax.ShapeDtypeStruct((B, S, D_MODEL), jnp.float32),
        compiler_params=pltpu.CompilerParams(
            dimension_semantics=("arbitrary", "arbitrary"), vmem_limit_bytes=VMEM_LIMIT),
        name="merge",
    )(ya, yb, x, g, wg, woa, wob, wo)


def _na_bias_table(rpb):
    c = np.arange(GRID_W)
    cs = np.clip(c - NA_KW // 2, 0, GRID_W - NA_KW)
    col_in = (c[None, :] >= cs[:, None]) & (c[None, :] < cs[:, None] + NA_KW)
    nt = 2 * NA_KW - 1
    t_idx = c[None, :] - c[:, None] + (NA_KW - 1)
    onehot = (t_idx[None] == np.arange(nt)[:, None, None]) & col_in[None]
    onehot = jnp.asarray(onehot.reshape(nt, GRID_W * GRID_W), dtype=jnp.float32)
    flat = (rpb.astype(jnp.float32) * LOG2_E).reshape(NA_HEADS * (2 * NA_KH - 1), nt)
    b = jnp.dot(flat, onehot, precision=lax.Precision.HIGHEST)
    b = b.reshape(NA_HEADS, 2 * NA_KH - 1, GRID_W, GRID_W)
    b = jnp.where(col_in[None, None], b, NEG_BIG)
    nd = 2 * NA_KH - 2
    return jnp.concatenate([b[:, :nd], b[:, 1:nd + 1]], axis=-1)


def _sw_mask_table():
    qi = np.arange(SW_BLOCK)[None, :, None]
    kj = np.arange(SW_KWIN)[None, None, :]
    d = np.arange(SW_SHIFTS)[:, None, None]
    band = np.abs(kj - d * SW_BLOCK - qi) <= WINDOW
    return jnp.asarray(np.where(band, 0.0, NEG_BIG), dtype=jnp.float32)


def _layer(x, p):
    S = x.shape[1]
    outs = _projection(x, p["g"], p["w_in"], p["bd"], p["qna"], p["kna"], p["qnb"], p["knb"],
                       p["cos"][:S], p["sin"][:S])
    qa, ka, va, sza, qb, kb, vb, szb = outs
    ya, yb = _attention(p["bounded"], qa, ka, va, sza, qb, kb, vb, szb, p["pb"], p["swm"], p["sink"],
                        p["hm"])
    return _merge(x, ya, yb, p["g"], p["w_gate"], p["woa"], p["wob"], p["wo"])


def _prepare(S, norm_g, w_in, qn_a, kn_a, rpb_a, qn_b, kn_b, sink_b, w_out_a, w_out_b, w_o):
    f32, bf16 = jnp.float32, jnp.bfloat16
    seg = np.arange(MXU_DIM) // HEAD_DIM
    bd = jnp.asarray(np.where(seg[:, None] == seg[None, :], 1.0 / HEAD_DIM, 0.0), dtype=bf16)
    hm = seg[None, None, :] == np.arange(HEADS_PER_MXU)[:, None, None]
    hm = jnp.asarray(np.broadcast_to(hm, (HEADS_PER_MXU, max(SW_BLOCK, GRID_W), MXU_DIM)), dtype=bf16)
    half = HEAD_DIM // 2
    inv = (np.float32(ROPE_THETA) ** (-np.arange(half, dtype=np.float32) / np.float32(half))).astype(np.float32)
    ang = np.arange(S, dtype=np.float32)[:, None] * inv[None, :]
    cos, sin = np.cos(ang), np.sin(ang)
    cos = jnp.asarray(np.tile(np.concatenate([cos, cos], axis=1), (1, LANES // HEAD_DIM)), dtype=f32)
    sin = jnp.asarray(np.tile(np.concatenate([-sin, sin], axis=1), (1, LANES // HEAD_DIM)), dtype=f32)
    sink_col = jnp.repeat((sink_b.astype(f32) * LOG2_E).reshape(SW_KV_HEADS, SW_GROUP), SW_BLOCK, axis=1)
    gains = jnp.stack([qn_a, kn_a, qn_b, kn_b]).astype(f32)
    gmax = jnp.max(jnp.abs(gains), axis=1)
    amax = lambda t: jnp.max(jnp.abs(t.astype(f32)))
    qk_bound = HEAD_DIM ** 0.5 * LOG2_E * (1.0 + 2.0 ** -6)
    bound_a = qk_bound * gmax[0] * gmax[1] + LOG2_E * amax(rpb_a)
    bound_b = jnp.maximum(qk_bound * gmax[2] * gmax[3], LOG2_E * amax(sink_b))
    bounded = (jnp.maximum(bound_a, bound_b) <= MAX_UNSHIFTED_LOGIT).astype(jnp.int32).reshape(1)
    gains = jnp.tile(gains, (1, NA_HEADS))
    w_bf = w_in.astype(bf16)
    return {
        "bounded": bounded,
        "g": norm_g.astype(f32).reshape(1, D_MODEL),
        "w_in": w_bf,
        "w_gate": w_bf,
        "bd": bd,
        "qna": gains[0:1], "kna": gains[1:2], "qnb": gains[2:3], "knb": gains[3:4, :SW_KV_WIDTH],
        "cos": cos, "sin": sin,
        "pb": _na_bias_table(rpb_a),
        "swm": _sw_mask_table(),
        "sink": sink_col.reshape(SW_KV_HEADS, SW_GROUP * SW_BLOCK, 1),
        "hm": hm,
        "woa": w_out_a.astype(bf16), "wob": w_out_b.astype(bf16), "wo": w_o.astype(bf16),
    }


def kernel(x_prompt, x_sample, norm_g, w_in, qn_a, kn_a, rpb_a, qn_b, kn_b, sink_b, w_out_a, w_out_b, w_o):
    depth = norm_g.shape[0]
    y_prompt, y_sample = x_prompt, x_sample
    S = max(x_prompt.shape[1], x_sample.shape[1])
    for l in range(depth):
        p = _prepare(S, norm_g[l], w_in[l], qn_a[l], kn_a[l], rpb_a[l], qn_b[l], kn_b[l], sink_b[l],
                     w_out_a[l], w_out_b[l], w_o[l])
        y_prompt = _layer(y_prompt, p)
        y_sample = _layer(y_sample, p)
    return (y_prompt, y_sample)
```
